```python
import jax, jax.numpy as jnp
from jax import lax
import numpy as np

D_MODEL = 1024
BATCH = 4
SEQ = 8192
DEPTH = 2
DEC_BATCH = 32
DEC_SEQ = 2048
PAST_LEN = 128

GRID_W = 64
ROPE_THETA = 10000.0
NORM_EPS = 1e-6

RET_HEADS = 4
RET_QK_DIM = 256
RET_V_DIM = 512
RET_CHUNK = 128
RET_QK_WIDTH = RET_HEADS * RET_QK_DIM
RET_V_WIDTH = RET_HEADS * RET_V_DIM
RET_IN_WIDTH = 2 * RET_QK_WIDTH + 2 * RET_V_WIDTH

ATTN_Q_HEADS = 8
ATTN_KV_HEADS = 2
ATTN_HEAD_DIM = 128
ATTN_GROUP = ATTN_Q_HEADS // ATTN_KV_HEADS
ATTN_Q_BLOCK = 128
ATTN_IN_WIDTH = (ATTN_Q_HEADS + 2 * ATTN_KV_HEADS) * ATTN_HEAD_DIM

MOE_GROUPS = 4
MOE_EXPERTS_PER_GROUP = 8
MOE_EXPERTS = MOE_GROUPS * MOE_EXPERTS_PER_GROUP
MOE_TOPK = 2
MOE_FF = 512
MOE_BLOCK = 256

N_MIXERS = 2
N_RET_LAYERS = (DEPTH + 1) // 2
N_ATTN_LAYERS = DEPTH // 2

kernel_name = 'hybrid_retention_gqa_hmoe_encoder'


def _rms_norm(x, gain):
    xf = x.astype(jnp.float32)
    y = xf * lax.rsqrt(jnp.mean(xf * xf, axis=-1, keepdims=True) + NORM_EPS)
    return (y * gain.astype(jnp.float32)).astype(x.dtype)


def _axial_rope_tables(n_tok, head_dim):
    rows = n_tok // GRID_W
    row = jnp.repeat(jnp.arange(rows, dtype=jnp.float32), GRID_W)
    col = jnp.tile(jnp.arange(GRID_W, dtype=jnp.float32), rows)
    half = head_dim // 2
    inv = ROPE_THETA ** (-jnp.arange(0, half, 2, dtype=jnp.float32) / half)
    ang_r = row[:, None] * inv[None, :]
    ang_c = col[:, None] * inv[None, :]
    ang = jnp.concatenate([ang_r, ang_r, ang_c, ang_c], axis=-1)
    return jnp.cos(ang), jnp.sin(ang)


def _rot_half(u):
    u1, u2 = jnp.split(u, 2, axis=-1)
    return jnp.concatenate([-u2, u1], axis=-1)


def _apply_axial_rope(x, cos, sin):
    xr, xc = jnp.split(x, 2, axis=-1)
    xrot = jnp.concatenate([_rot_half(xr), _rot_half(xc)], axis=-1)
    return (x * cos[:, None, :] + xrot * sin[:, None, :]).astype(x.dtype)


def _retention(h, w_in, decay_logit, w_out, cos, sin):
    B, S, _ = h.shape
    C = RET_CHUNK
    NC = S // C
    H, dk, dv = RET_HEADS, RET_QK_DIM, RET_V_DIM
    proj = h @ w_in
    q, k, v, g = jnp.split(proj, [RET_QK_WIDTH, 2 * RET_QK_WIDTH, 2 * RET_QK_WIDTH + RET_V_WIDTH], axis=-1)
    q = _apply_axial_rope(q.reshape(B, S, H, dk), cos, sin).astype(jnp.float32)
    k = _apply_axial_rope(k.reshape(B, S, H, dk), cos, sin).astype(jnp.float32) * (dk ** -0.5)
    v = v.reshape(B, S, H, dv).astype(jnp.float32)

    log_gamma = jax.nn.log_sigmoid(decay_logit.astype(jnp.float32))
    lg_f, lg_b = log_gamma[0], log_gamma[1]
    pos = jnp.arange(C, dtype=jnp.float32)
    dist = pos[:, None] - pos[None, :]
    inner = (jnp.where(dist >= 0, jnp.exp(lg_f[:, None, None] * jnp.maximum(dist, 0.0)), 0.0)
             + jnp.where(dist <= 0, jnp.exp(lg_b[:, None, None] * jnp.maximum(-dist, 0.0)), 0.0))

    qc = q.reshape(B, NC, C, H, dk)
    kc = k.reshape(B, NC, C, H, dk)
    vc = v.reshape(B, NC, C, H, dv)
    scores = jnp.einsum('bnihd,bnjhd->bnhij', qc, kc) * inner[None, None]
    o = jnp.einsum('bnhij,bnjhe->bnihe', scores, vc)

    q_dec_f = jnp.exp((pos + 1.0)[:, None] * lg_f[None, :])
    k_dec_f = jnp.exp((C - 1.0 - pos)[:, None] * lg_f[None, :])
    chunk_f = jnp.exp(lg_f * C)
    q_dec_b = jnp.exp((C - pos)[:, None] * lg_b[None, :])
    k_dec_b = jnp.exp(pos[:, None] * lg_b[None, :])
    chunk_b = jnp.exp(lg_b * C)

    xs = (jnp.moveaxis(qc, 1, 0), jnp.moveaxis(kc, 1, 0), jnp.moveaxis(vc, 1, 0))
    state0 = jnp.zeros((B, H, dk, dv), jnp.float32)

    def make_step(q_dec, k_dec, chunk_dec):
        def step(state, inp):
            qn, kn, vn = inp
            out = jnp.einsum('bihd,bhde->bihe', qn * q_dec[None, :, :, None], state)
            new = state * chunk_dec[None, :, None, None] + jnp.einsum('bjhd,bjhe->bhde', kn * k_dec[None, :, :, None], vn)
            return new, out
        return step

    _, o_fwd = lax.scan(make_step(q_dec_f, k_dec_f, chunk_f), state0, xs)
    _, o_bwd = lax.scan(make_step(q_dec_b, k_dec_b, chunk_b), state0, xs, reverse=True)
    o = o + jnp.moveaxis(o_fwd, 0, 1) + jnp.moveaxis(o_bwd, 0, 1)
    o = o.reshape(B, S, H, dv)
    o = o * lax.rsqrt(jnp.mean(o * o, axis=-1, keepdims=True) + NORM_EPS)
    y = jax.nn.silu(g.astype(jnp.float32)) * o.reshape(B, S, RET_V_WIDTH)
    return y.astype(h.dtype) @ w_out


def _gqa(h, w_in, q_gain, k_gain, w_out, cos, sin):
    B, S, _ = h.shape
    d = ATTN_HEAD_DIM
    proj = h @ w_in
    q, k, v = jnp.split(proj, [ATTN_Q_HEADS * d, (ATTN_Q_HEADS + ATTN_KV_HEADS) * d], axis=-1)
    q = _apply_axial_rope(_rms_norm(q.reshape(B, S, ATTN_Q_HEADS, d), q_gain), cos, sin)
    k = _apply_axial_rope(_rms_norm(k.reshape(B, S, ATTN_KV_HEADS, d), k_gain), cos, sin)
    v = v.reshape(B, S, ATTN_KV_HEADS, d)
    NB = S // ATTN_Q_BLOCK
    qb = q.reshape(B, NB, ATTN_Q_BLOCK, ATTN_KV_HEADS, ATTN_GROUP, d).transpose(1, 0, 2, 3, 4, 5)
    scale = d ** -0.5

    def block(qblk):
        s = jnp.einsum('bqkgd,bskd->bkgqs', qblk, k).astype(jnp.float32) * scale
        p = jax.nn.softmax(s, axis=-1).astype(v.dtype)
        return jnp.einsum('bkgqs,bskd->bqkgd', p, v)

    o = lax.map(block, qb)
    o = o.transpose(1, 0, 2, 3, 4, 5).reshape(B, S, ATTN_Q_HEADS * d)
    return o @ w_out


def _hier_moe(h, w_group, b_group, w_expert, b_expert, w_gate_up, w_down):
    B, S, D = h.shape
    T = B * S
    xf = h.reshape(T, D)
    group_prob = jax.nn.softmax((xf @ w_group).astype(jnp.float32) + b_group.astype(jnp.float32), axis=-1)
    g_sel = jnp.argmax(group_prob, axis=-1)
    g_w = jnp.max(group_prob, axis=-1)
    exp_logits = ((xf @ w_expert).astype(jnp.float32) + b_expert.astype(jnp.float32)).reshape(T, MOE_GROUPS, MOE_EXPERTS_PER_GROUP)
    in_group = exp_logits[jnp.arange(T), g_sel]
    top_v, top_i = lax.top_k(in_group, MOE_TOPK)
    top_w = jax.nn.softmax(top_v, axis=-1) * g_w[:, None]
    expert_id = g_sel[:, None] * MOE_EXPERTS_PER_GROUP + top_i

    A = T * MOE_TOPK
    n_blocks = -(-(A + MOE_EXPERTS * (MOE_BLOCK - 1)) // MOE_BLOCK)
    P = n_blocks * MOE_BLOCK
    flat_e = expert_id.reshape(A).astype(jnp.int32)
    flat_t = (jnp.arange(A, dtype=jnp.int32) // MOE_TOPK)
    flat_w = top_w.reshape(A)
    order = jnp.argsort(flat_e)
    sorted_e = flat_e[order]
    counts = jnp.bincount(flat_e, length=MOE_EXPERTS)
    padded = ((counts + MOE_BLOCK - 1) // MOE_BLOCK) * MOE_BLOCK
    starts = jnp.cumsum(counts) - counts
    pends = jnp.cumsum(padded)
    pstarts = pends - padded
    dest = pstarts[sorted_e] + (jnp.arange(A, dtype=jnp.int32) - starts[sorted_e])
    slot_tok = jnp.zeros((P,), jnp.int32).at[dest].set(flat_t[order])
    slot_w = jnp.zeros((P,), jnp.float32).at[dest].set(flat_w[order])
    block_start = jnp.arange(n_blocks, dtype=pends.dtype) * MOE_BLOCK
    block_e = jnp.minimum(jnp.searchsorted(pends, block_start, side='right'), MOE_EXPERTS - 1)

    def run_block(args):
        tok, e = args
        xb = xf[tok]
        gate, up = jnp.split(xb @ w_gate_up[e], 2, axis=-1)
        return (jax.nn.silu(gate) * up) @ w_down[e]

    yb = lax.map(run_block, (slot_tok.reshape(n_blocks, MOE_BLOCK), block_e)).reshape(P, D)
    out = jnp.zeros((T, D), jnp.float32).at[slot_tok].add(yb.astype(jnp.float32) * slot_w[:, None])
    return out.astype(h.dtype).reshape(B, S, D)


def _trunk(x, ln_mix, ret_w_in, ret_decay_logit, ret_w_out, attn_w_in, attn_q_gain, attn_k_gain,
           attn_w_out, ln_ffn, moe_w_group, moe_b_group, moe_w_expert, moe_b_expert,
           moe_w_gate_up, moe_w_down, ln_final):
    S = x.shape[1]
    cos_r, sin_r = _axial_rope_tables(S, RET_QK_DIM)
    cos_a, sin_a = _axial_rope_tables(S, ATTN_HEAD_DIM)
    h = x
    for i in range(DEPTH):
        hn = _rms_norm(h, ln_mix[i])
        j = i // N_MIXERS
        if i % N_MIXERS == 0:
            h = h + _retention(hn, ret_w_in[j], ret_decay_logit[j], ret_w_out[j], cos_r, sin_r)
        else:
            h = h + _gqa(hn, attn_w_in[j], attn_q_gain[j], attn_k_gain[j], attn_w_out[j], cos_a, sin_a)
        h = h + _hier_moe(_rms_norm(h, ln_ffn[i]), moe_w_group[i], moe_b_group[i], moe_w_expert[i],
                          moe_b_expert[i], moe_w_gate_up[i], moe_w_down[i])
    return _rms_norm(h, ln_final)


def setup_inputs(seed: int = 0) -> dict:
    key = jax.random.key(seed)
    ks = jax.random.split(key, 20)
    f32 = jnp.float32
    nrm = lambda k, shape, s: jax.random.normal(k, shape, f32) * s
    base_logit = np.log(2.0 ** (5.0 + np.arange(RET_HEADS)) - 1.0).astype(np.float32)
    decay = jnp.asarray(base_logit)[None, None, :] + nrm(ks[4], (N_RET_LAYERS, 2, RET_HEADS), 0.1)
    return {
        'x_prompt': nrm(ks[0], (BATCH, SEQ, D_MODEL), 1.0),
        'x_sample': nrm(ks[1], (DEC_BATCH, DEC_SEQ, D_MODEL), 1.0),
        'ln_mix': 1.0 + nrm(ks[2], (DEPTH, D_MODEL), 0.02),
        'ret_w_in': nrm(ks[3], (N_RET_LAYERS, D_MODEL, RET_IN_WIDTH), D_MODEL ** -0.5),
        'ret_decay_logit': decay,
        'ret_w_out': nrm(ks[5], (N_RET_LAYERS, RET_V_WIDTH, D_MODEL), RET_V_WIDTH ** -0.5),
        'attn_w_in': nrm(ks[6], (N_ATTN_LAYERS, D_MODEL, ATTN_IN_WIDTH), D_MODEL ** -0.5),
        'attn_q_gain': 1.0 + nrm(ks[7], (N_ATTN_LAYERS, ATTN_HEAD_DIM), 0.02),
        'attn_k_gain': 1.0 + nrm(ks[8], (N_ATTN_LAYERS, ATTN_HEAD_DIM), 0.02),
        'attn_w_out': nrm(ks[9], (N_ATTN_LAYERS, ATTN_Q_HEADS * ATTN_HEAD_DIM, D_MODEL), (ATTN_Q_HEADS * ATTN_HEAD_DIM) ** -0.5),
        'ln_ffn': 1.0 + nrm(ks[10], (DEPTH, D_MODEL), 0.02),
        'moe_w_group': nrm(ks[11], (DEPTH, D_MODEL, MOE_GROUPS), D_MODEL ** -0.5),
        'moe_b_group': nrm(ks[12], (DEPTH, MOE_GROUPS), 0.01),
        'moe_w_expert': nrm(ks[13], (DEPTH, D_MODEL, MOE_EXPERTS), D_MODEL ** -0.5),
        'moe_b_expert': nrm(ks[14], (DEPTH, MOE_EXPERTS), 0.01),
        'moe_w_gate_up': nrm(ks[15], (DEPTH, MOE_EXPERTS, D_MODEL, 2 * MOE_FF), D_MODEL ** -0.5),
        'moe_w_down': nrm(ks[16], (DEPTH, MOE_EXPERTS, MOE_FF, D_MODEL), MOE_FF ** -0.5),
        'ln_final': 1.0 + nrm(ks[17], (D_MODEL,), 0.02),
    }


def reference(x_prompt, x_sample, ln_mix, ret_w_in, ret_decay_logit, ret_w_out, attn_w_in, attn_q_gain,
              attn_k_gain, attn_w_out, ln_ffn, moe_w_group, moe_b_group, moe_w_expert, moe_b_expert,
              moe_w_gate_up, moe_w_down, ln_final):
    y_prompt = _trunk(x_prompt, ln_mix, ret_w_in, ret_decay_logit, ret_w_out, attn_w_in, attn_q_gain,
                      attn_k_gain, attn_w_out, ln_ffn, moe_w_group, moe_b_group, moe_w_expert, moe_b_expert,
                      moe_w_gate_up, moe_w_down, ln_final)
    y_sample = _trunk(x_sample, ln_mix, ret_w_in, ret_decay_logit, ret_w_out, attn_w_in, attn_q_gain,
                      attn_k_gain, attn_w_out, ln_ffn, moe_w_group, moe_b_group, moe_w_expert, moe_b_expert,
                      moe_w_gate_up, moe_w_down, ln_final)
    return (y_prompt, y_sample)
```

```python
import functools

import jax
import jax.numpy as jnp
from jax import lax
from jax.experimental import pallas as pl
from jax.experimental.pallas import tpu as pltpu

F32 = jnp.float32
BF16 = jnp.bfloat16

D_MODEL = 1024
GRID_W = 64
ROPE_THETA = 10000.0
NORM_EPS = 1e-6

RET_HEADS = 4
RET_QK_DIM = 256
RET_V_DIM = 512
RET_CHUNK = 128
RET_QK_WIDTH = RET_HEADS * RET_QK_DIM
RET_V_WIDTH = RET_HEADS * RET_V_DIM

ATTN_Q_HEADS = 8
ATTN_KV_HEADS = 2
ATTN_HEAD_DIM = 128
ATTN_GROUP = ATTN_Q_HEADS // ATTN_KV_HEADS
ATTN_IN_WIDTH = (ATTN_Q_HEADS + 2 * ATTN_KV_HEADS) * ATTN_HEAD_DIM

MOE_GROUPS = 4
MOE_EXPERTS_PER_GROUP = 8
MOE_EXPERTS = MOE_GROUPS * MOE_EXPERTS_PER_GROUP
MOE_TOPK = 2
MOE_FF = 512
MOE_BLOCK = 256

LANES = 128
VMEM_LIMIT = 56 * 1024 * 1024


def _cparams(*sem):
    return pltpu.CompilerParams(dimension_semantics=sem, vmem_limit_bytes=VMEM_LIMIT)


def _in_proj_body(combine, *refs):
    if combine:
        h_ref, y0_ref, y1_ref, rp_ref, g_ref, w_ref, hout_ref, o_ref, xn_ref = refs
    else:
        h_ref, g_ref, w_ref, o_ref, xn_ref = refs

    @pl.when(pl.program_id(1) == 0)
    def _():
        h = h_ref[...]
        if combine:
            rp = rp_ref[...]
            h = h + rp[:, 2:3] * y0_ref[...].astype(F32) + rp[:, 3:4] * y1_ref[...].astype(F32)
            hout_ref[...] = h
        xn = h * lax.rsqrt(jnp.mean(h * h, axis=-1, keepdims=True) + NORM_EPS) * g_ref[...]
        xn_ref[...] = xn.astype(BF16)

    o_ref[...] = jnp.dot(xn_ref[...], w_ref[...], preferred_element_type=F32).astype(o_ref.dtype)


def _in_proj(h, gain, w, comb=None, tm=1024, tn=1536):
    T, D = h.shape
    N = w.shape[1]
    tm = min(tm, T)
    tn = min(tn, N)
    row = lambda i, j: (i, 0)
    in_specs = [pl.BlockSpec((tm, D), row)]
    args = [h]
    if comb is not None:
        y0, y1, rp = comb
        in_specs += [pl.BlockSpec((tm, D), row), pl.BlockSpec((tm, D), row), pl.BlockSpec((tm, LANES), row)]
        args += [y0, y1, rp]
    in_specs += [pl.BlockSpec((1, D), lambda i, j: (0, 0)), pl.BlockSpec((D, tn), lambda i, j: (0, j))]
    args += [gain.reshape(1, D), w]
    out_shape = [jax.ShapeDtypeStruct((T, N), BF16)]
    out_specs = [pl.BlockSpec((tm, tn), lambda i, j: (i, j))]
    if comb is not None:
        out_shape.insert(0, jax.ShapeDtypeStruct((T, D), F32))
        out_specs.insert(0, pl.BlockSpec((tm, D), row))
    res = pl.pallas_call(
        functools.partial(_in_proj_body, comb is not None),
        grid=(T // tm, N // tn),
        in_specs=in_specs, out_specs=out_specs, out_shape=out_shape,
        scratch_shapes=[pltpu.VMEM((tm, D), BF16)],
        compiler_params=_cparams("parallel", "arbitrary"),
        name="in_proj",
    )(*args)
    if comb is not None:
        return res[0], res[1]
    return h, res[0]


def _rope256(x, cos, sin_signed):
    xr = jnp.concatenate([pltpu.roll(x[:, :LANES], 64, 1), pltpu.roll(x[:, LANES:], 64, 1)], axis=1)
    return x * cos + xr * sin_signed


def _retention_body(dl_ref, q_ref, k_ref, v_ref, g_ref, cos_ref, sin_ref, y_ref, state_ref, obwd_ref,
                    *, n_chunks, n_blocks):
    C = RET_CHUNK
    h = pl.program_id(1)
    phase = pl.program_id(2)
    n = pl.program_id(3)
    sb = n_chunks * C

    def log_gamma(s):
        x = jnp.full((1, 1), s, F32)
        return jnp.minimum(x, 0.0) - jnp.log(1.0 + jnp.exp(-jnp.abs(x)))

    lg_f = log_gamma(dl_ref[h])
    lg_b = log_gamma(dl_ref[RET_HEADS + h])
    pos = lax.broadcasted_iota(jnp.int32, (C, RET_QK_DIM), 0).astype(F32)

    @pl.when(n == 0)
    def _():
        state_ref[...] = jnp.zeros_like(state_ref)

    def load_qkv(rows):
        cos = cos_ref[rows, :]
        sin = sin_ref[rows, :]
        q = _rope256(q_ref[rows, :].astype(F32), cos, sin)
        k = _rope256(k_ref[rows, :].astype(F32), cos, sin) * (RET_QK_DIM ** -0.5)
        return q, k, v_ref[rows, :]

    def scan_step(q, k, v, q_dec, k_dec, chunk_dec):
        st = state_ref[...]
        out = jnp.dot((q * q_dec).astype(BF16), st.astype(BF16), preferred_element_type=F32)
        kt = (k * k_dec).T.astype(BF16)
        state_ref[...] = st * chunk_dec + jnp.dot(kt, v, preferred_element_type=F32)
        return out

    @pl.when(phase == 0)
    def _():
        q_dec = jnp.exp((C - pos) * lg_b)
        k_dec = jnp.exp(pos * lg_b)
        chunk_dec = jnp.exp(lg_b * C)
        base = (n_blocks - 1 - n) * sb

        def body(cc, carry):
            c = n_chunks - 1 - cc
            rows = pl.ds(pl.multiple_of(c * C, C), C)
            q, k, v = load_qkv(rows)
            out = scan_step(q, k, v, q_dec, k_dec, chunk_dec)
            obwd_ref[pl.ds(pl.multiple_of(base + c * C, C), C), :] = out.astype(obwd_ref.dtype)
            return carry

        lax.fori_loop(0, n_chunks, body, 0)

    @pl.when(phase == 1)
    def _():
        q_dec = jnp.exp((pos + 1.0) * lg_f)
        k_dec = jnp.exp((C - 1.0 - pos) * lg_f)
        chunk_dec = jnp.exp(lg_f * C)
        ii = lax.broadcasted_iota(jnp.int32, (C, C), 0)
        jj = lax.broadcasted_iota(jnp.int32, (C, C), 1)
        dist = (ii - jj).astype(F32)
        inner = (jnp.where(dist >= 0, jnp.exp(lg_f * jnp.maximum(dist, 0.0)), 0.0)
                 + jnp.where(dist <= 0, jnp.exp(lg_b * jnp.maximum(-dist, 0.0)), 0.0))
        base = n * sb

        def body(c, carry):
            rows = pl.ds(pl.multiple_of(c * C, C), C)
            q, k, v = load_qkv(rows)
            scores = lax.dot_general(q.astype(BF16), k.astype(BF16), (((1,), (1,)), ((), ())),
                                     preferred_element_type=F32) * inner
            o = jnp.dot(scores.astype(BF16), v, preferred_element_type=F32)
            o = o + scan_step(q, k, v, q_dec, k_dec, chunk_dec)
            o = o + obwd_ref[pl.ds(pl.multiple_of(base + c * C, C), C), :].astype(F32)
            o = o * lax.rsqrt(jnp.mean(o * o, axis=-1, keepdims=True) + NORM_EPS)
            g = g_ref[rows, :].astype(F32)
            y_ref[rows, :] = (g * jax.nn.sigmoid(g) * o).astype(y_ref.dtype)
            return carry

        lax.fori_loop(0, n_chunks, body, 0)


def _retention(proj, decay_logit, cos, sin_signed, B, S, sb=2048):
    sb = min(sb, S)
    nb = S // sb
    proj3 = proj.reshape(B, S, proj.shape[-1])
    blk = lambda p, n: jnp.where(p == 0, nb - 1 - n, n)
    kq = RET_QK_WIDTH // RET_QK_DIM
    kv = 2 * RET_QK_WIDTH // RET_V_DIM
    kg = kv + RET_HEADS
    in_specs = [
        pl.BlockSpec(memory_space=pltpu.SMEM),
        pl.BlockSpec((None, sb, RET_QK_DIM), lambda b, h, p, n: (b, blk(p, n), h)),
        pl.BlockSpec((None, sb, RET_QK_DIM), lambda b, h, p, n: (b, blk(p, n), kq + h)),
        pl.BlockSpec((None, sb, RET_V_DIM), lambda b, h, p, n: (b, blk(p, n), kv + h)),
        pl.BlockSpec((None, sb, RET_V_DIM), lambda b, h, p, n: (b, n * p, kg + h)),
        pl.BlockSpec((sb, RET_QK_DIM), lambda b, h, p, n: (blk(p, n), 0)),
        pl.BlockSpec((sb, RET_QK_DIM), lambda b, h, p, n: (blk(p, n), 0)),
    ]
    out = pl.pallas_call(
        functools.partial(_retention_body, n_chunks=sb // RET_CHUNK, n_blocks=nb),
        grid=(B, RET_HEADS, 2, nb),
        in_specs=in_specs,
        out_specs=pl.BlockSpec((None, sb, RET_V_DIM), lambda b, h, p, n: (b, n * p, h)),
        out_shape=jax.ShapeDtypeStruct((B, S, RET_V_WIDTH), BF16),
        scratch_shapes=[pltpu.VMEM((RET_QK_DIM, RET_V_DIM), F32), pltpu.VMEM((S, RET_V_DIM), BF16)],
        compiler_params=_cparams("parallel", "parallel", "arbitrary", "arbitrary"),
        name="retention",
    )(decay_logit.reshape(2 * RET_HEADS).astype(F32), proj3, proj3, proj3, proj3, cos, sin_signed)
    return out.reshape(B * S, RET_V_WIDTH)


def _rope128(x, cos, sin_signed, low):
    xr = jnp.where(low, pltpu.roll(x, 96, 1), pltpu.roll(x, 32, 1))
    return x * cos + xr * sin_signed


def _qk_prep_body(p_ref, qg_ref, kg_ref, cos_ref, sin_ref, q_ref, k_ref):
    d = ATTN_HEAD_DIM
    cos = cos_ref[...]
    sin = sin_ref[...]
    low = (lax.broadcasted_iota(jnp.int32, cos.shape, 1) % 64) < 32

    def norm_rope(x, gain):
        x = x.astype(F32)
        x = x * lax.rsqrt(jnp.mean(x * x, axis=-1, keepdims=True) + NORM_EPS) * gain
        return _rope128(x, cos, sin, low)

    for hq in range(ATTN_Q_HEADS):
        q = norm_rope(p_ref[:, hq * d:(hq + 1) * d], qg_ref[...]) * (d ** -0.5)
        q_ref[:, hq * d:(hq + 1) * d] = q.astype(q_ref.dtype)
    for hk in range(ATTN_KV_HEADS):
        c0 = (ATTN_Q_HEADS + hk) * d
        k_ref[:, hk * d:(hk + 1) * d] = norm_rope(p_ref[:, c0:c0 + d], kg_ref[...]).astype(k_ref.dtype)


def _qk_prep(proj, q_gain, k_gain, cos, sin_signed, S, tm=512):
    T = proj.shape[0]
    tm = min(tm, S)
    spb = S // tm
    d = ATTN_HEAD_DIM
    row = lambda i: (i, 0)
    return pl.pallas_call(
        _qk_prep_body,
        grid=(T // tm,),
        in_specs=[pl.BlockSpec((tm, ATTN_IN_WIDTH), row),
                  pl.BlockSpec((1, d), lambda i: (0, 0)), pl.BlockSpec((1, d), lambda i: (0, 0)),
                  pl.BlockSpec((tm, d), lambda i: (i % spb, 0)), pl.BlockSpec((tm, d), lambda i: (i % spb, 0))],
        out_specs=[pl.BlockSpec((tm, ATTN_Q_HEADS * d), row), pl.BlockSpec((tm, ATTN_KV_HEADS * d), row)],
        out_shape=[jax.ShapeDtypeStruct((T, ATTN_Q_HEADS * d), BF16),
                   jax.ShapeDtypeStruct((T, ATTN_KV_HEADS * d), BF16)],
        compiler_params=_cparams("parallel"),
        name="qk_prep",
    )(proj, q_gain.reshape(1, d), k_gain.reshape(1, d), cos, sin_signed)


def _attn_body(q_ref, k_ref, v_ref, o_ref, qs_ref, m_ref, l_ref, acc_ref, *, tq, tk, n_kv):
    d = ATTN_HEAD_DIM
    for g in range(ATTN_GROUP):
        qs_ref[g * tq:(g + 1) * tq, :] = q_ref[:, g * d:(g + 1) * d]
    m_ref[...] = jnp.full_like(m_ref, -jnp.inf)
    l_ref[...] = jnp.zeros_like(l_ref)
    acc_ref[...] = jnp.zeros_like(acc_ref)

    def step(c, carry):
        rows = pl.ds(pl.multiple_of(c * tk, tk), tk)
        s = lax.dot_general(qs_ref[...], k_ref[rows, :], (((1,), (1,)), ((), ())), preferred_element_type=F32)
        m_prev = m_ref[...]
        m_new = jnp.maximum(m_prev, jnp.max(s, axis=-1, keepdims=True))
        alpha = jnp.exp(m_prev - m_new)
        p = jnp.exp(s - m_new)
        l_ref[...] = alpha * l_ref[...] + jnp.sum(p, axis=-1, keepdims=True)
        acc_ref[...] = alpha * acc_ref[...] + jnp.dot(p.astype(BF16), v_ref[rows, :], preferred_element_type=F32)
        m_ref[...] = m_new
        return carry

    lax.fori_loop(0, n_kv, step, 0)
    out = acc_ref[...] / l_ref[...]
    for g in range(ATTN_GROUP):
        o_ref[:, g * d:(g + 1) * d] = out[g * tq:(g + 1) * tq, :].astype(o_ref.dtype)


def _attention(qn, kn, proj, B, S, tq=256, tk=512):
    d = ATTN_HEAD_DIM
    tq = min(tq, S)
    tk = min(tk, S)
    gw = ATTN_GROUP * d
    q3 = qn.reshape(B, S, ATTN_Q_HEADS * d)
    k3 = kn.reshape(B, S, ATTN_KV_HEADS * d)
    p3 = proj.reshape(B, S, ATTN_IN_WIDTH)
    v_col = ATTN_Q_HEADS + ATTN_KV_HEADS
    m = ATTN_GROUP * tq
    out = pl.pallas_call(
        functools.partial(_attn_body, tq=tq, tk=tk, n_kv=S // tk),
        grid=(B, ATTN_KV_HEADS, S // tq),
        in_specs=[pl.BlockSpec((None, tq, gw), lambda b, kh, i: (b, i, kh)),
                  pl.BlockSpec((None, S, d), lambda b, kh, i: (b, 0, kh)),
                  pl.BlockSpec((None, S, d), lambda b, kh, i: (b, 0, v_col + kh))],
        out_specs=pl.BlockSpec((None, tq, gw), lambda b, kh, i: (b, i, kh)),
        out_shape=jax.ShapeDtypeStruct((B, S, ATTN_Q_HEADS * d), BF16),
        scratch_shapes=[pltpu.VMEM((m, d), BF16), pltpu.VMEM((m, 1), F32), pltpu.VMEM((m, 1), F32),
                        pltpu.VMEM((m, d), F32)],
        compiler_params=_cparams("parallel", "parallel", "arbitrary"),
        name="attention",
    )(q3, k3, p3)
    return out.reshape(B * S, ATTN_Q_HEADS * d)


ROUTER_GROUP_LANE = MOE_EXPERTS


def _out_proj_body(y_ref, h_ref, w_ref, g_ref, wr_ref, br_ref, hout_ref, xn_ref, rp_ref):
    h = h_ref[...] + jnp.dot(y_ref[...], w_ref[...], preferred_element_type=F32)
    hout_ref[...] = h
    xn = h * lax.rsqrt(jnp.mean(h * h, axis=-1, keepdims=True) + NORM_EPS) * g_ref[...]
    xn_ref[...] = xn.astype(xn_ref.dtype)

    logits = jnp.dot(xn, wr_ref[...], preferred_element_type=F32, precision=lax.Precision.HIGHEST) + br_ref[...]
    lane = lax.broadcasted_iota(jnp.int32, logits.shape, 1)
    big = jnp.int32(LANES)
    neg = jnp.float32(-jnp.inf)

    def first_argmax(x):
        mx = jnp.max(x, axis=-1, keepdims=True)
        return mx, jnp.min(jnp.where(x == mx, lane, big), axis=-1, keepdims=True)

    gmask = (lane >= ROUTER_GROUP_LANE) & (lane < ROUTER_GROUP_LANE + MOE_GROUPS)
    gmax, gidx = first_argmax(jnp.where(gmask, logits, neg))
    g_sel = gidx - ROUTER_GROUP_LANE
    g_w = 1.0 / jnp.sum(jnp.where(gmask, jnp.exp(logits - gmax), 0.0), axis=-1, keepdims=True)

    emask = (lane // MOE_EXPERTS_PER_GROUP) == g_sel
    el = jnp.where(emask, logits, neg)
    v1, i1 = first_argmax(el)
    v2, i2 = first_argmax(jnp.where(lane == i1, neg, el))
    e2 = jnp.exp(v2 - v1)
    w1 = g_w / (1.0 + e2)
    w2 = g_w * e2 / (1.0 + e2)
    rp = jnp.where(lane == 0, i1.astype(F32),
                   jnp.where(lane == 1, i2.astype(F32),
                             jnp.where(lane == 2, w1, jnp.where(lane == 3, w2, 0.0))))
    rp_ref[...] = rp


def _out_proj(y, h, w, gain, w_router, b_router, tm=512):
    T, K = y.shape
    D = h.shape[1]
    tm = min(tm, T)
    row = lambda i: (i, 0)
    const = lambda i: (0, 0)
    return pl.pallas_call(
        _out_proj_body,
        grid=(T // tm,),
        in_specs=[pl.BlockSpec((tm, K), row), pl.BlockSpec((tm, D), row), pl.BlockSpec((K, D), const),
                  pl.BlockSpec((1, D), const), pl.BlockSpec((D, LANES), const), pl.BlockSpec((1, LANES), const)],
        out_specs=[pl.BlockSpec((tm, D), row), pl.BlockSpec((tm, D), row), pl.BlockSpec((tm, LANES), row)],
        out_shape=[jax.ShapeDtypeStruct((T, D), F32), jax.ShapeDtypeStruct((T, D), BF16),
                   jax.ShapeDtypeStruct((T, LANES), F32)],
        compiler_params=_cparams("parallel"),
        name="out_proj",
    )(y, h, w, gain.reshape(1, D), w_router, b_router)


def _moe_body(be_ref, bv_ref, x_ref, wgu_ref, wd_ref, y_ref):
    b = pl.program_id(0)

    @pl.when(bv_ref[b] != 0)
    def _():
        gu = jnp.dot(x_ref[...], wgu_ref[...], preferred_element_type=F32)
        gate = gu[:, :MOE_FF]
        up = gu[:, MOE_FF:]
        act = (gate * jax.nn.sigmoid(gate) * up).astype(BF16)
        y_ref[...] = jnp.dot(act, wd_ref[...], preferred_element_type=F32).astype(y_ref.dtype)

    @pl.when(bv_ref[b] == 0)
    def _():
        y_ref[...] = jnp.zeros_like(y_ref)


def _moe_experts(xs, block_e, block_valid, w_gate_up, w_down):
    P, D = xs.shape
    n_blocks = P // MOE_BLOCK
    grid_spec = pltpu.PrefetchScalarGridSpec(
        num_scalar_prefetch=2,
        grid=(n_blocks,),
        in_specs=[pl.BlockSpec((MOE_BLOCK, D), lambda b, be, bv: (b, 0)),
                  pl.BlockSpec((None, D, 2 * MOE_FF), lambda b, be, bv: (be[b], 0, 0)),
                  pl.BlockSpec((None, MOE_FF, D), lambda b, be, bv: (be[b], 0, 0))],
        out_specs=pl.BlockSpec((MOE_BLOCK, D), lambda b, be, bv: (b, 0)),
    )
    return pl.pallas_call(
        _moe_body,
        grid_spec=grid_spec,
        out_shape=jax.ShapeDtypeStruct((P, D), BF16),
        compiler_params=_cparams("arbitrary"),
        name="moe_experts",
    )(block_e, block_valid, xs, w_gate_up, w_down)


def _moe_dispatch_plan(rp, T):
    A = T * MOE_TOPK
    n_blocks = -(-(A + MOE_EXPERTS * (MOE_BLOCK - 1)) // MOE_BLOCK)
    P = n_blocks * MOE_BLOCK
    flat_e = rp[:, :MOE_TOPK].astype(jnp.int32).reshape(A)
    onehot = (flat_e[:, None] == jnp.arange(MOE_EXPERTS, dtype=jnp.int32)[None, :]).astype(jnp.int32)
    csum = jnp.cumsum(onehot, axis=0)
    rank = jnp.sum(jnp.where(onehot > 0, csum, 0), axis=1) - 1
    counts = csum[-1]
    padded = ((counts + MOE_BLOCK - 1) // MOE_BLOCK) * MOE_BLOCK
    pends = jnp.cumsum(padded)
    pstarts = pends - padded
    dest = pstarts[flat_e] + rank
    flat_t = jnp.arange(A, dtype=jnp.int32) // MOE_TOPK
    slot_tok = jnp.zeros((P,), jnp.int32).at[dest].set(flat_t)
    block_start = jnp.arange(n_blocks, dtype=jnp.int32) * MOE_BLOCK
    block_e = jnp.minimum(jnp.searchsorted(pends, block_start, side="right"), MOE_EXPERTS - 1).astype(jnp.int32)
    block_valid = (block_start < pends[-1]).astype(jnp.int32)
    return slot_tok, dest.reshape(T, MOE_TOPK), block_e, block_valid


def _moe(xn, rp, w_gate_up, w_down):
    T = xn.shape[0]
    slot_tok, dest, block_e, block_valid = _moe_dispatch_plan(rp, T)
    xs = jnp.take(xn, slot_tok, axis=0)
    y = _moe_experts(xs, block_e, block_valid, w_gate_up, w_down)
    return jnp.take(y, dest[:, 0], axis=0), jnp.take(y, dest[:, 1], axis=0)


def _final_body(h_ref, y0_ref, y1_ref, rp_ref, g_ref, o_ref):
    rp = rp_ref[...]
    h = h_ref[...] + rp[:, 2:3] * y0_ref[...].astype(F32) + rp[:, 3:4] * y1_ref[...].astype(F32)
    o_ref[...] = h * lax.rsqrt(jnp.mean(h * h, axis=-1, keepdims=True) + NORM_EPS) * g_ref[...]


def _final(h, y0, y1, rp, gain, tm=1024):
    T, D = h.shape
    tm = min(tm, T)
    row = lambda i: (i, 0)
    return pl.pallas_call(
        _final_body,
        grid=(T // tm,),
        in_specs=[pl.BlockSpec((tm, D), row), pl.BlockSpec((tm, D), row), pl.BlockSpec((tm, D), row),
                  pl.BlockSpec((tm, LANES), row), pl.BlockSpec((1, D), lambda i: (0, 0))],
        out_specs=pl.BlockSpec((tm, D), row),
        out_shape=jax.ShapeDtypeStruct((T, D), F32),
        compiler_params=_cparams("parallel"),
        name="final_norm",
    )(h, y0, y1, rp, gain.reshape(1, D))


def _rope_tables(n_tok, head_dim):
    t = jnp.arange(n_tok, dtype=jnp.int32)
    row = (t // GRID_W).astype(F32)
    col = (t % GRID_W).astype(F32)
    half = head_dim // 2
    inv = ROPE_THETA ** (-jnp.arange(0, half, 2, dtype=F32) / half)
    ang_r = row[:, None] * inv[None, :]
    ang_c = col[:, None] * inv[None, :]
    ang = jnp.concatenate([ang_r, ang_r, ang_c, ang_c], axis=-1)
    q = half // 2
    sign = jnp.where((jnp.arange(head_dim) % half) < q, -1.0, 1.0).astype(F32)
    return jnp.cos(ang), jnp.sin(ang) * sign[None, :]


def _router_weights(w_group, b_group, w_expert, b_expert):
    D = w_group.shape[0]
    pad = LANES - MOE_EXPERTS - MOE_GROUPS
    w = jnp.concatenate([w_expert, w_group, jnp.zeros((D, pad), F32)], axis=1)
    b = jnp.concatenate([b_expert, b_group, jnp.zeros((pad,), F32)]).reshape(1, LANES)
    return w.astype(F32), b.astype(F32)


def _trunk(x, p):
    B, S, D = x.shape
    T = B * S
    cos_r, sin_r = _rope_tables(S, RET_QK_DIM)
    cos_a, sin_a = _rope_tables(S, ATTN_HEAD_DIM)
    h = x.reshape(T, D)

    h, proj = _in_proj(h, p["ln_mix"][0], p["ret_w_in"][0])
    y = _retention(proj, p["ret_decay_logit"][0], cos_r, sin_r, B, S)
    h, xn, rp = _out_proj(y, h, p["ret_w_out"][0], p["ln_ffn"][0], *p["router"][0])
    y0, y1 = _moe(xn, rp, p["moe_w_gate_up"][0], p["moe_w_down"][0])

    h, proj = _in_proj(h, p["ln_mix"][1], p["attn_w_in"][0], comb=(y0, y1, rp))
    qn, kn = _qk_prep(proj, p["attn_q_gain"][0], p["attn_k_gain"][0], cos_a, sin_a, S)
    y = _attention(qn, kn, proj, B, S)
    h, xn, rp = _out_proj(y, h, p["attn_w_out"][0], p["ln_ffn"][1], *p["router"][1])
    y0, y1 = _moe(xn, rp, p["moe_w_gate_up"][1], p["moe_w_down"][1])

    return _final(h, y0, y1, rp, p["ln_final"]).reshape(B, S, D)


def kernel(x_prompt, x_sample, ln_mix, ret_w_in, ret_decay_logit, ret_w_out, attn_w_in, attn_q_gain, attn_k_gain,
           attn_w_out, ln_ffn, moe_w_group, moe_b_group, moe_w_expert, moe_b_expert, moe_w_gate_up, moe_w_down,
           ln_final):
    p = {
        "ln_mix": ln_mix, "ln_ffn": ln_ffn, "ln_final": ln_final,
        "ret_w_in": ret_w_in.astype(BF16), "ret_decay_logit": ret_decay_logit, "ret_w_out": ret_w_out.astype(BF16),
        "attn_w_in": attn_w_in.astype(BF16), "attn_q_gain": attn_q_gain, "attn_k_gain": attn_k_gain,
        "attn_w_out": attn_w_out.astype(BF16),
        "router": [_router_weights(moe_w_group[i], moe_b_group[i], moe_w_expert[i], moe_b_expert[i])
                   for i in range(moe_w_group.shape[0])],
        "moe_w_gate_up": moe_w_gate_up.astype(BF16), "moe_w_down": moe_w_down.astype(BF16),
    }
    return _trunk(x_prompt, p), _trunk(x_sample, p)
```

```python
import functools

import jax
import jax.numpy as jnp
from jax import lax
from jax.experimental import pallas as pl
from jax.experimental.pallas import tpu as pltpu

F32 = jnp.float32
BF16 = jnp.bfloat16

D_MODEL = 1024
GRID_W = 64
ROPE_THETA = 10000.0
NORM_EPS = 1e-6

RET_HEADS = 4
RET_QK_DIM = 256
RET_V_DIM = 512
RET_SCAN_CHUNK = 256
RET_QK_WIDTH = RET_HEADS * RET_QK_DIM
RET_V_WIDTH = RET_HEADS * RET_V_DIM

ATTN_Q_HEADS = 8
ATTN_KV_HEADS = 2
ATTN_HEAD_DIM = 128
ATTN_GROUP = ATTN_Q_HEADS // ATTN_KV_HEADS
ATTN_IN_WIDTH = (ATTN_Q_HEADS + 2 * ATTN_KV_HEADS) * ATTN_HEAD_DIM

MOE_GROUPS = 4
MOE_EXPERTS_PER_GROUP = 8
MOE_EXPERTS = MOE_GROUPS * MOE_EXPERTS_PER_GROUP
MOE_TOPK = 2
MOE_FF = 512
MOE_BLOCK = 256

LANES = 128
LOG2_E = 1.4426950408889634
VMEM_LIMIT = 56 * 1024 * 1024


def _cparams(*sem):
    return pltpu.CompilerParams(dimension_semantics=sem, vmem_limit_bytes=VMEM_LIMIT)


def _in_proj_body(combine, *refs):
    if combine:
        h_ref, y0_ref, y1_ref, rp_ref, g_ref, w_ref, hout_ref, o_ref, xn_ref = refs
    else:
        h_ref, g_ref, w_ref, o_ref, xn_ref = refs

    @pl.when(pl.program_id(1) == 0)
    def _():
        h = h_ref[...]
        if combine:
            rp = rp_ref[...]
            h = h + rp[:, 2:3] * y0_ref[...].astype(F32) + rp[:, 3:4] * y1_ref[...].astype(F32)
            hout_ref[...] = h
        xn = h * lax.rsqrt(jnp.mean(h * h, axis=-1, keepdims=True) + NORM_EPS) * g_ref[...]
        xn_ref[...] = xn.astype(BF16)

    o_ref[...] = jnp.dot(xn_ref[...], w_ref[...], preferred_element_type=F32).astype(o_ref.dtype)


def _in_proj(h, gain, w, comb=None, tm=1024, tn=1536):
    T, D = h.shape
    N = w.shape[1]
    tm = min(tm, T)
    tn = min(tn, N)
    row = lambda i, j: (i, 0)
    in_specs = [pl.BlockSpec((tm, D), row)]
    args = [h]
    if comb is not None:
        y0, y1, rp = comb
        in_specs += [pl.BlockSpec((tm, D), row), pl.BlockSpec((tm, D), row), pl.BlockSpec((tm, LANES), row)]
        args += [y0, y1, rp]
    in_specs += [pl.BlockSpec((1, D), lambda i, j: (0, 0)), pl.BlockSpec((D, tn), lambda i, j: (0, j))]
    args += [gain.reshape(1, D), w]
    out_shape = [jax.ShapeDtypeStruct((T, N), BF16)]
    out_specs = [pl.BlockSpec((tm, tn), lambda i, j: (i, j))]
    if comb is not None:
        out_shape.insert(0, jax.ShapeDtypeStruct((T, D), F32))
        out_specs.insert(0, pl.BlockSpec((tm, D), row))
    res = pl.pallas_call(
        functools.partial(_in_proj_body, comb is not None),
        grid=(T // tm, N // tn),
        in_specs=in_specs, out_specs=out_specs, out_shape=out_shape,
        scratch_shapes=[pltpu.VMEM((tm, D), BF16)],
        compiler_params=_cparams("parallel", "arbitrary"),
        name="in_proj",
    )(*args)
    if comb is not None:
        return res[0], res[1]
    return h, res[0]


def _rope256(x, cos, sin_signed):
    xr = jnp.concatenate([pltpu.roll(x[:, :LANES], 64, 1), pltpu.roll(x[:, LANES:], 64, 1)], axis=1)
    return x * cos + xr * sin_signed


def _retention_body(dl_ref, q_ref, k_ref, v_ref, g_ref, cos_ref, sin_ref, y_ref, state_ref, obwd_ref,
                    *, n_chunks, n_blocks):
    C = RET_SCAN_CHUNK
    h = pl.program_id(1)
    phase = pl.program_id(2)
    n = pl.program_id(3)
    sb = n_chunks * C

    def log_gamma(s):
        x = jnp.full((1, 1), s, F32)
        return jnp.minimum(x, 0.0) - jnp.log(1.0 + jnp.exp(-jnp.abs(x)))

    lg_f = log_gamma(dl_ref[h])
    lg_b = log_gamma(dl_ref[RET_HEADS + h])
    pos = lax.broadcasted_iota(jnp.int32, (C, RET_QK_DIM), 0).astype(F32)
    k_scale = RET_QK_DIM ** -0.5

    @pl.when(n == 0)
    def _():
        state_ref[...] = jnp.zeros_like(state_ref)

    def load(rows, q_dec, k_dec):
        cos = cos_ref[rows, :]
        sin = sin_ref[rows, :]
        q = _rope256(q_ref[rows, :], cos, sin)
        k = _rope256(k_ref[rows, :], cos, sin)
        return q, k, v_ref[rows, :], q * q_dec, (k * k_dec).T

    @pl.when(phase == 0)
    def _():
        q_dec = jnp.exp((C - pos) * lg_b).astype(BF16)
        k_dec = (jnp.exp(pos * lg_b) * k_scale).astype(BF16)
        chunk_dec = jnp.exp(lg_b * C)
        base = (n_blocks - 1 - n) * sb

        def body(cc, carry):
            c = n_chunks - 1 - cc
            rows = pl.ds(pl.multiple_of(c * C, C), C)
            _, _, v, qd, kt = load(rows, q_dec, k_dec)
            st = state_ref[...]
            out = jnp.dot(qd, st.astype(BF16), preferred_element_type=F32)
            state_ref[...] = st * chunk_dec + jnp.dot(kt, v, preferred_element_type=F32)
            obwd_ref[pl.ds(pl.multiple_of(base + c * C, C), C), :] = out.astype(obwd_ref.dtype)
            return carry

        lax.fori_loop(0, n_chunks, body, 0, unroll=min(2, n_chunks))

    @pl.when(phase == 1)
    def _():
        q_dec = jnp.exp((pos + 1.0) * lg_f).astype(BF16)
        k_dec = (jnp.exp((C - 1.0 - pos) * lg_f) * k_scale).astype(BF16)
        chunk_dec = jnp.exp(lg_f * C)
        ii = lax.broadcasted_iota(jnp.int32, (C, C), 0)
        jj = lax.broadcasted_iota(jnp.int32, (C, C), 1)
        dist = (ii - jj).astype(F32)
        inner = (jnp.where(dist >= 0, jnp.exp(lg_f * jnp.maximum(dist, 0.0)), 0.0)
                 + jnp.where(dist <= 0, jnp.exp(lg_b * jnp.maximum(-dist, 0.0)), 0.0)) * k_scale
        base = n * sb

        def body(c, carry):
            rows = pl.ds(pl.multiple_of(c * C, C), C)
            q, k, v, qd, kt = load(rows, q_dec, k_dec)
            scores = lax.dot_general(q, k, (((1,), (1,)), ((), ())), preferred_element_type=F32) * inner
            st = state_ref[...]
            lhs = jnp.concatenate([qd, scores.astype(BF16)], axis=1)
            rhs = jnp.concatenate([st.astype(BF16), v], axis=0)
            o = jnp.dot(lhs, rhs, preferred_element_type=F32)
            state_ref[...] = st * chunk_dec + jnp.dot(kt, v, preferred_element_type=F32)
            o = o + obwd_ref[pl.ds(pl.multiple_of(base + c * C, C), C), :].astype(F32)
            o = o * lax.rsqrt(jnp.mean(o * o, axis=-1, keepdims=True) + NORM_EPS)
            g = g_ref[rows, :].astype(F32)
            y_ref[rows, :] = (g * jax.nn.sigmoid(g) * o).astype(y_ref.dtype)
            return carry

        lax.fori_loop(0, n_chunks, body, 0, unroll=min(2, n_chunks))


def _retention(proj, decay_logit, cos, sin_signed, B, S, sb=2048):
    sb = min(sb, S)
    nb = S // sb
    proj3 = proj.reshape(B, S, proj.shape[-1])
    blk = lambda p, n: jnp.where(p == 0, nb - 1 - n, n)
    kq = RET_QK_WIDTH // RET_QK_DIM
    kv = 2 * RET_QK_WIDTH // RET_V_DIM
    kg = kv + RET_HEADS
    in_specs = [
        pl.BlockSpec(memory_space=pltpu.SMEM),
        pl.BlockSpec((None, sb, RET_QK_DIM), lambda b, h, p, n: (b, blk(p, n), h)),
        pl.BlockSpec((None, sb, RET_QK_DIM), lambda b, h, p, n: (b, blk(p, n), kq + h)),
        pl.BlockSpec((None, sb, RET_V_DIM), lambda b, h, p, n: (b, blk(p, n), kv + h)),
        pl.BlockSpec((None, sb, RET_V_DIM), lambda b, h, p, n: (b, n * p, kg + h)),
        pl.BlockSpec((sb, RET_QK_DIM), lambda b, h, p, n: (blk(p, n), 0)),
        pl.BlockSpec((sb, RET_QK_DIM), lambda b, h, p, n: (blk(p, n), 0)),
    ]
    out = pl.pallas_call(
        functools.partial(_retention_body, n_chunks=sb // RET_SCAN_CHUNK, n_blocks=nb),
        grid=(B, RET_HEADS, 2, nb),
        in_specs=in_specs,
        out_specs=pl.BlockSpec((None, sb, RET_V_DIM), lambda b, h, p, n: (b, n * p, h)),
        out_shape=jax.ShapeDtypeStruct((B, S, RET_V_WIDTH), BF16),
        scratch_shapes=[pltpu.VMEM((RET_QK_DIM, RET_V_DIM), F32), pltpu.VMEM((S, RET_V_DIM), BF16)],
        compiler_params=_cparams("parallel", "parallel", "arbitrary", "arbitrary"),
        name="retention",
    )(decay_logit.reshape(2 * RET_HEADS).astype(F32), proj3, proj3, proj3, proj3,
      cos.astype(BF16), sin_signed.astype(BF16))
    return out.reshape(B * S, RET_V_WIDTH)


def _rope128(x, cos, sin_signed, low):
    xr = jnp.where(low, pltpu.roll(x, 96, 1), pltpu.roll(x, 32, 1))
    return x * cos + xr * sin_signed


def _qk_prep_body(p_ref, qg_ref, kg_ref, cos_ref, sin_ref, q_ref, k_ref, v_ref):
    d = ATTN_HEAD_DIM
    cos = cos_ref[...]
    sin = sin_ref[...]
    low = (lax.broadcasted_iota(jnp.int32, cos.shape, 1) % 64) < 32

    def norm_rope(x, gain):
        x = x.astype(F32)
        x = x * lax.rsqrt(jnp.mean(x * x, axis=-1, keepdims=True) + NORM_EPS) * gain
        return _rope128(x, cos, sin, low)

    q_scale = (d ** -0.5) * LOG2_E
    for hq in range(ATTN_Q_HEADS):
        q = norm_rope(p_ref[:, hq * d:(hq + 1) * d], qg_ref[...]) * q_scale
        q_ref[:, hq * d:(hq + 1) * d] = q.astype(q_ref.dtype)
    for hk in range(ATTN_KV_HEADS):
        c0 = (ATTN_Q_HEADS + hk) * d
        k_ref[:, hk * d:(hk + 1) * d] = norm_rope(p_ref[:, c0:c0 + d], kg_ref[...]).astype(k_ref.dtype)
        c1 = (ATTN_Q_HEADS + ATTN_KV_HEADS + hk) * d
        v_ref[:, 2 * hk * d:(2 * hk + 1) * d] = p_ref[:, c1:c1 + d]
        v_ref[:, (2 * hk + 1) * d:(2 * hk + 2) * d] = jnp.ones((p_ref.shape[0], d), v_ref.dtype)


def _qk_prep(proj, q_gain, k_gain, cos, sin_signed, S, tm=512):
    T = proj.shape[0]
    tm = min(tm, S)
    spb = S // tm
    d = ATTN_HEAD_DIM
    row = lambda i: (i, 0)
    return pl.pallas_call(
        _qk_prep_body,
        grid=(T // tm,),
        in_specs=[pl.BlockSpec((tm, ATTN_IN_WIDTH), row),
                  pl.BlockSpec((1, d), lambda i: (0, 0)), pl.BlockSpec((1, d), lambda i: (0, 0)),
                  pl.BlockSpec((tm, d), lambda i: (i % spb, 0)), pl.BlockSpec((tm, d), lambda i: (i % spb, 0))],
        out_specs=[pl.BlockSpec((tm, ATTN_Q_HEADS * d), row), pl.BlockSpec((tm, ATTN_KV_HEADS * d), row),
                   pl.BlockSpec((tm, 2 * ATTN_KV_HEADS * d), row)],
        out_shape=[jax.ShapeDtypeStruct((T, ATTN_Q_HEADS * d), BF16),
                   jax.ShapeDtypeStruct((T, ATTN_KV_HEADS * d), BF16),
                   jax.ShapeDtypeStruct((T, 2 * ATTN_KV_HEADS * d), BF16)],
        compiler_params=_cparams("parallel"),
        name="qk_prep",
    )(proj, q_gain.reshape(1, d), k_gain.reshape(1, d), cos, sin_signed)


def _attn_body(q_ref, k_ref, v_ref, o_ref, qs_ref, m_ref, acc_ref, *, tq, tk, n_kv):
    d = ATTN_HEAD_DIM
    for g in range(ATTN_GROUP):
        qs_ref[g * tq:(g + 1) * tq, :] = q_ref[:, g * d:(g + 1) * d]
    m_ref[...] = jnp.full_like(m_ref, -jnp.inf)
    acc_ref[...] = jnp.zeros_like(acc_ref)
    rep = tk // LANES

    def step(c, carry):
        rows = pl.ds(pl.multiple_of(c * tk, tk), tk)
        k = k_ref[rows, :]
        v = v_ref[rows, :]
        for g in range(ATTN_GROUP):
            r = slice(g * tq, (g + 1) * tq)
            s = lax.dot_general(qs_ref[r, :], k, (((1,), (1,)), ((), ())), preferred_element_type=F32)
            m_prev = m_ref[r, :]
            m_new = jnp.maximum(m_prev, jnp.max(s, axis=-1, keepdims=True))
            alpha = jnp.exp2(m_prev - m_new)
            p = jnp.exp2(s - jnp.concatenate([m_new] * rep, axis=1))
            pv = jnp.dot(p.astype(BF16), v, preferred_element_type=F32)
            acc_ref[r, :] = jnp.concatenate([alpha, alpha], axis=1) * acc_ref[r, :] + pv
            m_ref[r, :] = m_new
        return carry

    lax.fori_loop(0, n_kv, step, 0, unroll=min(4, n_kv))
    for g in range(ATTN_GROUP):
        r = slice(g * tq, (g + 1) * tq)
        o_ref[:, g * d:(g + 1) * d] = (acc_ref[r, :d] / acc_ref[r, d:]).astype(o_ref.dtype)


def _attention(qn, kn, vx, B, S, tq=512, tk=512):
    d = ATTN_HEAD_DIM
    tq = min(tq, S)
    tk = min(tk, S)
    gw = ATTN_GROUP * d
    q3 = qn.reshape(B, S, ATTN_Q_HEADS * d)
    k3 = kn.reshape(B, S, ATTN_KV_HEADS * d)
    v3 = vx.reshape(B, S, 2 * ATTN_KV_HEADS * d)
    m = ATTN_GROUP * tq
    out = pl.pallas_call(
        functools.partial(_attn_body, tq=tq, tk=tk, n_kv=S // tk),
        grid=(B, ATTN_KV_HEADS, S // tq),
        in_specs=[pl.BlockSpec((None, tq, gw), lambda b, kh, i: (b, i, kh)),
                  pl.BlockSpec((None, S, d), lambda b, kh, i: (b, 0, kh)),
                  pl.BlockSpec((None, S, 2 * d), lambda b, kh, i: (b, 0, kh))],
        out_specs=pl.BlockSpec((None, tq, gw), lambda b, kh, i: (b, i, kh)),
        out_shape=jax.ShapeDtypeStruct((B, S, ATTN_Q_HEADS * d), BF16),
        scratch_shapes=[pltpu.VMEM((m, d), BF16), pltpu.VMEM((m, LANES), F32), pltpu.VMEM((m, 2 * d), F32)],
        compiler_params=_cparams("parallel", "parallel", "arbitrary"),
        name="attention",
    )(q3, k3, v3)
    return out.reshape(B * S, ATTN_Q_HEADS * d)


ROUTER_GROUP_ROW = MOE_EXPERTS
SUBLANES = 8
assert MOE_EXPERTS_PER_GROUP == SUBLANES and MOE_GROUPS <= SUBLANES


def _route(logits_t):
    tm = logits_t.shape[1]
    row = lax.broadcasted_iota(jnp.int32, (SUBLANES, tm), 0)
    neg = jnp.float32(-jnp.inf)
    big = jnp.int32(SUBLANES)

    def first_argmax(x):
        mx = jnp.max(x, axis=0, keepdims=True)
        return mx, jnp.min(jnp.where(x == mx, row, big), axis=0, keepdims=True)

    gl = jnp.where(row < MOE_GROUPS, logits_t[ROUTER_GROUP_ROW:ROUTER_GROUP_ROW + SUBLANES, :], neg)
    gmax, g_sel = first_argmax(gl)
    g_w = 1.0 / jnp.sum(jnp.exp(gl - gmax), axis=0, keepdims=True)
    el = logits_t[0:SUBLANES, :]
    for g in range(1, MOE_GROUPS):
        el = jnp.where(g_sel == g, logits_t[g * SUBLANES:(g + 1) * SUBLANES, :], el)
    v1, i1 = first_argmax(el)
    v2, i2 = first_argmax(jnp.where(row == i1, neg, el))
    e2 = jnp.exp(v2 - v1)
    w1 = g_w / (1.0 + e2)
    w2 = g_w * e2 / (1.0 + e2)
    base = g_sel * MOE_EXPERTS_PER_GROUP
    return jnp.where(row == 0, (base + i1).astype(F32),
                     jnp.where(row == 1, (base + i2).astype(F32),
                               jnp.where(row == 2, w1, jnp.where(row == 3, w2, 0.0))))


def _out_proj_body(y_ref, h_ref, w_ref, g_ref, wr_ref, br_ref, hout_ref, xn_ref, rp_ref, rpt_ref):
    h = h_ref[...] + jnp.dot(y_ref[...], w_ref[...], preferred_element_type=F32)
    hout_ref[...] = h
    xn = h * lax.rsqrt(jnp.mean(h * h, axis=-1, keepdims=True) + NORM_EPS) * g_ref[...]
    xn_hi = xn.astype(BF16)
    xn_ref[...] = xn_hi
    xn_lo = (xn - xn_hi.astype(F32)).astype(BF16)
    wr = wr_ref[...]
    hi = jnp.dot(xn_hi, wr, preferred_element_type=F32)
    lo = jnp.dot(xn_lo, wr[:, :LANES], preferred_element_type=F32)
    logits = hi[:, :LANES] + hi[:, LANES:] + lo + br_ref[...]
    rpt = _route(logits.T)
    rpt_ref[...] = rpt
    full = jnp.concatenate([rpt, jnp.zeros((LANES - SUBLANES, rpt.shape[1]), F32)], axis=0)
    rp_ref[...] = full.T


def _out_proj(y, h, w, gain, w_router, b_router, tm=512):
    T, K = y.shape
    D = h.shape[1]
    tm = min(tm, T)
    row = lambda i: (i, 0)
    const = lambda i: (0, 0)
    return pl.pallas_call(
        _out_proj_body,
        grid=(T // tm,),
        in_specs=[pl.BlockSpec((tm, K), row), pl.BlockSpec((tm, D), row), pl.BlockSpec((K, D), const),
                  pl.BlockSpec((1, D), const), pl.BlockSpec((D, 2 * LANES), const), pl.BlockSpec((1, LANES), const)],
        out_specs=[pl.BlockSpec((tm, D), row), pl.BlockSpec((tm, D), row), pl.BlockSpec((tm, LANES), row),
                   pl.BlockSpec((SUBLANES, tm), lambda i: (0, i))],
        out_shape=[jax.ShapeDtypeStruct((T, D), F32), jax.ShapeDtypeStruct((T, D), BF16),
                   jax.ShapeDtypeStruct((T, LANES), F32), jax.ShapeDtypeStruct((SUBLANES, T), F32)],
        compiler_params=_cparams("parallel"),
        name="out_proj",
    )(y, h, w, gain.reshape(1, D), w_router, b_router)


def _moe_body(be_ref, bv_ref, x_ref, wgu_ref, wd_ref, y_ref):
    b = pl.program_id(0)

    @pl.when(bv_ref[b] != 0)
    def _():
        gu = jnp.dot(x_ref[...], wgu_ref[...], preferred_element_type=F32)
        gate = gu[:, :MOE_FF]
        up = gu[:, MOE_FF:]
        act = (gate * jax.nn.sigmoid(gate) * up).astype(BF16)
        y_ref[...] = jnp.dot(act, wd_ref[...], preferred_element_type=F32).astype(y_ref.dtype)

    @pl.when(bv_ref[b] == 0)
    def _():
        y_ref[...] = jnp.zeros_like(y_ref)


def _moe_experts(xs, block_e, block_valid, w_gate_up, w_down):
    P, D = xs.shape
    n_blocks = P // MOE_BLOCK
    grid_spec = pltpu.PrefetchScalarGridSpec(
        num_scalar_prefetch=2,
        grid=(n_blocks,),
        in_specs=[pl.BlockSpec((MOE_BLOCK, D), lambda b, be, bv: (b, 0)),
                  pl.BlockSpec((None, D, 2 * MOE_FF), lambda b, be, bv: (be[b], 0, 0)),
                  pl.BlockSpec((None, MOE_FF, D), lambda b, be, bv: (be[b], 0, 0))],
        out_specs=pl.BlockSpec((MOE_BLOCK, D), lambda b, be, bv: (b, 0)),
    )
    return pl.pallas_call(
        _moe_body,
        grid_spec=grid_spec,
        out_shape=jax.ShapeDtypeStruct((P, D), BF16),
        compiler_params=_cparams("arbitrary"),
        name="moe_experts",
    )(block_e, block_valid, xs, w_gate_up, w_down)


def _moe_dispatch_plan(rpt, T):
    A = T * MOE_TOPK
    n_blocks = -(-(A + MOE_EXPERTS * (MOE_BLOCK - 1)) // MOE_BLOCK)
    P = n_blocks * MOE_BLOCK
    flat_e = rpt[:MOE_TOPK].astype(jnp.int32).T.reshape(A)
    onehot = (flat_e[:, None] == jnp.arange(MOE_EXPERTS, dtype=jnp.int32)[None, :]).astype(jnp.int32)
    csum = jnp.cumsum(onehot, axis=0)
    rank = jnp.sum(jnp.where(onehot > 0, csum, 0), axis=1) - 1
    counts = csum[-1]
    padded = ((counts + MOE_BLOCK - 1) // MOE_BLOCK) * MOE_BLOCK
    pends = jnp.cumsum(padded)
    pstarts = pends - padded
    dest = pstarts[flat_e] + rank
    flat_t = jnp.arange(A, dtype=jnp.int32) // MOE_TOPK
    slot_tok = jnp.zeros((P,), jnp.int32).at[dest].set(flat_t)
    block_start = jnp.arange(n_blocks, dtype=jnp.int32) * MOE_BLOCK
    block_e = jnp.minimum(jnp.searchsorted(pends, block_start, side="right"), MOE_EXPERTS - 1).astype(jnp.int32)
    block_valid = (block_start < pends[-1]).astype(jnp.int32)
    return slot_tok, dest.reshape(T, MOE_TOPK), block_e, block_valid


def _moe(xn, rpt, w_gate_up, w_down):
    T = xn.shape[0]
    slot_tok, dest, block_e, block_valid = _moe_dispatch_plan(rpt, T)
    xs = jnp.take(xn, slot_tok, axis=0)
    y = _moe_experts(xs, block_e, block_valid, w_gate_up, w_down)
    return jnp.take(y, dest[:, 0], axis=0), jnp.take(y, dest[:, 1], axis=0)


def _final_body(h_ref, y0_ref, y1_ref, rp_ref, g_ref, o_ref):
    rp = rp_ref[...]
    h = h_ref[...] + rp[:, 2:3] * y0_ref[...].astype(F32) + rp[:, 3:4] * y1_ref[...].astype(F32)
    o_ref[...] = h * lax.rsqrt(jnp.mean(h * h, axis=-1, keepdims=True) + NORM_EPS) * g_ref[...]


def _final(h, y0, y1, rp, gain, tm=1024):
    T, D = h.shape
    tm = min(tm, T)
    row = lambda i: (i, 0)
    return pl.pallas_call(
        _final_body,
        grid=(T // tm,),
        in_specs=[pl.BlockSpec((tm, D), row), pl.BlockSpec((tm, D), row), pl.BlockSpec((tm, D), row),
                  pl.BlockSpec((tm, LANES), row), pl.BlockSpec((1, D), lambda i: (0, 0))],
        out_specs=pl.BlockSpec((tm, D), row),
        out_shape=jax.ShapeDtypeStruct((T, D), F32),
        compiler_params=_cparams("parallel"),
        name="final_norm",
    )(h, y0, y1, rp, gain.reshape(1, D))


def _rope_tables(n_tok, head_dim):
    t = jnp.arange(n_tok, dtype=jnp.int32)
    row = (t // GRID_W).astype(F32)
    col = (t % GRID_W).astype(F32)
    half = head_dim // 2
    inv = ROPE_THETA ** (-jnp.arange(0, half, 2, dtype=F32) / half)
    ang_r = row[:, None] * inv[None, :]
    ang_c = col[:, None] * inv[None, :]
    ang = jnp.concatenate([ang_r, ang_r, ang_c, ang_c], axis=-1)
    q = half // 2
    sign = jnp.where((jnp.arange(head_dim) % half) < q, -1.0, 1.0).astype(F32)
    return jnp.cos(ang), jnp.sin(ang) * sign[None, :]


def _router_weights(w_group, b_group, w_expert, b_expert):
    D = w_group.shape[0]
    pad = LANES - MOE_EXPERTS - MOE_GROUPS
    w = jnp.concatenate([w_expert, w_group, jnp.zeros((D, pad), F32)], axis=1).astype(F32)
    b = jnp.concatenate([b_expert, b_group, jnp.zeros((pad,), F32)]).reshape(1, LANES)
    w_hi = w.astype(BF16)
    w_lo = (w - w_hi.astype(F32)).astype(BF16)
    return jnp.concatenate([w_hi, w_lo], axis=1), b.astype(F32)


def _trunk(x, p):
    B, S, D = x.shape
    T = B * S
    cos_r, sin_r = _rope_tables(S, RET_QK_DIM)
    cos_a, sin_a = _rope_tables(S, ATTN_HEAD_DIM)
    h = x.reshape(T, D)

    h, proj = _in_proj(h, p["ln_mix"][0], p["ret_w_in"][0])
    y = _retention(proj, p["ret_decay_logit"][0], cos_r, sin_r, B, S)
    h, xn, rp, rpt = _out_proj(y, h, p["ret_w_out"][0], p["ln_ffn"][0], *p["router"][0])
    y0, y1 = _moe(xn, rpt, p["moe_w_gate_up"][0], p["moe_w_down"][0])

    h, proj = _in_proj(h, p["ln_mix"][1], p["attn_w_in"][0], comb=(y0, y1, rp))
    qn, kn, vx = _qk_prep(proj, p["attn_q_gain"][0], p["attn_k_gain"][0], cos_a, sin_a, S)
    y = _attention(qn, kn, vx, B, S)
    h, xn, rp, rpt = _out_proj(y, h, p["attn_w_out"][0], p["ln_ffn"][1], *p["router"][1])
    y0, y1 = _moe(xn, rpt, p["moe_w_gate_up"][1], p["moe_w_down"][1])

    return _final(h, y0, y1, rp, p["ln_final"]).reshape(B, S, D)


def kernel(x_prompt, x_sample, ln_mix, ret_w_in, ret_decay_logit, ret_w_out, attn_w_in, attn_q_gain, attn_k_gain,
           attn_w_out, ln_ffn, moe_w_group, moe_b_group, moe_w_expert, moe_b_expert, moe_w_gate_up, moe_w_down,
           ln_final):
    p = {
        "ln_mix": ln_mix, "ln_ffn": ln_ffn, "ln_final": ln_final,
        "ret_w_in": ret_w_in.astype(BF16), "ret_decay_logit": ret_decay_logit, "ret_w_out": ret_w_out.astype(BF16),
        "attn_w_in": attn_w_in.astype(BF16), "attn_q_gain": attn_q_gain, "attn_k_gain": attn_k_gain,
        "attn_w_out": attn_w_out.astype(BF16),
        "router": [_router_weights(moe_w_group[i], moe_b_group[i], moe_w_expert[i], moe_b_expert[i])
                   for i in range(moe_w_group.shape[0])],
        "moe_w_gate_up": moe_w_gate_up.astype(BF16), "moe_w_down": moe_w_down.astype(BF16),
    }
    return _trunk(x_prompt, p), _trunk(x_sample, p)
```

```python
import functools

import jax
import jax.numpy as jnp
from jax import lax
from jax.experimental import pallas as pl
from jax.experimental.pallas import tpu as pltpu
from jax.experimental.pallas import tpu_sc as plsc

F32 = jnp.float32
BF16 = jnp.bfloat16
U32 = jnp.uint32
I32 = jnp.int32

D_MODEL = 1024
GRID_W = 64
ROPE_THETA = 10000.0
NORM_EPS = 1e-6

RET_HEADS = 4
RET_QK_DIM = 256
RET_V_DIM = 512
RET_SCAN_CHUNK = 256
RET_QK_WIDTH = RET_HEADS * RET_QK_DIM
RET_V_WIDTH = RET_HEADS * RET_V_DIM

ATTN_Q_HEADS = 8
ATTN_KV_HEADS = 2
ATTN_HEAD_DIM = 128
ATTN_GROUP = ATTN_Q_HEADS // ATTN_KV_HEADS
ATTN_IN_WIDTH = (ATTN_Q_HEADS + 2 * ATTN_KV_HEADS) * ATTN_HEAD_DIM

MOE_GROUPS = 4
MOE_EXPERTS_PER_GROUP = 8
MOE_EXPERTS = MOE_GROUPS * MOE_EXPERTS_PER_GROUP
MOE_TOPK = 2
MOE_FF = 512
MOE_BLOCK = 256

LANES = 128
LOG2_E = 1.4426950408889634
VMEM_LIMIT = 56 * 1024 * 1024


def _cparams(*sem):
    return pltpu.CompilerParams(dimension_semantics=sem, vmem_limit_bytes=VMEM_LIMIT)


def _pack_bf16_pairs(x):
    w = x.shape[1] // 2
    lo = lax.bitcast_convert_type(x[:, :w].astype(BF16).astype(F32), U32)
    hi = lax.bitcast_convert_type(x[:, w:].astype(BF16).astype(F32), U32)
    return lax.shift_right_logical(lo, jnp.uint32(16)) | (hi & jnp.uint32(0xFFFF0000))


def _unpack_bf16_pairs(u):
    lo = lax.bitcast_convert_type(lax.shift_left(u, jnp.uint32(16)), F32)
    hi = lax.bitcast_convert_type(u & jnp.uint32(0xFFFF0000), F32)
    return lo, hi


def _combine(h, y0_u32, y1_u32, rp):
    lo0, hi0 = _unpack_bf16_pairs(y0_u32)
    lo1, hi1 = _unpack_bf16_pairs(y1_u32)
    w0 = rp[:, 2:3]
    w1 = rp[:, 3:4]
    half = h.shape[1] // 2
    return jnp.concatenate([h[:, :half] + w0 * lo0 + w1 * lo1, h[:, half:] + w0 * hi0 + w1 * hi1], axis=1)


def _in_proj_body(combine, *refs):
    if combine:
        h_ref, y0_ref, y1_ref, rp_ref, g_ref, w_ref, hout_ref, o_ref, xn_ref = refs
    else:
        h_ref, g_ref, w_ref, o_ref, xn_ref = refs

    @pl.when(pl.program_id(1) == 0)
    def _():
        h = h_ref[...]
        if combine:
            h = _combine(h, y0_ref[...], y1_ref[...], rp_ref[...])
            hout_ref[...] = h
        xn = h * lax.rsqrt(jnp.mean(h * h, axis=-1, keepdims=True) + NORM_EPS) * g_ref[...]
        xn_ref[...] = xn.astype(BF16)

    o_ref[...] = jnp.dot(xn_ref[...], w_ref[...], preferred_element_type=F32).astype(o_ref.dtype)


def _in_proj(h, gain, w, comb=None, tm=1024, tn=1536):
    T, D = h.shape
    N = w.shape[1]
    tm = min(tm, T)
    tn = min(tn, N)
    row = lambda i, j: (i, 0)
    in_specs = [pl.BlockSpec((tm, D), row)]
    args = [h]
    if comb is not None:
        y0, y1, rp = comb
        in_specs += [pl.BlockSpec((tm, D // 2), row), pl.BlockSpec((tm, D // 2), row),
                     pl.BlockSpec((tm, LANES), row)]
        args += [y0, y1, rp]
    in_specs += [pl.BlockSpec((1, D), lambda i, j: (0, 0)), pl.BlockSpec((D, tn), lambda i, j: (0, j))]
    args += [gain.reshape(1, D), w]
    out_shape = [jax.ShapeDtypeStruct((T, N), BF16)]
    out_specs = [pl.BlockSpec((tm, tn), lambda i, j: (i, j))]
    if comb is not None:
        out_shape.insert(0, jax.ShapeDtypeStruct((T, D), F32))
        out_specs.insert(0, pl.BlockSpec((tm, D), row))
    res = pl.pallas_call(
        functools.partial(_in_proj_body, comb is not None),
        grid=(T // tm, N // tn),
        in_specs=in_specs, out_specs=out_specs, out_shape=out_shape,
        scratch_shapes=[pltpu.VMEM((tm, D), BF16)],
        compiler_params=_cparams("parallel", "arbitrary"),
        name="in_proj",
    )(*args)
    if comb is not None:
        return res[0], res[1]
    return h, res[0]


def _rope256(x, cos, sin_signed):
    xr = jnp.concatenate([pltpu.roll(x[:, :LANES], 64, 1), pltpu.roll(x[:, LANES:], 64, 1)], axis=1)
    return x * cos + xr * sin_signed


def _retention_body(dl_ref, q_ref, k_ref, v_ref, g_ref, cos_ref, sin_ref, y_ref, state_ref, obwd_ref,
                    *, n_chunks, n_blocks):
    C = RET_SCAN_CHUNK
    h = pl.program_id(1)
    phase = pl.program_id(2)
    n = pl.program_id(3)
    sb = n_chunks * C

    def log_gamma(s):
        x = jnp.full((1, 1), s, F32)
        return jnp.minimum(x, 0.0) - jnp.log(1.0 + jnp.exp(-jnp.abs(x)))

    lg_f = log_gamma(dl_ref[h])
    lg_b = log_gamma(dl_ref[RET_HEADS + h])
    pos = lax.broadcasted_iota(jnp.int32, (C, RET_QK_DIM), 0).astype(F32)
    k_scale = RET_QK_DIM ** -0.5

    @pl.when(n == 0)
    def _():
        state_ref[...] = jnp.zeros_like(state_ref)

    def load(rows, q_dec, k_dec):
        cos = cos_ref[rows, :]
        sin = sin_ref[rows, :]
        q = _rope256(q_ref[rows, :], cos, sin)
        k = _rope256(k_ref[rows, :], cos, sin)
        return q, k, v_ref[rows, :], q * q_dec, (k * k_dec).T

    @pl.when(phase == 0)
    def _():
        q_dec = jnp.exp((C - pos) * lg_b).astype(BF16)
        k_dec = (jnp.exp(pos * lg_b) * k_scale).astype(BF16)
        chunk_dec = jnp.exp(lg_b * C)
        base = (n_blocks - 1 - n) * sb

        def body(cc, carry):
            c = n_chunks - 1 - cc
            rows = pl.ds(pl.multiple_of(c * C, C), C)
            _, _, v, qd, kt = load(rows, q_dec, k_dec)
            st = state_ref[...]
            out = jnp.dot(qd, st.astype(BF16), preferred_element_type=F32)
            state_ref[...] = st * chunk_dec + jnp.dot(kt, v, preferred_element_type=F32)
            obwd_ref[pl.ds(pl.multiple_of(base + c * C, C), C), :] = out.astype(obwd_ref.dtype)
            return carry

        lax.fori_loop(0, n_chunks, body, 0, unroll=min(2, n_chunks))

    @pl.when(phase == 1)
    def _():
        q_dec = jnp.exp((pos + 1.0) * lg_f).astype(BF16)
        k_dec = (jnp.exp((C - 1.0 - pos) * lg_f) * k_scale).astype(BF16)
        chunk_dec = jnp.exp(lg_f * C)
        ii = lax.broadcasted_iota(jnp.int32, (C, C), 0)
        jj = lax.broadcasted_iota(jnp.int32, (C, C), 1)
        dist = (ii - jj).astype(F32)
        inner = (jnp.where(dist >= 0, jnp.exp(lg_f * jnp.maximum(dist, 0.0)), 0.0)
                 + jnp.where(dist <= 0, jnp.exp(lg_b * jnp.maximum(-dist, 0.0)), 0.0)) * k_scale
        base = n * sb

        def body(c, carry):
            rows = pl.ds(pl.multiple_of(c * C, C), C)
            q, k, v, qd, kt = load(rows, q_dec, k_dec)
            scores = lax.dot_general(q, k, (((1,), (1,)), ((), ())), preferred_element_type=F32) * inner
            st = state_ref[...]
            lhs = jnp.concatenate([qd, scores.astype(BF16)], axis=1)
            rhs = jnp.concatenate([st.astype(BF16), v], axis=0)
            o = jnp.dot(lhs, rhs, preferred_element_type=F32)
            state_ref[...] = st * chunk_dec + jnp.dot(kt, v, preferred_element_type=F32)
            o = o + obwd_ref[pl.ds(pl.multiple_of(base + c * C, C), C), :].astype(F32)
            o = o * lax.rsqrt(jnp.mean(o * o, axis=-1, keepdims=True) + NORM_EPS)
            g = g_ref[rows, :].astype(F32)
            y_ref[rows, :] = (g * jax.nn.sigmoid(g) * o).astype(y_ref.dtype)
            return carry

        lax.fori_loop(0, n_chunks, body, 0, unroll=min(2, n_chunks))


def _retention(proj, decay_logit, cos, sin_signed, B, S, sb=2048):
    sb = min(sb, S)
    nb = S // sb
    proj3 = proj.reshape(B, S, proj.shape[-1])
    blk = lambda p, n: jnp.where(p == 0, nb - 1 - n, n)
    kq = RET_QK_WIDTH // RET_QK_DIM
    kv = 2 * RET_QK_WIDTH // RET_V_DIM
    kg = kv + RET_HEADS
    in_specs = [
        pl.BlockSpec(memory_space=pltpu.SMEM),
        pl.BlockSpec((None, sb, RET_QK_DIM), lambda b, h, p, n: (b, blk(p, n), h)),
        pl.BlockSpec((None, sb, RET_QK_DIM), lambda b, h, p, n: (b, blk(p, n), kq + h)),
        pl.BlockSpec((None, sb, RET_V_DIM), lambda b, h, p, n: (b, blk(p, n), kv + h)),
        pl.BlockSpec((None, sb, RET_V_DIM), lambda b, h, p, n: (b, n * p, kg + h)),
        pl.BlockSpec((sb, RET_QK_DIM), lambda b, h, p, n: (blk(p, n), 0)),
        pl.BlockSpec((sb, RET_QK_DIM), lambda b, h, p, n: (blk(p, n), 0)),
    ]
    out = pl.pallas_call(
        functools.partial(_retention_body, n_chunks=sb // RET_SCAN_CHUNK, n_blocks=nb),
        grid=(B, RET_HEADS, 2, nb),
        in_specs=in_specs,
        out_specs=pl.BlockSpec((None, sb, RET_V_DIM), lambda b, h, p, n: (b, n * p, h)),
        out_shape=jax.ShapeDtypeStruct((B, S, RET_V_WIDTH), BF16),
        scratch_shapes=[pltpu.VMEM((RET_QK_DIM, RET_V_DIM), F32), pltpu.VMEM((S, RET_V_DIM), BF16)],
        compiler_params=_cparams("parallel", "parallel", "arbitrary", "arbitrary"),
        name="retention",
    )(decay_logit.reshape(2 * RET_HEADS).astype(F32), proj3, proj3, proj3, proj3,
      cos.astype(BF16), sin_signed.astype(BF16))
    return out.reshape(B * S, RET_V_WIDTH)


def _rope128(x, cos, sin_signed, low):
    xr = jnp.where(low, pltpu.roll(x, 96, 1), pltpu.roll(x, 32, 1))
    return x * cos + xr * sin_signed


def _qk_prep_body(p_ref, qg_ref, kg_ref, cos_ref, sin_ref, q_ref, k_ref, v_ref):
    d = ATTN_HEAD_DIM
    cos = cos_ref[...]
    sin = sin_ref[...]
    low = (lax.broadcasted_iota(jnp.int32, cos.shape, 1) % 64) < 32

    def norm_rope(x, gain):
        x = x.astype(F32)
        x = x * lax.rsqrt(jnp.mean(x * x, axis=-1, keepdims=True) + NORM_EPS) * gain
        return _rope128(x, cos, sin, low)

    q_scale = (d ** -0.5) * LOG2_E
    for hq in range(ATTN_Q_HEADS):
        q = norm_rope(p_ref[:, hq * d:(hq + 1) * d], qg_ref[...]) * q_scale
        q_ref[:, hq * d:(hq + 1) * d] = q.astype(q_ref.dtype)
    for hk in range(ATTN_KV_HEADS):
        c0 = (ATTN_Q_HEADS + hk) * d
        k_ref[:, hk * d:(hk + 1) * d] = norm_rope(p_ref[:, c0:c0 + d], kg_ref[...]).astype(k_ref.dtype)
        c1 = (ATTN_Q_HEADS + ATTN_KV_HEADS + hk) * d
        v_ref[:, 2 * hk * d:(2 * hk + 1) * d] = p_ref[:, c1:c1 + d]
        v_ref[:, (2 * hk + 1) * d:(2 * hk + 2) * d] = jnp.ones((p_ref.shape[0], d), v_ref.dtype)


def _qk_prep(proj, q_gain, k_gain, cos, sin_signed, S, tm=512):
    T = proj.shape[0]
    tm = min(tm, S)
    spb = S // tm
    d = ATTN_HEAD_DIM
    row = lambda i: (i, 0)
    return pl.pallas_call(
        _qk_prep_body,
        grid=(T // tm,),
        in_specs=[pl.BlockSpec((tm, ATTN_IN_WIDTH), row),
                  pl.BlockSpec((1, d), lambda i: (0, 0)), pl.BlockSpec((1, d), lambda i: (0, 0)),
                  pl.BlockSpec((tm, d), lambda i: (i % spb, 0)), pl.BlockSpec((tm, d), lambda i: (i % spb, 0))],
        out_specs=[pl.BlockSpec((tm, ATTN_Q_HEADS * d), row), pl.BlockSpec((tm, ATTN_KV_HEADS * d), row),
                   pl.BlockSpec((tm, 2 * ATTN_KV_HEADS * d), row)],
        out_shape=[jax.ShapeDtypeStruct((T, ATTN_Q_HEADS * d), BF16),
                   jax.ShapeDtypeStruct((T, ATTN_KV_HEADS * d), BF16),
                   jax.ShapeDtypeStruct((T, 2 * ATTN_KV_HEADS * d), BF16)],
        compiler_params=_cparams("parallel"),
        name="qk_prep",
    )(proj, q_gain.reshape(1, d), k_gain.reshape(1, d), cos, sin_signed)


def _attn_body(q_ref, k_ref, v_ref, o_ref, qs_ref, m_ref, acc_ref, *, tq, tk, n_kv):
    d = ATTN_HEAD_DIM
    for g in range(ATTN_GROUP):
        qs_ref[g * tq:(g + 1) * tq, :] = q_ref[:, g * d:(g + 1) * d]
    m_ref[...] = jnp.full_like(m_ref, -jnp.inf)
    acc_ref[...] = jnp.zeros_like(acc_ref)
    rep = tk // LANES

    def step(c, carry):
        rows = pl.ds(pl.multiple_of(c * tk, tk), tk)
        k = k_ref[rows, :]
        v = v_ref[rows, :]
        for g in range(ATTN_GROUP):
            r = slice(g * tq, (g + 1) * tq)
            s = lax.dot_general(qs_ref[r, :], k, (((1,), (1,)), ((), ())), preferred_element_type=F32)
            m_prev = m_ref[r, :]
            m_new = jnp.maximum(m_prev, jnp.max(s, axis=-1, keepdims=True))
            alpha = jnp.exp2(m_prev - m_new)
            p = jnp.exp2(s - jnp.concatenate([m_new] * rep, axis=1))
            pv = jnp.dot(p.astype(BF16), v, preferred_element_type=F32)
            acc_ref[r, :] = jnp.concatenate([alpha, alpha], axis=1) * acc_ref[r, :] + pv
            m_ref[r, :] = m_new
        return carry

    lax.fori_loop(0, n_kv, step, 0, unroll=min(4, n_kv))
    for g in range(ATTN_GROUP):
        r = slice(g * tq, (g + 1) * tq)
        o_ref[:, g * d:(g + 1) * d] = (acc_ref[r, :d] / acc_ref[r, d:]).astype(o_ref.dtype)


def _attention(qn, kn, vx, B, S, tq=512, tk=512):
    d = ATTN_HEAD_DIM
    tq = min(tq, S)
    tk = min(tk, S)
    gw = ATTN_GROUP * d
    q3 = qn.reshape(B, S, ATTN_Q_HEADS * d)
    k3 = kn.reshape(B, S, ATTN_KV_HEADS * d)
    v3 = vx.reshape(B, S, 2 * ATTN_KV_HEADS * d)
    m = ATTN_GROUP * tq
    out = pl.pallas_call(
        functools.partial(_attn_body, tq=tq, tk=tk, n_kv=S // tk),
        grid=(B, ATTN_KV_HEADS, S // tq),
        in_specs=[pl.BlockSpec((None, tq, gw), lambda b, kh, i: (b, i, kh)),
                  pl.BlockSpec((None, S, d), lambda b, kh, i: (b, 0, kh)),
                  pl.BlockSpec((None, S, 2 * d), lambda b, kh, i: (b, 0, kh))],
        out_specs=pl.BlockSpec((None, tq, gw), lambda b, kh, i: (b, i, kh)),
        out_shape=jax.ShapeDtypeStruct((B, S, ATTN_Q_HEADS * d), BF16),
        scratch_shapes=[pltpu.VMEM((m, d), BF16), pltpu.VMEM((m, LANES), F32), pltpu.VMEM((m, 2 * d), F32)],
        compiler_params=_cparams("parallel", "parallel", "arbitrary"),
        name="attention",
    )(q3, k3, v3)
    return out.reshape(B * S, ATTN_Q_HEADS * d)


ROUTER_GROUP_ROW = MOE_EXPERTS
SUBLANES = 8
assert MOE_EXPERTS_PER_GROUP == SUBLANES and MOE_GROUPS <= SUBLANES


def _route(logits_t):
    tm = logits_t.shape[1]
    row = lax.broadcasted_iota(jnp.int32, (SUBLANES, tm), 0)
    neg = jnp.float32(-jnp.inf)
    big = jnp.int32(SUBLANES)

    def first_argmax(x):
        mx = jnp.max(x, axis=0, keepdims=True)
        return mx, jnp.min(jnp.where(x == mx, row, big), axis=0, keepdims=True)

    gl = jnp.where(row < MOE_GROUPS, logits_t[ROUTER_GROUP_ROW:ROUTER_GROUP_ROW + SUBLANES, :], neg)
    gmax, g_sel = first_argmax(gl)
    g_w = 1.0 / jnp.sum(jnp.exp(gl - gmax), axis=0, keepdims=True)
    el = logits_t[0:SUBLANES, :]
    for g in range(1, MOE_GROUPS):
        el = jnp.where(g_sel == g, logits_t[g * SUBLANES:(g + 1) * SUBLANES, :], el)
    v1, i1 = first_argmax(el)
    v2, i2 = first_argmax(jnp.where(row == i1, neg, el))
    e2 = jnp.exp(v2 - v1)
    w1 = g_w / (1.0 + e2)
    w2 = g_w * e2 / (1.0 + e2)
    base = g_sel * MOE_EXPERTS_PER_GROUP
    return jnp.where(row == 0, (base + i1).astype(F32),
                     jnp.where(row == 1, (base + i2).astype(F32),
                               jnp.where(row == 2, w1, jnp.where(row == 3, w2, 0.0))))


def _out_proj_body(y_ref, h_ref, w_ref, g_ref, wr_ref, br_ref, hout_ref, xn_ref, rp_ref, rpt_ref):
    h = h_ref[...] + jnp.dot(y_ref[...], w_ref[...], preferred_element_type=F32)
    hout_ref[...] = h
    xn = h * lax.rsqrt(jnp.mean(h * h, axis=-1, keepdims=True) + NORM_EPS) * g_ref[...]
    xn_ref[...] = _pack_bf16_pairs(xn)
    xn_hi = xn.astype(BF16)
    xn_lo = (xn - xn_hi.astype(F32)).astype(BF16)
    wr = wr_ref[...]
    hi = jnp.dot(xn_hi, wr, preferred_element_type=F32)
    lo = jnp.dot(xn_lo, wr[:, :LANES], preferred_element_type=F32)
    logits = hi[:, :LANES] + hi[:, LANES:] + lo + br_ref[...]
    rpt = _route(logits.T)
    rpt_ref[...] = rpt
    full = jnp.concatenate([rpt, jnp.zeros((LANES - SUBLANES, rpt.shape[1]), F32)], axis=0)
    rp_ref[...] = full.T


def _out_proj(y, h, w, gain, w_router, b_router, tm=512):
    T, K = y.shape
    D = h.shape[1]
    tm = min(tm, T)
    row = lambda i: (i, 0)
    const = lambda i: (0, 0)
    return pl.pallas_call(
        _out_proj_body,
        grid=(T // tm,),
        in_specs=[pl.BlockSpec((tm, K), row), pl.BlockSpec((tm, D), row), pl.BlockSpec((K, D), const),
                  pl.BlockSpec((1, D), const), pl.BlockSpec((D, 2 * LANES), const), pl.BlockSpec((1, LANES), const)],
        out_specs=[pl.BlockSpec((tm, D), row), pl.BlockSpec((tm, D // 2), row), pl.BlockSpec((tm, LANES), row),
                   pl.BlockSpec((SUBLANES, tm), lambda i: (0, i))],
        out_shape=[jax.ShapeDtypeStruct((T, D), F32), jax.ShapeDtypeStruct((T, D // 2), U32),
                   jax.ShapeDtypeStruct((T, LANES), F32), jax.ShapeDtypeStruct((SUBLANES, T), F32)],
        compiler_params=_cparams("parallel"),
        name="out_proj",
    )(y, h, w, gain.reshape(1, D), w_router, b_router)


def _moe_body(be_ref, bv_ref, x_ref, wgu_ref, wd_ref, y_ref):
    b = pl.program_id(0)

    @pl.when(bv_ref[b] != 0)
    def _():
        half = wgu_ref.shape[0] // 2
        lo, hi = _unpack_bf16_pairs(x_ref[...])
        gu = (jnp.dot(lo.astype(BF16), wgu_ref[:half, :], preferred_element_type=F32)
              + jnp.dot(hi.astype(BF16), wgu_ref[half:, :], preferred_element_type=F32))
        gate = gu[:, :MOE_FF]
        up = gu[:, MOE_FF:]
        act = (gate * jax.nn.sigmoid(gate) * up).astype(BF16)
        y_ref[...] = _pack_bf16_pairs(jnp.dot(act, wd_ref[...], preferred_element_type=F32))

    @pl.when(bv_ref[b] == 0)
    def _():
        y_ref[...] = jnp.zeros_like(y_ref)


def _moe_experts(xs, block_e, block_valid, w_gate_up, w_down):
    P, W = xs.shape
    D = 2 * W
    n_blocks = P // MOE_BLOCK
    grid_spec = pltpu.PrefetchScalarGridSpec(
        num_scalar_prefetch=2,
        grid=(n_blocks,),
        in_specs=[pl.BlockSpec((MOE_BLOCK, W), lambda b, be, bv: (b, 0)),
                  pl.BlockSpec((None, D, 2 * MOE_FF), lambda b, be, bv: (be[b], 0, 0)),
                  pl.BlockSpec((None, MOE_FF, D), lambda b, be, bv: (be[b], 0, 0))],
        out_specs=pl.BlockSpec((MOE_BLOCK, W), lambda b, be, bv: (b, 0)),
    )
    return pl.pallas_call(
        _moe_body,
        grid_spec=grid_spec,
        out_shape=jax.ShapeDtypeStruct((P, W), U32),
        compiler_params=_cparams("arbitrary"),
        name="moe_experts",
    )(block_e, block_valid, xs, w_gate_up, w_down)


def _rank_body(rpt_ref, tri_ref, rank_ref, counts_ref, carry_ref, *, n_groups):
    @pl.when(pl.program_id(0) == 0)
    def _():
        carry_ref[...] = jnp.zeros_like(carry_ref)

    erow = lax.broadcasted_iota(I32, (MOE_EXPERTS, LANES), 0)
    carry = carry_ref[...]
    tri = tri_ref[...]
    r0, r1 = [], []
    for g in range(n_groups):
        c = slice(g * LANES, (g + 1) * LANES)
        oh0 = erow == rpt_ref[0:1, c].astype(I32)
        oh1 = erow == rpt_ref[1:2, c].astype(I32)
        both = jnp.where(oh0 | oh1, 1.0, 0.0)
        before = jnp.dot(both.astype(BF16), tri, preferred_element_type=F32) + carry
        r0.append(jnp.sum(jnp.where(oh0, before, 0.0), axis=0, keepdims=True))
        r1.append(jnp.sum(jnp.where(oh1, before, 0.0), axis=0, keepdims=True))
        carry = carry + jnp.sum(both, axis=1, keepdims=True)
    carry_ref[...] = carry
    counts_ref[...] = carry.astype(I32)
    tm = n_groups * LANES
    rank = jnp.concatenate([jnp.concatenate(r0, axis=1), jnp.concatenate(r1, axis=1),
                            jnp.zeros((SUBLANES - MOE_TOPK, tm), F32)], axis=0)
    rank_ref[...] = rank.astype(I32)


def _slot_ranks(rpt, tm=2048):
    T = rpt.shape[1]
    tm = min(tm, T)
    tri = (lax.broadcasted_iota(I32, (LANES, LANES), 0) < lax.broadcasted_iota(I32, (LANES, LANES), 1)).astype(BF16)
    return pl.pallas_call(
        functools.partial(_rank_body, n_groups=tm // LANES),
        grid=(T // tm,),
        in_specs=[pl.BlockSpec((SUBLANES, tm), lambda i: (0, i)), pl.BlockSpec((LANES, LANES), lambda i: (0, 0))],
        out_specs=[pl.BlockSpec((SUBLANES, tm), lambda i: (0, i)), pl.BlockSpec((MOE_EXPERTS, LANES), lambda i: (0, 0))],
        out_shape=[jax.ShapeDtypeStruct((SUBLANES, T), I32), jax.ShapeDtypeStruct((MOE_EXPERTS, LANES), I32)],
        scratch_shapes=[pltpu.VMEM((MOE_EXPERTS, LANES), F32)],
        compiler_params=_cparams("arbitrary"),
        name="slot_ranks",
    )(rpt, tri)


def _moe_dispatch_plan(rpt):
    T = rpt.shape[1]
    A = T * MOE_TOPK
    n_blocks = -(-(A + MOE_EXPERTS * (MOE_BLOCK - 1)) // MOE_BLOCK)
    rank, counts = _slot_ranks(rpt)
    counts = counts[:, 0]
    padded = ((counts + MOE_BLOCK - 1) // MOE_BLOCK) * MOE_BLOCK
    pends = jnp.cumsum(padded)
    pstarts = pends - padded
    e = rpt[:MOE_TOPK].astype(I32)
    experts = jnp.arange(MOE_EXPERTS, dtype=I32)[:, None, None]
    dest = jnp.sum(jnp.where(e[None] == experts, pstarts[:, None, None], 0), axis=0) + rank[:MOE_TOPK]
    block_start = jnp.arange(n_blocks, dtype=I32) * MOE_BLOCK
    block_e = jnp.minimum(jnp.sum(pends[None, :] <= block_start[:, None], axis=1), MOE_EXPERTS - 1).astype(I32)
    block_valid = (block_start < pends[-1]).astype(I32)
    return dest, block_e, block_valid, n_blocks * MOE_BLOCK


SC_CORES = 2
SC_SUBCORES = 16
SC_WORKERS = SC_CORES * SC_SUBCORES
SC_ROWS = 64


def _sc_mesh():
    return plsc.VectorSubcoreMesh(core_axis_name="c", subcore_axis_name="s")


def _sc_scratch(n_chunks, width):
    return [pltpu.VMEM((n_chunks, SC_ROWS), I32), pltpu.VMEM((n_chunks, SC_ROWS), I32),
            pltpu.VMEM((SC_ROWS, width), U32), pltpu.VMEM((SC_ROWS, width), U32),
            pltpu.SemaphoreType.DMA, pltpu.SemaphoreType.DMA, pltpu.SemaphoreType.DMA, pltpu.SemaphoreType.DMA]


def _sc_scatter_rows(x, dest, P):
    T, W = x.shape
    tw = T // SC_WORKERS
    nch = tw // SC_ROWS
    assert tw * SC_WORKERS == T and nch * SC_ROWS == tw and nch % 2 == 0
    idx = dest.reshape(MOE_TOPK, SC_WORKERS, nch, SC_ROWS)

    @functools.partial(pl.kernel, mesh=_sc_mesh(), out_type=jax.ShapeDtypeStruct((P, W), U32),
                       scratch_types=_sc_scratch(nch, W))
    def scatter(x_hbm, i0_hbm, i1_hbm, o_hbm, i0_v, i1_v, b0, b1, r0, r1, s0, s1):
        wid = lax.axis_index("s") * SC_CORES + lax.axis_index("c")
        base = wid * tw
        pltpu.sync_copy(i0_hbm.at[wid], i0_v)
        pltpu.sync_copy(i1_hbm.at[wid], i1_v)

        @pl.loop(0, nch, step=2)
        def _(j):
            ca = pltpu.async_copy(x_hbm.at[pl.ds(base + j * SC_ROWS, SC_ROWS)], b0, r0)
            cb = pltpu.async_copy(x_hbm.at[pl.ds(base + (j + 1) * SC_ROWS, SC_ROWS)], b1, r1)
            ca.wait()
            a0 = pltpu.async_copy(b0, o_hbm.at[i0_v.at[j]], s0)
            a1 = pltpu.async_copy(b0, o_hbm.at[i1_v.at[j]], s0)
            cb.wait()
            e0 = pltpu.async_copy(b1, o_hbm.at[i0_v.at[j + 1]], s1)
            e1 = pltpu.async_copy(b1, o_hbm.at[i1_v.at[j + 1]], s1)
            a0.wait()
            a1.wait()
            e0.wait()
            e1.wait()

    return scatter(x, idx[0], idx[1])


def _sc_gather_rows(table, dest):
    T = dest.shape[1]
    W = table.shape[1]
    tw = T // SC_WORKERS
    nch = tw // SC_ROWS
    assert tw * SC_WORKERS == T and nch * SC_ROWS == tw
    idx = dest.reshape(MOE_TOPK, SC_WORKERS, nch, SC_ROWS)
    out = jax.ShapeDtypeStruct((T, W), U32)

    @functools.partial(pl.kernel, mesh=_sc_mesh(), out_type=[out, out], scratch_types=_sc_scratch(nch, W))
    def gather(tab_hbm, i0_hbm, i1_hbm, o0_hbm, o1_hbm, i0_v, i1_v, b0, b1, g0, g1, w0, w1):
        wid = lax.axis_index("s") * SC_CORES + lax.axis_index("c")
        base = wid * tw
        pltpu.sync_copy(i0_hbm.at[wid], i0_v)
        pltpu.sync_copy(i1_hbm.at[wid], i1_v)

        @pl.loop(0, nch)
        def _(j):
            rows = pl.ds(base + j * SC_ROWS, SC_ROWS)
            c0 = pltpu.async_copy(tab_hbm.at[i0_v.at[j]], b0, g0)
            c1 = pltpu.async_copy(tab_hbm.at[i1_v.at[j]], b1, g1)
            c0.wait()
            d0 = pltpu.async_copy(b0, o0_hbm.at[rows], w0)
            c1.wait()
            d1 = pltpu.async_copy(b1, o1_hbm.at[rows], w1)
            d0.wait()
            d1.wait()

    return gather(table, idx[0], idx[1])


def _moe(xn, rpt, w_gate_up, w_down):
    dest, block_e, block_valid, P = _moe_dispatch_plan(rpt)
    xs = _sc_scatter_rows(xn, dest, P)
    y = _moe_experts(xs, block_e, block_valid, w_gate_up, w_down)
    return _sc_gather_rows(y, dest)


def _final_body(h_ref, y0_ref, y1_ref, rp_ref, g_ref, o_ref):
    h = _combine(h_ref[...], y0_ref[...], y1_ref[...], rp_ref[...])
    o_ref[...] = h * lax.rsqrt(jnp.mean(h * h, axis=-1, keepdims=True) + NORM_EPS) * g_ref[...]


def _final(h, y0, y1, rp, gain, tm=1024):
    T, D = h.shape
    tm = min(tm, T)
    row = lambda i: (i, 0)
    return pl.pallas_call(
        _final_body,
        grid=(T // tm,),
        in_specs=[pl.BlockSpec((tm, D), row), pl.BlockSpec((tm, D // 2), row), pl.BlockSpec((tm, D // 2), row),
                  pl.BlockSpec((tm, LANES), row), pl.BlockSpec((1, D), lambda i: (0, 0))],
        out_specs=pl.BlockSpec((tm, D), row),
        out_shape=jax.ShapeDtypeStruct((T, D), F32),
        compiler_params=_cparams("parallel"),
        name="final_norm",
    )(h, y0, y1, rp, gain.reshape(1, D))


def _rope_tables(n_tok, head_dim):
    t = jnp.arange(n_tok, dtype=jnp.int32)
    row = (t // GRID_W).astype(F32)
    col = (t % GRID_W).astype(F32)
    half = head_dim // 2
    inv = ROPE_THETA ** (-jnp.arange(0, half, 2, dtype=F32) / half)
    ang_r = row[:, None] * inv[None, :]
    ang_c = col[:, None] * inv[None, :]
    ang = jnp.concatenate([ang_r, ang_r, ang_c, ang_c], axis=-1)
    q = half // 2
    sign = jnp.where((jnp.arange(head_dim) % half) < q, -1.0, 1.0).astype(F32)
    return jnp.cos(ang), jnp.sin(ang) * sign[None, :]


def _router_weights(w_group, b_group, w_expert, b_expert):
    D = w_group.shape[0]
    pad = LANES - MOE_EXPERTS - MOE_GROUPS
    w = jnp.concatenate([w_expert, w_group, jnp.zeros((D, pad), F32)], axis=1).astype(F32)
    b = jnp.concatenate([b_expert, b_group, jnp.zeros((pad,), F32)]).reshape(1, LANES)
    w_hi = w.astype(BF16)
    w_lo = (w - w_hi.astype(F32)).astype(BF16)
    return jnp.concatenate([w_hi, w_lo], axis=1), b.astype(F32)


def _trunk(x, p):
    B, S, D = x.shape
    T = B * S
    cos_r, sin_r = _rope_tables(S, RET_QK_DIM)
    cos_a, sin_a = _rope_tables(S, ATTN_HEAD_DIM)
    h = x.reshape(T, D)

    h, proj = _in_proj(h, p["ln_mix"][0], p["ret_w_in"][0])
    y = _retention(proj, p["ret_decay_logit"][0], cos_r, sin_r, B, S)
    h, xn, rp, rpt = _out_proj(y, h, p["ret_w_out"][0], p["ln_ffn"][0], *p["router"][0])
    y0, y1 = _moe(xn, rpt, p["moe_w_gate_up"][0], p["moe_w_down"][0])

    h, proj = _in_proj(h, p["ln_mix"][1], p["attn_w_in"][0], comb=(y0, y1, rp))
    qn, kn, vx = _qk_prep(proj, p["attn_q_gain"][0], p["attn_k_gain"][0], cos_a, sin_a, S)
    y = _attention(qn, kn, vx, B, S)
    h, xn, rp, rpt = _out_proj(y, h, p["attn_w_out"][0], p["ln_ffn"][1], *p["router"][1])
    y0, y1 = _moe(xn, rpt, p["moe_w_gate_up"][1], p["moe_w_down"][1])

    return _final(h, y0, y1, rp, p["ln_final"]).reshape(B, S, D)


def kernel(x_prompt, x_sample, ln_mix, ret_w_in, ret_decay_logit, ret_w_out, attn_w_in, attn_q_gain, attn_k_gain,
           attn_w_out, ln_ffn, moe_w_group, moe_b_group, moe_w_expert, moe_b_expert, moe_w_gate_up, moe_w_down,
           ln_final):
    p = {
        "ln_mix": ln_mix, "ln_ffn": ln_ffn, "ln_final": ln_final,
        "ret_w_in": ret_w_in.astype(BF16), "ret_decay_logit": ret_decay_logit, "ret_w_out": ret_w_out.astype(BF16),
        "attn_w_in": attn_w_in.astype(BF16), "attn_q_gain": attn_q_gain, "attn_k_gain": attn_k_gain,
        "attn_w_out": attn_w_out.astype(BF16),
        "router": [_router_weights(moe_w_group[i], moe_b_group[i], moe_w_expert[i], moe_b_expert[i])
                   for i in range(moe_w_group.shape[0])],
        "moe_w_gate_up": moe_w_gate_up.astype(BF16), "moe_w_down": moe_w_down.astype(BF16),
    }
    return _trunk(x_prompt, p), _trunk(x_sample, p)
```

```python
import functools

import jax
import jax.numpy as jnp
from jax import lax
from jax.experimental import pallas as pl
from jax.experimental.pallas import tpu as pltpu
from jax.experimental.pallas import tpu_sc as plsc

F32 = jnp.float32
BF16 = jnp.bfloat16
U32 = jnp.uint32
I32 = jnp.int32

D_MODEL = 1024
GRID_W = 64
ROPE_THETA = 10000.0
NORM_EPS = 1e-6

RET_HEADS = 4
RET_QK_DIM = 256
RET_V_DIM = 512
RET_SCAN_CHUNK = 256
RET_QK_WIDTH = RET_HEADS * RET_QK_DIM
RET_V_WIDTH = RET_HEADS * RET_V_DIM

ATTN_Q_HEADS = 8
ATTN_KV_HEADS = 2
ATTN_HEAD_DIM = 128
ATTN_GROUP = ATTN_Q_HEADS // ATTN_KV_HEADS
ATTN_IN_WIDTH = (ATTN_Q_HEADS + 2 * ATTN_KV_HEADS) * ATTN_HEAD_DIM

MOE_GROUPS = 4
MOE_EXPERTS_PER_GROUP = 8
MOE_EXPERTS = MOE_GROUPS * MOE_EXPERTS_PER_GROUP
MOE_TOPK = 2
MOE_FF = 512
MOE_BLOCK = 512

LANES = 128
LOG2_E = 1.4426950408889634
VMEM_LIMIT = 56 * 1024 * 1024


def _cparams(*sem):
    return pltpu.CompilerParams(dimension_semantics=sem, vmem_limit_bytes=VMEM_LIMIT)


def _pack_bf16_pairs(x):
    w = x.shape[1] // 2
    lo = lax.bitcast_convert_type(x[:, :w].astype(BF16).astype(F32), U32)
    hi = lax.bitcast_convert_type(x[:, w:].astype(BF16).astype(F32), U32)
    return lax.shift_right_logical(lo, jnp.uint32(16)) | (hi & jnp.uint32(0xFFFF0000))


def _unpack_bf16_pairs(u):
    lo = lax.bitcast_convert_type(lax.shift_left(u, jnp.uint32(16)), F32)
    hi = lax.bitcast_convert_type(u & jnp.uint32(0xFFFF0000), F32)
    return lo, hi


def _combine(h, y0_u32, y1_u32, rp):
    lo0, hi0 = _unpack_bf16_pairs(y0_u32)
    lo1, hi1 = _unpack_bf16_pairs(y1_u32)
    w0 = rp[:, 2:3]
    w1 = rp[:, 3:4]
    half = h.shape[1] // 2
    return jnp.concatenate([h[:, :half] + w0 * lo0 + w1 * lo1, h[:, half:] + w0 * hi0 + w1 * hi1], axis=1)


def _in_proj_body(combine, *refs):
    if combine:
        h_ref, y0_ref, y1_ref, rp_ref, g_ref, w_ref, hout_ref, o_ref, xn_ref = refs
    else:
        h_ref, g_ref, w_ref, o_ref, xn_ref = refs

    @pl.when(pl.program_id(1) == 0)
    def _():
        h = h_ref[...]
        if combine:
            h = _combine(h, y0_ref[...], y1_ref[...], rp_ref[...])
            hout_ref[...] = h
        xn = h * lax.rsqrt(jnp.mean(h * h, axis=-1, keepdims=True) + NORM_EPS) * g_ref[...]
        xn_ref[...] = xn.astype(BF16)

    o_ref[...] = jnp.dot(xn_ref[...], w_ref[...], preferred_element_type=F32).astype(o_ref.dtype)


def _in_proj(h, gain, w, comb=None, tm=1024, tn=1536):
    T, D = h.shape
    N = w.shape[1]
    tm = min(tm, T)
    tn = min(tn, N)
    row = lambda i, j: (i, 0)
    in_specs = [pl.BlockSpec((tm, D), row)]
    args = [h]
    if comb is not None:
        y0, y1, rp = comb
        in_specs += [pl.BlockSpec((tm, D // 2), row), pl.BlockSpec((tm, D // 2), row),
                     pl.BlockSpec((tm, LANES), row)]
        args += [y0, y1, rp]
    in_specs += [pl.BlockSpec((1, D), lambda i, j: (0, 0)), pl.BlockSpec((D, tn), lambda i, j: (0, j))]
    args += [gain.reshape(1, D), w]
    out_shape = [jax.ShapeDtypeStruct((T, N), BF16)]
    out_specs = [pl.BlockSpec((tm, tn), lambda i, j: (i, j))]
    if comb is not None:
        out_shape.insert(0, jax.ShapeDtypeStruct((T, D), F32))
        out_specs.insert(0, pl.BlockSpec((tm, D), row))
    res = pl.pallas_call(
        functools.partial(_in_proj_body, comb is not None),
        grid=(T // tm, N // tn),
        in_specs=in_specs, out_specs=out_specs, out_shape=out_shape,
        scratch_shapes=[pltpu.VMEM((tm, D), BF16)],
        compiler_params=_cparams("parallel", "arbitrary"),
        name="in_proj",
    )(*args)
    if comb is not None:
        return res[0], res[1]
    return h, res[0]


def _rope256(x, cos, sin_signed):
    xr = jnp.concatenate([pltpu.roll(x[:, :LANES], 64, 1), pltpu.roll(x[:, LANES:], 64, 1)], axis=1)
    return x * cos + xr * sin_signed


def _retention_body(dl_ref, q_ref, k_ref, v_ref, g_ref, cos_ref, sin_ref, y_ref, state_ref, obwd_ref,
                    *, n_chunks, n_blocks):
    C = RET_SCAN_CHUNK
    h = pl.program_id(1)
    phase = pl.program_id(2)
    n = pl.program_id(3)
    sb = n_chunks * C

    def log_gamma(s):
        x = jnp.full((1, 1), s, F32)
        return jnp.minimum(x, 0.0) - jnp.log(1.0 + jnp.exp(-jnp.abs(x)))

    lg_f = log_gamma(dl_ref[h])
    lg_b = log_gamma(dl_ref[RET_HEADS + h])
    pos = lax.broadcasted_iota(jnp.int32, (C, RET_QK_DIM), 0).astype(F32)
    k_scale = RET_QK_DIM ** -0.5

    @pl.when(n == 0)
    def _():
        state_ref[...] = jnp.zeros_like(state_ref)

    def load(rows, q_dec, k_dec):
        cos = cos_ref[rows, :]
        sin = sin_ref[rows, :]
        q = _rope256(q_ref[rows, :], cos, sin)
        k = _rope256(k_ref[rows, :], cos, sin)
        return q, k, v_ref[rows, :], q * q_dec, (k * k_dec).T

    @pl.when(phase == 0)
    def _():
        q_dec = jnp.exp((C - pos) * lg_b).astype(BF16)
        k_dec = (jnp.exp(pos * lg_b) * k_scale).astype(BF16)
        chunk_dec = jnp.exp(lg_b * C)
        base = (n_blocks - 1 - n) * sb

        def body(cc, carry):
            c = n_chunks - 1 - cc
            rows = pl.ds(pl.multiple_of(c * C, C), C)
            _, _, v, qd, kt = load(rows, q_dec, k_dec)
            st = state_ref[...]
            out = jnp.dot(qd, st.astype(BF16), preferred_element_type=F32)
            state_ref[...] = st * chunk_dec + jnp.dot(kt, v, preferred_element_type=F32)
            obwd_ref[pl.ds(pl.multiple_of(base + c * C, C), C), :] = out.astype(obwd_ref.dtype)
            return carry

        lax.fori_loop(0, n_chunks, body, 0, unroll=min(2, n_chunks))

    @pl.when(phase == 1)
    def _():
        q_dec = jnp.exp((pos + 1.0) * lg_f).astype(BF16)
        k_dec = (jnp.exp((C - 1.0 - pos) * lg_f) * k_scale).astype(BF16)
        chunk_dec = jnp.exp(lg_f * C)
        ii = lax.broadcasted_iota(jnp.int32, (C, C), 0)
        jj = lax.broadcasted_iota(jnp.int32, (C, C), 1)
        dist = (ii - jj).astype(F32)
        inner = (jnp.where(dist >= 0, jnp.exp(lg_f * jnp.maximum(dist, 0.0)), 0.0)
                 + jnp.where(dist <= 0, jnp.exp(lg_b * jnp.maximum(-dist, 0.0)), 0.0)) * k_scale
        base = n * sb

        def body(c, carry):
            rows = pl.ds(pl.multiple_of(c * C, C), C)
            q, k, v, qd, kt = load(rows, q_dec, k_dec)
            scores = lax.dot_general(q, k, (((1,), (1,)), ((), ())), preferred_element_type=F32) * inner
            st = state_ref[...]
            lhs = jnp.concatenate([qd, scores.astype(BF16)], axis=1)
            rhs = jnp.concatenate([st.astype(BF16), v], axis=0)
            o = jnp.dot(lhs, rhs, preferred_element_type=F32)
            state_ref[...] = st * chunk_dec + jnp.dot(kt, v, preferred_element_type=F32)
            o = o + obwd_ref[pl.ds(pl.multiple_of(base + c * C, C), C), :].astype(F32)
            o = o * lax.rsqrt(jnp.mean(o * o, axis=-1, keepdims=True) + NORM_EPS)
            g = g_ref[rows, :].astype(F32)
            y_ref[rows, :] = (g * jax.nn.sigmoid(g) * o).astype(y_ref.dtype)
            return carry

        lax.fori_loop(0, n_chunks, body, 0, unroll=min(2, n_chunks))


def _retention(proj, decay_logit, cos, sin_signed, B, S, sb=2048):
    sb = min(sb, S)
    nb = S // sb
    proj3 = proj.reshape(B, S, proj.shape[-1])
    blk = lambda p, n: jnp.where(p == 0, nb - 1 - n, n)
    kq = RET_QK_WIDTH // RET_QK_DIM
    kv = 2 * RET_QK_WIDTH // RET_V_DIM
    kg = kv + RET_HEADS
    in_specs = [
        pl.BlockSpec(memory_space=pltpu.SMEM),
        pl.BlockSpec((None, sb, RET_QK_DIM), lambda b, h, p, n: (b, blk(p, n), h)),
        pl.BlockSpec((None, sb, RET_QK_DIM), lambda b, h, p, n: (b, blk(p, n), kq + h)),
        pl.BlockSpec((None, sb, RET_V_DIM), lambda b, h, p, n: (b, blk(p, n), kv + h)),
        pl.BlockSpec((None, sb, RET_V_DIM), lambda b, h, p, n: (b, n * p, kg + h)),
        pl.BlockSpec((sb, RET_QK_DIM), lambda b, h, p, n: (blk(p, n), 0)),
        pl.BlockSpec((sb, RET_QK_DIM), lambda b, h, p, n: (blk(p, n), 0)),
    ]
    out = pl.pallas_call(
        functools.partial(_retention_body, n_chunks=sb // RET_SCAN_CHUNK, n_blocks=nb),
        grid=(B, RET_HEADS, 2, nb),
        in_specs=in_specs,
        out_specs=pl.BlockSpec((None, sb, RET_V_DIM), lambda b, h, p, n: (b, n * p, h)),
        out_shape=jax.ShapeDtypeStruct((B, S, RET_V_WIDTH), BF16),
        scratch_shapes=[pltpu.VMEM((RET_QK_DIM, RET_V_DIM), F32), pltpu.VMEM((S, RET_V_DIM), BF16)],
        compiler_params=_cparams("parallel", "parallel", "arbitrary", "arbitrary"),
        name="retention",
    )(decay_logit.reshape(2 * RET_HEADS).astype(F32), proj3, proj3, proj3, proj3,
      cos.astype(BF16), sin_signed.astype(BF16))
    return out.reshape(B * S, RET_V_WIDTH)


def _rope128(x, cos, sin_signed, low):
    xr = jnp.where(low, pltpu.roll(x, 96, 1), pltpu.roll(x, 32, 1))
    return x * cos + xr * sin_signed


def _norm_rope128(x, gain, cos, sin_signed):
    low = (lax.broadcasted_iota(jnp.int32, cos.shape, 1) % 64) < 32
    x = x.astype(F32)
    x = x * lax.rsqrt(jnp.mean(x * x, axis=-1, keepdims=True) + NORM_EPS) * gain
    return _rope128(x, cos, sin_signed, low)


def _kv_prep_body(p_ref, kg_ref, cos_ref, sin_ref, k_ref, v_ref):
    d = ATTN_HEAD_DIM
    for hk in range(ATTN_KV_HEADS):
        k = _norm_rope128(p_ref[:, hk * d:(hk + 1) * d], kg_ref[...], cos_ref[...], sin_ref[...])
        k_ref[:, hk * d:(hk + 1) * d] = k.astype(k_ref.dtype)
        c1 = (ATTN_KV_HEADS + hk) * d
        v_ref[:, 2 * hk * d:(2 * hk + 1) * d] = p_ref[:, c1:c1 + d]
        v_ref[:, (2 * hk + 1) * d:(2 * hk + 2) * d] = jnp.ones((p_ref.shape[0], d), v_ref.dtype)


def _kv_prep(proj, k_gain, cos, sin_signed, S, tm=1024):
    T = proj.shape[0]
    tm = min(tm, S)
    spb = S // tm
    d = ATTN_HEAD_DIM
    kvw = 2 * ATTN_KV_HEADS * d
    assert (ATTN_Q_HEADS * d) % kvw == 0
    row = lambda i: (i, 0)
    return pl.pallas_call(
        _kv_prep_body,
        grid=(T // tm,),
        in_specs=[pl.BlockSpec((tm, kvw), lambda i: (i, ATTN_Q_HEADS * d // kvw)),
                  pl.BlockSpec((1, d), lambda i: (0, 0)),
                  pl.BlockSpec((tm, d), lambda i: (i % spb, 0)), pl.BlockSpec((tm, d), lambda i: (i % spb, 0))],
        out_specs=[pl.BlockSpec((tm, ATTN_KV_HEADS * d), row), pl.BlockSpec((tm, kvw), row)],
        out_shape=[jax.ShapeDtypeStruct((T, ATTN_KV_HEADS * d), BF16), jax.ShapeDtypeStruct((T, kvw), BF16)],
        compiler_params=_cparams("parallel"),
        name="kv_prep",
    )(proj, k_gain.reshape(1, d), cos, sin_signed)


def _attn_body(q_ref, qg_ref, cos_ref, sin_ref, k_ref, v_ref, o_ref, qs_ref, m_ref, acc_ref, *, tq, tk, n_kv):
    d = ATTN_HEAD_DIM
    q_scale = (d ** -0.5) * LOG2_E
    for g in range(ATTN_GROUP):
        q = _norm_rope128(q_ref[:, g * d:(g + 1) * d], qg_ref[...], cos_ref[...], sin_ref[...]) * q_scale
        qs_ref[g * tq:(g + 1) * tq, :] = q.astype(qs_ref.dtype)
    m_ref[...] = jnp.full_like(m_ref, -jnp.inf)
    acc_ref[...] = jnp.zeros_like(acc_ref)
    rep = tk // LANES

    def step(c, carry):
        rows = pl.ds(pl.multiple_of(c * tk, tk), tk)
        k = k_ref[rows, :]
        v = v_ref[rows, :]
        for g in range(ATTN_GROUP):
            r = slice(g * tq, (g + 1) * tq)
            s = lax.dot_general(qs_ref[r, :], k, (((1,), (1,)), ((), ())), preferred_element_type=F32)
            m_prev = m_ref[r, :]
            m_new = jnp.maximum(m_prev, jnp.max(s, axis=-1, keepdims=True))
            alpha = jnp.exp2(m_prev - m_new)
            p = jnp.exp2(s - jnp.concatenate([m_new] * rep, axis=1))
            pv = jnp.dot(p.astype(BF16), v, preferred_element_type=F32)
            acc_ref[r, :] = jnp.concatenate([alpha, alpha], axis=1) * acc_ref[r, :] + pv
            m_ref[r, :] = m_new
        return carry

    lax.fori_loop(0, n_kv, step, 0, unroll=min(8, n_kv))
    for g in range(ATTN_GROUP):
        r = slice(g * tq, (g + 1) * tq)
        o_ref[:, g * d:(g + 1) * d] = (acc_ref[r, :d] / acc_ref[r, d:]).astype(o_ref.dtype)


def _attention(proj, q_gain, cos, sin_signed, kn, vx, B, S, tq=512, tk=256):
    d = ATTN_HEAD_DIM
    tq = min(tq, S)
    tk = min(tk, S)
    gw = ATTN_GROUP * d
    p3 = proj.reshape(B, S, ATTN_IN_WIDTH)
    k3 = kn.reshape(B, S, ATTN_KV_HEADS * d)
    v3 = vx.reshape(B, S, 2 * ATTN_KV_HEADS * d)
    m = ATTN_GROUP * tq
    out = pl.pallas_call(
        functools.partial(_attn_body, tq=tq, tk=tk, n_kv=S // tk),
        grid=(B, ATTN_KV_HEADS, S // tq),
        in_specs=[pl.BlockSpec((None, tq, gw), lambda b, kh, i: (b, i, kh)),
                  pl.BlockSpec((1, d), lambda b, kh, i: (0, 0)),
                  pl.BlockSpec((tq, d), lambda b, kh, i: (i, 0)), pl.BlockSpec((tq, d), lambda b, kh, i: (i, 0)),
                  pl.BlockSpec((None, S, d), lambda b, kh, i: (b, 0, kh)),
                  pl.BlockSpec((None, S, 2 * d), lambda b, kh, i: (b, 0, kh))],
        out_specs=pl.BlockSpec((None, tq, gw), lambda b, kh, i: (b, i, kh)),
        out_shape=jax.ShapeDtypeStruct((B, S, ATTN_Q_HEADS * d), BF16),
        scratch_shapes=[pltpu.VMEM((m, d), BF16), pltpu.VMEM((m, LANES), F32), pltpu.VMEM((m, 2 * d), F32)],
        compiler_params=_cparams("parallel", "parallel", "arbitrary"),
        name="attention",
    )(p3, q_gain.reshape(1, d), cos, sin_signed, k3, v3)
    return out.reshape(B * S, ATTN_Q_HEADS * d)


ROUTER_GROUP_ROW = MOE_EXPERTS
SUBLANES = 8
assert MOE_EXPERTS_PER_GROUP == SUBLANES and MOE_GROUPS <= SUBLANES


def _route(logits_t):
    tm = logits_t.shape[1]
    row = lax.broadcasted_iota(jnp.int32, (SUBLANES, tm), 0)
    neg = jnp.float32(-jnp.inf)
    big = jnp.int32(SUBLANES)

    def first_argmax(x):
        mx = jnp.max(x, axis=0, keepdims=True)
        return mx, jnp.min(jnp.where(x == mx, row, big), axis=0, keepdims=True)

    gl = jnp.where(row < MOE_GROUPS, logits_t[ROUTER_GROUP_ROW:ROUTER_GROUP_ROW + SUBLANES, :], neg)
    gmax, g_sel = first_argmax(gl)
    g_w = 1.0 / jnp.sum(jnp.exp(gl - gmax), axis=0, keepdims=True)
    el = logits_t[0:SUBLANES, :]
    for g in range(1, MOE_GROUPS):
        el = jnp.where(g_sel == g, logits_t[g * SUBLANES:(g + 1) * SUBLANES, :], el)
    v1, i1 = first_argmax(el)
    v2, i2 = first_argmax(jnp.where(row == i1, neg, el))
    e2 = jnp.exp(v2 - v1)
    w1 = g_w / (1.0 + e2)
    w2 = g_w * e2 / (1.0 + e2)
    base = g_sel * MOE_EXPERTS_PER_GROUP
    return jnp.where(row == 0, (base + i1).astype(F32),
                     jnp.where(row == 1, (base + i2).astype(F32),
                               jnp.where(row == 2, w1, jnp.where(row == 3, w2, 0.0))))


def _out_proj_body(y_ref, h_ref, w_ref, g_ref, wr_ref, br_ref, hout_ref, xn_ref, rp_ref, rpt_ref):
    h = h_ref[...] + jnp.dot(y_ref[...], w_ref[...], preferred_element_type=F32)
    hout_ref[...] = h
    xn = h * lax.rsqrt(jnp.mean(h * h, axis=-1, keepdims=True) + NORM_EPS) * g_ref[...]
    xn_ref[...] = _pack_bf16_pairs(xn)
    xn_hi = xn.astype(BF16)
    xn_lo = (xn - xn_hi.astype(F32)).astype(BF16)
    wr = wr_ref[...]
    hi = jnp.dot(xn_hi, wr, preferred_element_type=F32)
    lo = jnp.dot(xn_lo, wr[:, :LANES], preferred_element_type=F32)
    logits = hi[:, :LANES] + hi[:, LANES:] + lo + br_ref[...]
    rpt = _route(logits.T)
    rpt_ref[...] = rpt
    full = jnp.concatenate([rpt, jnp.zeros((LANES - SUBLANES, rpt.shape[1]), F32)], axis=0)
    rp_ref[...] = full.T


def _out_proj(y, h, w, gain, w_router, b_router, tm=512):
    T, K = y.shape
    D = h.shape[1]
    tm = min(tm, T)
    row = lambda i: (i, 0)
    const = lambda i: (0, 0)
    return pl.pallas_call(
        _out_proj_body,
        grid=(T // tm,),
        in_specs=[pl.BlockSpec((tm, K), row), pl.BlockSpec((tm, D), row), pl.BlockSpec((K, D), const),
                  pl.BlockSpec((1, D), const), pl.BlockSpec((D, 2 * LANES), const), pl.BlockSpec((1, LANES), const)],
        out_specs=[pl.BlockSpec((tm, D), row), pl.BlockSpec((tm, D // 2), row), pl.BlockSpec((tm, LANES), row),
                   pl.BlockSpec((SUBLANES, tm), lambda i: (0, i))],
        out_shape=[jax.ShapeDtypeStruct((T, D), F32), jax.ShapeDtypeStruct((T, D // 2), U32),
                   jax.ShapeDtypeStruct((T, LANES), F32), jax.ShapeDtypeStruct((SUBLANES, T), F32)],
        compiler_params=_cparams("parallel"),
        name="out_proj",
    )(y, h, w, gain.reshape(1, D), w_router, b_router)


def _moe_body(be_ref, bv_ref, x_ref, wgu_ref, wd_ref, y_ref, wgu_bf_ref, wd_bf_ref):
    b = pl.program_id(0)

    @pl.when((b == 0) | (be_ref[b] != be_ref[jnp.maximum(b - 1, 0)]))
    def _():
        wgu_bf_ref[...] = wgu_ref[...].astype(BF16)
        wd_bf_ref[...] = wd_ref[...].astype(BF16)

    @pl.when(bv_ref[b] != 0)
    def _():
        lo, hi = _unpack_bf16_pairs(x_ref[...])
        x = jnp.concatenate([lo.astype(BF16), hi.astype(BF16)], axis=1)
        gu = jnp.dot(x, wgu_bf_ref[...], preferred_element_type=F32)
        gate = gu[:, :MOE_FF]
        up = gu[:, MOE_FF:]
        act = (gate * jax.nn.sigmoid(gate) * up).astype(BF16)
        y_ref[...] = _pack_bf16_pairs(jnp.dot(act, wd_bf_ref[...], preferred_element_type=F32))

    @pl.when(bv_ref[b] == 0)
    def _():
        y_ref[...] = jnp.zeros_like(y_ref)


def _moe_experts(xs, block_e, block_valid, w_gate_up, w_down):
    P, W = xs.shape
    D = 2 * W
    n_blocks = P // MOE_BLOCK
    grid_spec = pltpu.PrefetchScalarGridSpec(
        num_scalar_prefetch=2,
        grid=(n_blocks,),
        in_specs=[pl.BlockSpec((MOE_BLOCK, W), lambda b, be, bv: (b, 0)),
                  pl.BlockSpec((None, D, 2 * MOE_FF), lambda b, be, bv: (be[b], 0, 0)),
                  pl.BlockSpec((None, MOE_FF, D), lambda b, be, bv: (be[b], 0, 0))],
        out_specs=pl.BlockSpec((MOE_BLOCK, W), lambda b, be, bv: (b, 0)),
        scratch_shapes=[pltpu.VMEM((D, 2 * MOE_FF), BF16), pltpu.VMEM((MOE_FF, D), BF16)],
    )
    return pl.pallas_call(
        _moe_body,
        grid_spec=grid_spec,
        out_shape=jax.ShapeDtypeStruct((P, W), U32),
        compiler_params=_cparams("arbitrary"),
        name="moe_experts",
    )(block_e, block_valid, xs, w_gate_up, w_down)


def _rank_body(rpt_ref, tri_ref, rank_ref, counts_ref, carry_ref, *, n_groups):
    @pl.when(pl.program_id(0) == 0)
    def _():
        carry_ref[...] = jnp.zeros_like(carry_ref)

    erow = lax.broadcasted_iota(I32, (MOE_EXPERTS, LANES), 0)
    carry = carry_ref[...]
    tri = tri_ref[...]
    r0, r1 = [], []
    for g in range(n_groups):
        c = slice(g * LANES, (g + 1) * LANES)
        oh0 = erow == rpt_ref[0:1, c].astype(I32)
        oh1 = erow == rpt_ref[1:2, c].astype(I32)
        both = jnp.where(oh0 | oh1, 1.0, 0.0)
        before = jnp.dot(both.astype(BF16), tri, preferred_element_type=F32) + carry
        r0.append(jnp.sum(jnp.where(oh0, before, 0.0), axis=0, keepdims=True))
        r1.append(jnp.sum(jnp.where(oh1, before, 0.0), axis=0, keepdims=True))
        carry = carry + jnp.sum(both, axis=1, keepdims=True)
    carry_ref[...] = carry
    counts_ref[...] = carry.astype(I32)
    tm = n_groups * LANES
    rank = jnp.concatenate([jnp.concatenate(r0, axis=1), jnp.concatenate(r1, axis=1),
                            jnp.zeros((SUBLANES - MOE_TOPK, tm), F32)], axis=0)
    rank_ref[...] = rank.astype(I32)


def _slot_ranks(rpt, tm=2048):
    T = rpt.shape[1]
    tm = min(tm, T)
    tri = (lax.broadcasted_iota(I32, (LANES, LANES), 0) < lax.broadcasted_iota(I32, (LANES, LANES), 1)).astype(BF16)
    return pl.pallas_call(
        functools.partial(_rank_body, n_groups=tm // LANES),
        grid=(T // tm,),
        in_specs=[pl.BlockSpec((SUBLANES, tm), lambda i: (0, i)), pl.BlockSpec((LANES, LANES), lambda i: (0, 0))],
        out_specs=[pl.BlockSpec((SUBLANES, tm), lambda i: (0, i)), pl.BlockSpec((MOE_EXPERTS, LANES), lambda i: (0, 0))],
        out_shape=[jax.ShapeDtypeStruct((SUBLANES, T), I32), jax.ShapeDtypeStruct((MOE_EXPERTS, LANES), I32)],
        scratch_shapes=[pltpu.VMEM((MOE_EXPERTS, LANES), F32)],
        compiler_params=_cparams("arbitrary"),
        name="slot_ranks",
    )(rpt, tri)


def _moe_dispatch_plan(rpt):
    T = rpt.shape[1]
    A = T * MOE_TOPK
    n_blocks = -(-(A + MOE_EXPERTS * (MOE_BLOCK - 1)) // MOE_BLOCK)
    rank, counts = _slot_ranks(rpt)
    counts = counts[:, 0]
    padded = ((counts + MOE_BLOCK - 1) // MOE_BLOCK) * MOE_BLOCK
    pends = jnp.cumsum(padded)
    pstarts = pends - padded
    e = rpt[:MOE_TOPK].astype(I32)
    experts = jnp.arange(MOE_EXPERTS, dtype=I32)[:, None, None]
    dest = jnp.sum(jnp.where(e[None] == experts, pstarts[:, None, None], 0), axis=0) + rank[:MOE_TOPK]
    block_start = jnp.arange(n_blocks, dtype=I32) * MOE_BLOCK
    block_e = jnp.minimum(jnp.sum(pends[None, :] <= block_start[:, None], axis=1), MOE_EXPERTS - 1).astype(I32)
    block_valid = (block_start < pends[-1]).astype(I32)
    return dest, block_e, block_valid, n_blocks * MOE_BLOCK


SC_CORES = 2
SC_SUBCORES = 16
SC_WORKERS = SC_CORES * SC_SUBCORES
SC_ROWS = 64


def _sc_mesh():
    return plsc.VectorSubcoreMesh(core_axis_name="c", subcore_axis_name="s")


def _sc_scratch(n_chunks, width):
    return [pltpu.VMEM((n_chunks, SC_ROWS), I32), pltpu.VMEM((n_chunks, SC_ROWS), I32),
            pltpu.VMEM((SC_ROWS, width), U32), pltpu.VMEM((SC_ROWS, width), U32),
            pltpu.SemaphoreType.DMA, pltpu.SemaphoreType.DMA, pltpu.SemaphoreType.DMA, pltpu.SemaphoreType.DMA]


def _sc_scatter_rows(x, dest, P):
    T, W = x.shape
    tw = T // SC_WORKERS
    nch = tw // SC_ROWS
    assert tw * SC_WORKERS == T and nch * SC_ROWS == tw and nch % 2 == 0
    idx = dest.reshape(MOE_TOPK, SC_WORKERS, nch, SC_ROWS)

    @functools.partial(pl.kernel, mesh=_sc_mesh(), out_type=jax.ShapeDtypeStruct((P, W), U32),
                       scratch_types=_sc_scratch(nch, W))
    def scatter(x_hbm, i0_hbm, i1_hbm, o_hbm, i0_v, i1_v, b0, b1, r0, r1, s0, s1):
        wid = lax.axis_index("s") * SC_CORES + lax.axis_index("c")
        base = wid * tw
        pltpu.sync_copy(i0_hbm.at[wid], i0_v)
        pltpu.sync_copy(i1_hbm.at[wid], i1_v)

        @pl.loop(0, nch, step=2)
        def _(j):
            ca = pltpu.async_copy(x_hbm.at[pl.ds(base + j * SC_ROWS, SC_ROWS)], b0, r0)
            cb = pltpu.async_copy(x_hbm.at[pl.ds(base + (j + 1) * SC_ROWS, SC_ROWS)], b1, r1)
            ca.wait()
            a0 = pltpu.async_copy(b0, o_hbm.at[i0_v.at[j]], s0)
            a1 = pltpu.async_copy(b0, o_hbm.at[i1_v.at[j]], s0)
            cb.wait()
            e0 = pltpu.async_copy(b1, o_hbm.at[i0_v.at[j + 1]], s1)
            e1 = pltpu.async_copy(b1, o_hbm.at[i1_v.at[j + 1]], s1)
            a0.wait()
            a1.wait()
            e0.wait()
            e1.wait()

    return scatter(x, idx[0], idx[1])


def _sc_gather_rows(table, dest):
    T = dest.shape[1]
    W = table.shape[1]
    tw = T // SC_WORKERS
    nch = tw // SC_ROWS
    assert tw * SC_WORKERS == T and nch * SC_ROWS == tw
    idx = dest.reshape(MOE_TOPK, SC_WORKERS, nch, SC_ROWS)
    out = jax.ShapeDtypeStruct((T, W), U32)

    @functools.partial(pl.kernel, mesh=_sc_mesh(), out_type=[out, out], scratch_types=_sc_scratch(nch, W))
    def gather(tab_hbm, i0_hbm, i1_hbm, o0_hbm, o1_hbm, i0_v, i1_v, b0, b1, g0, g1, w0, w1):
        wid = lax.axis_index("s") * SC_CORES + lax.axis_index("c")
        base = wid * tw
        pltpu.sync_copy(i0_hbm.at[wid], i0_v)
        pltpu.sync_copy(i1_hbm.at[wid], i1_v)

        @pl.loop(0, nch)
        def _(j):
            rows = pl.ds(base + j * SC_ROWS, SC_ROWS)
            c0 = pltpu.async_copy(tab_hbm.at[i0_v.at[j]], b0, g0)
            c1 = pltpu.async_copy(tab_hbm.at[i1_v.at[j]], b1, g1)
            c0.wait()
            d0 = pltpu.async_copy(b0, o0_hbm.at[rows], w0)
            c1.wait()
            d1 = pltpu.async_copy(b1, o1_hbm.at[rows], w1)
            d0.wait()
            d1.wait()

    return gather(table, idx[0], idx[1])


def _moe(xn, rpt, w_gate_up, w_down):
    dest, block_e, block_valid, P = _moe_dispatch_plan(rpt)
    xs = _sc_scatter_rows(xn, dest, P)
    y = _moe_experts(xs, block_e, block_valid, w_gate_up, w_down)
    return _sc_gather_rows(y, dest)


def _final_body(h_ref, y0_ref, y1_ref, rp_ref, g_ref, o_ref):
    h = _combine(h_ref[...], y0_ref[...], y1_ref[...], rp_ref[...])
    o_ref[...] = h * lax.rsqrt(jnp.mean(h * h, axis=-1, keepdims=True) + NORM_EPS) * g_ref[...]


def _final(h, y0, y1, rp, gain, tm=1024):
    T, D = h.shape
    tm = min(tm, T)
    row = lambda i: (i, 0)
    return pl.pallas_call(
        _final_body,
        grid=(T // tm,),
        in_specs=[pl.BlockSpec((tm, D), row), pl.BlockSpec((tm, D // 2), row), pl.BlockSpec((tm, D // 2), row),
                  pl.BlockSpec((tm, LANES), row), pl.BlockSpec((1, D), lambda i: (0, 0))],
        out_specs=pl.BlockSpec((tm, D), row),
        out_shape=jax.ShapeDtypeStruct((T, D), F32),
        compiler_params=_cparams("parallel"),
        name="final_norm",
    )(h, y0, y1, rp, gain.reshape(1, D))


def _rope_tables(n_tok, head_dim):
    t = jnp.arange(n_tok, dtype=jnp.int32)
    row = (t // GRID_W).astype(F32)
    col = (t % GRID_W).astype(F32)
    half = head_dim // 2
    inv = ROPE_THETA ** (-jnp.arange(0, half, 2, dtype=F32) / half)
    ang_r = row[:, None] * inv[None, :]
    ang_c = col[:, None] * inv[None, :]
    ang = jnp.concatenate([ang_r, ang_r, ang_c, ang_c], axis=-1)
    q = half // 2
    sign = jnp.where((jnp.arange(head_dim) % half) < q, -1.0, 1.0).astype(F32)
    return jnp.cos(ang), jnp.sin(ang) * sign[None, :]


def _router_weights(w_group, b_group, w_expert, b_expert):
    D = w_group.shape[0]
    pad = LANES - MOE_EXPERTS - MOE_GROUPS
    w = jnp.concatenate([w_expert, w_group, jnp.zeros((D, pad), F32)], axis=1).astype(F32)
    b = jnp.concatenate([b_expert, b_group, jnp.zeros((pad,), F32)]).reshape(1, LANES)
    w_hi = w.astype(BF16)
    w_lo = (w - w_hi.astype(F32)).astype(BF16)
    return jnp.concatenate([w_hi, w_lo], axis=1), b.astype(F32)


def _trunk(x, p):
    B, S, D = x.shape
    T = B * S
    cos_r, sin_r = _rope_tables(S, RET_QK_DIM)
    cos_a, sin_a = _rope_tables(S, ATTN_HEAD_DIM)
    h = x.reshape(T, D)

    h, proj = _in_proj(h, p["ln_mix"][0], p["ret_w_in"][0])
    y = _retention(proj, p["ret_decay_logit"][0], cos_r, sin_r, B, S)
    h, xn, rp, rpt = _out_proj(y, h, p["ret_w_out"][0], p["ln_ffn"][0], *p["router"][0])
    y0, y1 = _moe(xn, rpt, p["moe_w_gate_up"][0], p["moe_w_down"][0])

    h, proj = _in_proj(h, p["ln_mix"][1], p["attn_w_in"][0], comb=(y0, y1, rp))
    kn, vx = _kv_prep(proj, p["attn_k_gain"][0], cos_a, sin_a, S)
    y = _attention(proj, p["attn_q_gain"][0], cos_a, sin_a, kn, vx, B, S)
    h, xn, rp, rpt = _out_proj(y, h, p["attn_w_out"][0], p["ln_ffn"][1], *p["router"][1])
    y0, y1 = _moe(xn, rpt, p["moe_w_gate_up"][1], p["moe_w_down"][1])

    return _final(h, y0, y1, rp, p["ln_final"]).reshape(B, S, D)


def kernel(x_prompt, x_sample, ln_mix, ret_w_in, ret_decay_logit, ret_w_out, attn_w_in, attn_q_gain, attn_k_gain,
           attn_w_out, ln_ffn, moe_w_group, moe_b_group, moe_w_expert, moe_b_expert, moe_w_gate_up, moe_w_down,
           ln_final):
    p = {
        "ln_mix": ln_mix, "ln_ffn": ln_ffn, "ln_final": ln_final,
        "ret_w_in": ret_w_in.astype(BF16), "ret_decay_logit": ret_decay_logit, "ret_w_out": ret_w_out.astype(BF16),
        "attn_w_in": attn_w_in.astype(BF16), "attn_q_gain": attn_q_gain, "attn_k_gain": attn_k_gain,
        "attn_w_out": attn_w_out.astype(BF16),
        "router": [_router_weights(moe_w_group[i], moe_b_group[i], moe_w_expert[i], moe_b_expert[i])
                   for i in range(moe_w_group.shape[0])],
        "moe_w_gate_up": moe_w_gate_up, "moe_w_down": moe_w_down,
    }
    return _trunk(x_prompt, p), _trunk(x_sample, p)
```

```python
import functools

import jax
import jax.numpy as jnp
from jax import lax
from jax.experimental import pallas as pl
from jax.experimental.pallas import tpu as pltpu
from jax.experimental.pallas import tpu_sc as plsc

F32 = jnp.float32
BF16 = jnp.bfloat16
U32 = jnp.uint32
I32 = jnp.int32

D_MODEL = 1024
GRID_W = 64
ROPE_THETA = 10000.0
NORM_EPS = 1e-6

RET_HEADS = 4
RET_QK_DIM = 256
RET_V_DIM = 512
RET_SCAN_CHUNK = 256
RET_QK_WIDTH = RET_HEADS * RET_QK_DIM
RET_V_WIDTH = RET_HEADS * RET_V_DIM

ATTN_Q_HEADS = 8
ATTN_KV_HEADS = 2
ATTN_HEAD_DIM = 128
ATTN_GROUP = ATTN_Q_HEADS // ATTN_KV_HEADS
ATTN_IN_WIDTH = (ATTN_Q_HEADS + 2 * ATTN_KV_HEADS) * ATTN_HEAD_DIM

MOE_GROUPS = 4
MOE_EXPERTS_PER_GROUP = 8
MOE_EXPERTS = MOE_GROUPS * MOE_EXPERTS_PER_GROUP
MOE_TOPK = 2
MOE_FF = 512
MOE_BLOCK = 512

LANES = 128
LOG2_E = 1.4426950408889634
VMEM_LIMIT = 56 * 1024 * 1024


def _cparams(*sem):
    return pltpu.CompilerParams(dimension_semantics=sem, vmem_limit_bytes=VMEM_LIMIT)


def _pack_bf16_pairs(x):
    w = x.shape[1] // 2
    lo = lax.bitcast_convert_type(x[:, :w].astype(BF16).astype(F32), U32)
    hi = lax.bitcast_convert_type(x[:, w:].astype(BF16).astype(F32), U32)
    return lax.shift_right_logical(lo, jnp.uint32(16)) | (hi & jnp.uint32(0xFFFF0000))


def _unpack_bf16_pairs(u):
    lo = lax.bitcast_convert_type(lax.shift_left(u, jnp.uint32(16)), F32)
    hi = lax.bitcast_convert_type(u & jnp.uint32(0xFFFF0000), F32)
    return lo, hi


def _combine(h, y0_u32, y1_u32, rp):
    lo0, hi0 = _unpack_bf16_pairs(y0_u32)
    lo1, hi1 = _unpack_bf16_pairs(y1_u32)
    w0 = rp[:, 2:3]
    w1 = rp[:, 3:4]
    half = h.shape[1] // 2
    return jnp.concatenate([h[:, :half] + w0 * lo0 + w1 * lo1, h[:, half:] + w0 * hi0 + w1 * hi1], axis=1)


def _in_proj_body(combine, *refs):
    if combine:
        h_ref, y0_ref, y1_ref, rp_ref, g_ref, w_ref, hout_ref, o_ref, xn_ref = refs
    else:
        h_ref, g_ref, w_ref, o_ref, xn_ref = refs

    @pl.when(pl.program_id(1) == 0)
    def _():
        h = h_ref[...]
        if combine:
            h = _combine(h, y0_ref[...], y1_ref[...], rp_ref[...])
            hout_ref[...] = h
        xn = h * lax.rsqrt(jnp.mean(h * h, axis=-1, keepdims=True) + NORM_EPS) * g_ref[...]
        xn_ref[...] = xn.astype(BF16)

    o_ref[...] = jnp.dot(xn_ref[...], w_ref[...], preferred_element_type=F32).astype(o_ref.dtype)


def _in_proj(h, gain, w, comb=None, tm=1024, tn=1536):
    T, D = h.shape
    N = w.shape[1]
    tm = min(tm, T)
    tn = min(tn, N)
    row = lambda i, j: (i, 0)
    in_specs = [pl.BlockSpec((tm, D), row)]
    args = [h]
    if comb is not None:
        y0, y1, rp = comb
        in_specs += [pl.BlockSpec((tm, D // 2), row), pl.BlockSpec((tm, D // 2), row),
                     pl.BlockSpec((tm, LANES), row)]
        args += [y0, y1, rp]
    in_specs += [pl.BlockSpec((1, D), lambda i, j: (0, 0)), pl.BlockSpec((D, tn), lambda i, j: (0, j))]
    args += [gain.reshape(1, D), w]
    out_shape = [jax.ShapeDtypeStruct((T, N), BF16)]
    out_specs = [pl.BlockSpec((tm, tn), lambda i, j: (i, j))]
    if comb is not None:
        out_shape.insert(0, jax.ShapeDtypeStruct((T, D), F32))
        out_specs.insert(0, pl.BlockSpec((tm, D), row))
    res = pl.pallas_call(
        functools.partial(_in_proj_body, comb is not None),
        grid=(T // tm, N // tn),
        in_specs=in_specs, out_specs=out_specs, out_shape=out_shape,
        scratch_shapes=[pltpu.VMEM((tm, D), BF16)],
        compiler_params=_cparams("parallel", "arbitrary"),
        name="in_proj",
    )(*args)
    if comb is not None:
        return res[0], res[1]
    return h, res[0]


def _rope256(x, cos, sin_signed):
    xr = jnp.concatenate([pltpu.roll(x[:, :LANES], 64, 1), pltpu.roll(x[:, LANES:], 64, 1)], axis=1)
    return x * cos + xr * sin_signed


def _retention_body(dl_ref, q_ref, k_ref, v_ref, g_ref, cos_ref, sin_ref, y_ref, state_ref, obwd_ref,
                    *, n_chunks, n_blocks):
    C = RET_SCAN_CHUNK
    h = pl.program_id(1)
    phase = pl.program_id(2)
    n = pl.program_id(3)
    sb = n_chunks * C

    def log_gamma(s):
        x = jnp.full((1, 1), s, F32)
        return jnp.minimum(x, 0.0) - jnp.log(1.0 + jnp.exp(-jnp.abs(x)))

    lg_f = log_gamma(dl_ref[h])
    lg_b = log_gamma(dl_ref[RET_HEADS + h])
    pos = lax.broadcasted_iota(jnp.int32, (C, RET_QK_DIM), 0).astype(F32)
    k_scale = RET_QK_DIM ** -0.5

    @pl.when(n == 0)
    def _():
        state_ref[...] = jnp.zeros_like(state_ref)

    def load(rows, q_dec, k_dec):
        cos = cos_ref[rows, :]
        sin = sin_ref[rows, :]
        q = _rope256(q_ref[rows, :], cos, sin)
        k = _rope256(k_ref[rows, :], cos, sin)
        return q, k, v_ref[rows, :], q * q_dec, (k * k_dec).T

    @pl.when(phase == 0)
    def _():
        q_dec = jnp.exp((C - pos) * lg_b).astype(BF16)
        k_dec = (jnp.exp(pos * lg_b) * k_scale).astype(BF16)
        chunk_dec = jnp.exp(lg_b * C)
        base = (n_blocks - 1 - n) * sb

        def body(cc, carry):
            c = n_chunks - 1 - cc
            rows = pl.ds(pl.multiple_of(c * C, C), C)
            _, _, v, qd, kt = load(rows, q_dec, k_dec)
            st = state_ref[...]
            out = jnp.dot(qd, st.astype(BF16), preferred_element_type=F32)
            state_ref[...] = st * chunk_dec + jnp.dot(kt, v, preferred_element_type=F32)
            obwd_ref[pl.ds(pl.multiple_of(base + c * C, C), C), :] = out.astype(obwd_ref.dtype)
            return carry

        lax.fori_loop(0, n_chunks, body, 0, unroll=min(2, n_chunks))

    @pl.when(phase == 1)
    def _():
        q_dec = jnp.exp((pos + 1.0) * lg_f).astype(BF16)
        k_dec = (jnp.exp((C - 1.0 - pos) * lg_f) * k_scale).astype(BF16)
        chunk_dec = jnp.exp(lg_f * C)
        ii = lax.broadcasted_iota(jnp.int32, (C, C), 0)
        jj = lax.broadcasted_iota(jnp.int32, (C, C), 1)
        dist = (ii - jj).astype(F32)
        inner = (jnp.where(dist >= 0, jnp.exp(lg_f * jnp.maximum(dist, 0.0)), 0.0)
                 + jnp.where(dist <= 0, jnp.exp(lg_b * jnp.maximum(-dist, 0.0)), 0.0)) * k_scale
        base = n * sb

        def body(c, carry):
            rows = pl.ds(pl.multiple_of(c * C, C), C)
            q, k, v, qd, kt = load(rows, q_dec, k_dec)
            scores = lax.dot_general(q, k, (((1,), (1,)), ((), ())), preferred_element_type=F32) * inner
            st = state_ref[...]
            lhs = jnp.concatenate([qd, scores.astype(BF16)], axis=1)
            rhs = jnp.concatenate([st.astype(BF16), v], axis=0)
            o = jnp.dot(lhs, rhs, preferred_element_type=F32)
            state_ref[...] = st * chunk_dec + jnp.dot(kt, v, preferred_element_type=F32)
            o = o + obwd_ref[pl.ds(pl.multiple_of(base + c * C, C), C), :].astype(F32)
            o = o * lax.rsqrt(jnp.mean(o * o, axis=-1, keepdims=True) + NORM_EPS)
            g = g_ref[rows, :].astype(F32)
            y_ref[rows, :] = (g * jax.nn.sigmoid(g) * o).astype(y_ref.dtype)
            return carry

        lax.fori_loop(0, n_chunks, body, 0, unroll=min(2, n_chunks))


def _retention(proj, decay_logit, cos, sin_signed, B, S, sb=2048):
    sb = min(sb, S)
    nb = S // sb
    proj3 = proj.reshape(B, S, proj.shape[-1])
    blk = lambda p, n: jnp.where(p == 0, nb - 1 - n, n)
    kq = RET_QK_WIDTH // RET_QK_DIM
    kv = 2 * RET_QK_WIDTH // RET_V_DIM
    kg = kv + RET_HEADS
    in_specs = [
        pl.BlockSpec(memory_space=pltpu.SMEM),
        pl.BlockSpec((None, sb, RET_QK_DIM), lambda b, h, p, n: (b, blk(p, n), h)),
        pl.BlockSpec((None, sb, RET_QK_DIM), lambda b, h, p, n: (b, blk(p, n), kq + h)),
        pl.BlockSpec((None, sb, RET_V_DIM), lambda b, h, p, n: (b, blk(p, n), kv + h)),
        pl.BlockSpec((None, sb, RET_V_DIM), lambda b, h, p, n: (b, n * p, kg + h)),
        pl.BlockSpec((sb, RET_QK_DIM), lambda b, h, p, n: (blk(p, n), 0)),
        pl.BlockSpec((sb, RET_QK_DIM), lambda b, h, p, n: (blk(p, n), 0)),
    ]
    out = pl.pallas_call(
        functools.partial(_retention_body, n_chunks=sb // RET_SCAN_CHUNK, n_blocks=nb),
        grid=(B, RET_HEADS, 2, nb),
        in_specs=in_specs,
        out_specs=pl.BlockSpec((None, sb, RET_V_DIM), lambda b, h, p, n: (b, n * p, h)),
        out_shape=jax.ShapeDtypeStruct((B, S, RET_V_WIDTH), BF16),
        scratch_shapes=[pltpu.VMEM((RET_QK_DIM, RET_V_DIM), F32), pltpu.VMEM((S, RET_V_DIM), BF16)],
        compiler_params=_cparams("parallel", "parallel", "arbitrary", "arbitrary"),
        name="retention",
    )(decay_logit.reshape(2 * RET_HEADS).astype(F32), proj3, proj3, proj3, proj3,
      cos.astype(BF16), sin_signed.astype(BF16))
    return out.reshape(B * S, RET_V_WIDTH)


ATTN_HEAD_PERM = tuple(list(range(0, 32)) + list(range(64, 96)) + list(range(32, 64)) + list(range(96, 128)))


def _row_sum128(x):
    hi = x.astype(BF16)
    lo = (x - hi.astype(F32)).astype(BF16)
    ones = jnp.ones((LANES, LANES), BF16)
    return jnp.dot(hi, ones, preferred_element_type=F32) + jnp.dot(lo, ones, preferred_element_type=F32)


def _norm_rope128(x, gain, cos, sin_signed):
    x = x.astype(F32)
    x = x * lax.rsqrt(_row_sum128(x * x) * (1.0 / ATTN_HEAD_DIM) + NORM_EPS) * gain
    return x * cos + pltpu.roll(x, 64, 1) * sin_signed


def _qkv_prep_body(p_ref, qg_ref, kg_ref, cos_ref, sin_ref, q_ref, k_ref, v_ref):
    d = ATTN_HEAD_DIM
    q_scale = (d ** -0.5) * LOG2_E
    for hq in range(ATTN_Q_HEADS):
        q = _norm_rope128(p_ref[:, hq * d:(hq + 1) * d], qg_ref[...], cos_ref[...], sin_ref[...]) * q_scale
        q_ref[:, hq * d:(hq + 1) * d] = q.astype(q_ref.dtype)
    for hk in range(ATTN_KV_HEADS):
        c0 = (ATTN_Q_HEADS + hk) * d
        k = _norm_rope128(p_ref[:, c0:c0 + d], kg_ref[...], cos_ref[...], sin_ref[...])
        k_ref[:, hk * d:(hk + 1) * d] = k.astype(k_ref.dtype)
        c1 = (ATTN_Q_HEADS + ATTN_KV_HEADS + hk) * d
        v_ref[:, 2 * hk * d:(2 * hk + 1) * d] = p_ref[:, c1:c1 + d]
        v_ref[:, (2 * hk + 1) * d:(2 * hk + 2) * d] = jnp.ones((p_ref.shape[0], d), v_ref.dtype)


def _qkv_prep(proj, q_gain, k_gain, cos, sin_signed, S, tm=1024):
    T = proj.shape[0]
    tm = min(tm, S)
    spb = S // tm
    d = ATTN_HEAD_DIM
    row = lambda i: (i, 0)
    return pl.pallas_call(
        _qkv_prep_body,
        grid=(T // tm,),
        in_specs=[pl.BlockSpec((tm, ATTN_IN_WIDTH), row),
                  pl.BlockSpec((1, d), lambda i: (0, 0)), pl.BlockSpec((1, d), lambda i: (0, 0)),
                  pl.BlockSpec((tm, d), lambda i: (i % spb, 0)), pl.BlockSpec((tm, d), lambda i: (i % spb, 0))],
        out_specs=[pl.BlockSpec((tm, ATTN_Q_HEADS * d), row), pl.BlockSpec((tm, ATTN_KV_HEADS * d), row),
                   pl.BlockSpec((tm, 2 * ATTN_KV_HEADS * d), row)],
        out_shape=[jax.ShapeDtypeStruct((T, ATTN_Q_HEADS * d), BF16),
                   jax.ShapeDtypeStruct((T, ATTN_KV_HEADS * d), BF16),
                   jax.ShapeDtypeStruct((T, 2 * ATTN_KV_HEADS * d), BF16)],
        compiler_params=_cparams("parallel"),
        name="qkv_prep",
    )(proj, q_gain.reshape(1, d), k_gain.reshape(1, d), cos, sin_signed)


def _attn_body(q_ref, k_ref, v_ref, o_ref, qs_ref, m_ref, acc_ref, *, tq, tk, n_kv):
    d = ATTN_HEAD_DIM
    for g in range(ATTN_GROUP):
        qs_ref[g * tq:(g + 1) * tq, :] = q_ref[:, g * d:(g + 1) * d]
    m_ref[...] = jnp.full_like(m_ref, -jnp.inf)
    acc_ref[...] = jnp.zeros_like(acc_ref)
    rep = tk // LANES

    def step(c, carry):
        rows = pl.ds(pl.multiple_of(c * tk, tk), tk)
        k = k_ref[rows, :]
        v = v_ref[rows, :]
        for g in range(ATTN_GROUP):
            r = slice(g * tq, (g + 1) * tq)
            s = lax.dot_general(qs_ref[r, :], k, (((1,), (1,)), ((), ())), preferred_element_type=F32)
            m_prev = m_ref[r, :]
            m_new = jnp.maximum(m_prev, jnp.max(s, axis=-1, keepdims=True))
            alpha = jnp.exp2(m_prev - m_new)
            p = jnp.exp2(s - jnp.concatenate([m_new] * rep, axis=1))
            pv = jnp.dot(p.astype(BF16), v, preferred_element_type=F32)
            acc_ref[r, :] = jnp.concatenate([alpha, alpha], axis=1) * acc_ref[r, :] + pv
            m_ref[r, :] = m_new
        return carry

    lax.fori_loop(0, n_kv, step, 0, unroll=min(8, n_kv))
    for g in range(ATTN_GROUP):
        r = slice(g * tq, (g + 1) * tq)
        o_ref[:, g * d:(g + 1) * d] = (acc_ref[r, :d] / acc_ref[r, d:]).astype(o_ref.dtype)


def _attention(qn, kn, vx, B, S, tq=512, tk=256):
    d = ATTN_HEAD_DIM
    tq = min(tq, S)
    tk = min(tk, S)
    gw = ATTN_GROUP * d
    q3 = qn.reshape(B, S, ATTN_Q_HEADS * d)
    k3 = kn.reshape(B, S, ATTN_KV_HEADS * d)
    v3 = vx.reshape(B, S, 2 * ATTN_KV_HEADS * d)
    m = ATTN_GROUP * tq
    out = pl.pallas_call(
        functools.partial(_attn_body, tq=tq, tk=tk, n_kv=S // tk),
        grid=(B, ATTN_KV_HEADS, S // tq),
        in_specs=[pl.BlockSpec((None, tq, gw), lambda b, kh, i: (b, i, kh)),
                  pl.BlockSpec((None, S, d), lambda b, kh, i: (b, 0, kh)),
                  pl.BlockSpec((None, S, 2 * d), lambda b, kh, i: (b, 0, kh))],
        out_specs=pl.BlockSpec((None, tq, gw), lambda b, kh, i: (b, i, kh)),
        out_shape=jax.ShapeDtypeStruct((B, S, ATTN_Q_HEADS * d), BF16),
        scratch_shapes=[pltpu.VMEM((m, d), BF16), pltpu.VMEM((m, LANES), F32), pltpu.VMEM((m, 2 * d), F32)],
        compiler_params=_cparams("parallel", "parallel", "arbitrary"),
        name="attention",
    )(q3, k3, v3)
    return out.reshape(B * S, ATTN_Q_HEADS * d)


ROUTER_GROUP_ROW = MOE_EXPERTS
SUBLANES = 8
assert MOE_EXPERTS_PER_GROUP == SUBLANES and MOE_GROUPS <= SUBLANES


def _route(logits_t):
    tm = logits_t.shape[1]
    row = lax.broadcasted_iota(jnp.int32, (SUBLANES, tm), 0)
    neg = jnp.float32(-jnp.inf)
    big = jnp.int32(SUBLANES)

    def first_argmax(x):
        mx = jnp.max(x, axis=0, keepdims=True)
        return mx, jnp.min(jnp.where(x == mx, row, big), axis=0, keepdims=True)

    gl = jnp.where(row < MOE_GROUPS, logits_t[ROUTER_GROUP_ROW:ROUTER_GROUP_ROW + SUBLANES, :], neg)
    gmax, g_sel = first_argmax(gl)
    g_w = 1.0 / jnp.sum(jnp.exp(gl - gmax), axis=0, keepdims=True)
    el = logits_t[0:SUBLANES, :]
    for g in range(1, MOE_GROUPS):
        el = jnp.where(g_sel == g, logits_t[g * SUBLANES:(g + 1) * SUBLANES, :], el)
    v1, i1 = first_argmax(el)
    v2, i2 = first_argmax(jnp.where(row == i1, neg, el))
    e2 = jnp.exp(v2 - v1)
    w1 = g_w / (1.0 + e2)
    w2 = g_w * e2 / (1.0 + e2)
    base = g_sel * MOE_EXPERTS_PER_GROUP
    return jnp.where(row == 0, (base + i1).astype(F32),
                     jnp.where(row == 1, (base + i2).astype(F32),
                               jnp.where(row == 2, w1, jnp.where(row == 3, w2, 0.0))))


def _out_proj_body(y_ref, h_ref, w_ref, g_ref, wr_ref, br_ref, hout_ref, xn_ref, rp_ref, rpt_ref):
    h = h_ref[...] + jnp.dot(y_ref[...], w_ref[...], preferred_element_type=F32)
    hout_ref[...] = h
    xn = h * lax.rsqrt(jnp.mean(h * h, axis=-1, keepdims=True) + NORM_EPS) * g_ref[...]
    xn_ref[...] = _pack_bf16_pairs(xn)
    xn_hi = xn.astype(BF16)
    xn_lo = (xn - xn_hi.astype(F32)).astype(BF16)
    wr = wr_ref[...]
    hi = jnp.dot(xn_hi, wr, preferred_element_type=F32)
    lo = jnp.dot(xn_lo, wr[:, :LANES], preferred_element_type=F32)
    logits = hi[:, :LANES] + hi[:, LANES:] + lo + br_ref[...]
    rpt = _route(logits.T)
    rpt_ref[...] = rpt
    full = jnp.concatenate([rpt, jnp.zeros((LANES - SUBLANES, rpt.shape[1]), F32)], axis=0)
    rp_ref[...] = full.T


def _out_proj(y, h, w, gain, w_router, b_router, tm=512):
    T, K = y.shape
    D = h.shape[1]
    tm = min(tm, T)
    row = lambda i: (i, 0)
    const = lambda i: (0, 0)
    return pl.pallas_call(
        _out_proj_body,
        grid=(T // tm,),
        in_specs=[pl.BlockSpec((tm, K), row), pl.BlockSpec((tm, D), row), pl.BlockSpec((K, D), const),
                  pl.BlockSpec((1, D), const), pl.BlockSpec((D, 2 * LANES), const), pl.BlockSpec((1, LANES), const)],
        out_specs=[pl.BlockSpec((tm, D), row), pl.BlockSpec((tm, D // 2), row), pl.BlockSpec((tm, LANES), row),
                   pl.BlockSpec((SUBLANES, tm), lambda i: (0, i))],
        out_shape=[jax.ShapeDtypeStruct((T, D), F32), jax.ShapeDtypeStruct((T, D // 2), U32),
                   jax.ShapeDtypeStruct((T, LANES), F32), jax.ShapeDtypeStruct((SUBLANES, T), F32)],
        compiler_params=_cparams("parallel"),
        name="out_proj",
    )(y, h, w, gain.reshape(1, D), w_router, b_router)


def _moe_body(be_ref, bv_ref, x_ref, wgu_ref, wd_ref, y_ref, wgu_bf_ref, wd_bf_ref):
    b = pl.program_id(0)

    @pl.when((b == 0) | (be_ref[b] != be_ref[jnp.maximum(b - 1, 0)]))
    def _():
        wgu_bf_ref[...] = wgu_ref[...].astype(BF16)
        wd_bf_ref[...] = wd_ref[...].astype(BF16)

    @pl.when(bv_ref[b] != 0)
    def _():
        lo, hi = _unpack_bf16_pairs(x_ref[...])
        x = jnp.concatenate([lo.astype(BF16), hi.astype(BF16)], axis=1)
        gu = jnp.dot(x, wgu_bf_ref[...], preferred_element_type=F32)
        gate = gu[:, :MOE_FF]
        up = gu[:, MOE_FF:]
        act = (gate * jax.nn.sigmoid(gate) * up).astype(BF16)
        y_ref[...] = _pack_bf16_pairs(jnp.dot(act, wd_bf_ref[...], preferred_element_type=F32))

    @pl.when(bv_ref[b] == 0)
    def _():
        y_ref[...] = jnp.zeros_like(y_ref)


def _moe_experts(xs, block_e, block_valid, w_gate_up, w_down, layer):
    P, W = xs.shape
    D = 2 * W
    n_blocks = P // MOE_BLOCK
    grid_spec = pltpu.PrefetchScalarGridSpec(
        num_scalar_prefetch=2,
        grid=(n_blocks,),
        in_specs=[pl.BlockSpec((MOE_BLOCK, W), lambda b, be, bv: (b, 0)),
                  pl.BlockSpec((None, None, D, 2 * MOE_FF), lambda b, be, bv: (layer, be[b], 0, 0)),
                  pl.BlockSpec((None, None, MOE_FF, D), lambda b, be, bv: (layer, be[b], 0, 0))],
        out_specs=pl.BlockSpec((MOE_BLOCK, W), lambda b, be, bv: (b, 0)),
        scratch_shapes=[pltpu.VMEM((D, 2 * MOE_FF), BF16), pltpu.VMEM((MOE_FF, D), BF16)],
    )
    return pl.pallas_call(
        _moe_body,
        grid_spec=grid_spec,
        out_shape=jax.ShapeDtypeStruct((P, W), U32),
        compiler_params=_cparams("arbitrary"),
        name="moe_experts",
    )(block_e, block_valid, xs, w_gate_up, w_down)


def _rank_body(rpt_ref, tri_ref, rank_ref, counts_ref, carry_ref, *, n_groups):
    @pl.when(pl.program_id(0) == 0)
    def _():
        carry_ref[...] = jnp.zeros_like(carry_ref)

    erow = lax.broadcasted_iota(I32, (MOE_EXPERTS, LANES), 0)
    carry = carry_ref[...]
    tri = tri_ref[...]
    r0, r1 = [], []
    for g in range(n_groups):
        c = slice(g * LANES, (g + 1) * LANES)
        oh0 = erow == rpt_ref[0:1, c].astype(I32)
        oh1 = erow == rpt_ref[1:2, c].astype(I32)
        both = jnp.where(oh0 | oh1, 1.0, 0.0)
        before = jnp.dot(both.astype(BF16), tri, preferred_element_type=F32) + carry
        r0.append(jnp.sum(jnp.where(oh0, before, 0.0), axis=0, keepdims=True))
        r1.append(jnp.sum(jnp.where(oh1, before, 0.0), axis=0, keepdims=True))
        carry = carry + jnp.sum(both, axis=1, keepdims=True)
    carry_ref[...] = carry
    counts_ref[...] = carry.astype(I32)
    tm = n_groups * LANES
    rank = jnp.concatenate([jnp.concatenate(r0, axis=1), jnp.concatenate(r1, axis=1),
                            jnp.zeros((SUBLANES - MOE_TOPK, tm), F32)], axis=0)
    rank_ref[...] = rank.astype(I32)


def _slot_ranks(rpt, tm=2048):
    T = rpt.shape[1]
    tm = min(tm, T)
    tri = (lax.broadcasted_iota(I32, (LANES, LANES), 0) < lax.broadcasted_iota(I32, (LANES, LANES), 1)).astype(BF16)
    return pl.pallas_call(
        functools.partial(_rank_body, n_groups=tm // LANES),
        grid=(T // tm,),
        in_specs=[pl.BlockSpec((SUBLANES, tm), lambda i: (0, i)), pl.BlockSpec((LANES, LANES), lambda i: (0, 0))],
        out_specs=[pl.BlockSpec((SUBLANES, tm), lambda i: (0, i)), pl.BlockSpec((MOE_EXPERTS, LANES), lambda i: (0, 0))],
        out_shape=[jax.ShapeDtypeStruct((SUBLANES, T), I32), jax.ShapeDtypeStruct((MOE_EXPERTS, LANES), I32)],
        scratch_shapes=[pltpu.VMEM((MOE_EXPERTS, LANES), F32)],
        compiler_params=_cparams("arbitrary"),
        name="slot_ranks",
    )(rpt, tri)


def _moe_dispatch_plan(rpt):
    T = rpt.shape[1]
    A = T * MOE_TOPK
    n_blocks = -(-(A + MOE_EXPERTS * (MOE_BLOCK - 1)) // MOE_BLOCK)
    rank, counts = _slot_ranks(rpt)
    counts = counts[:, 0]
    padded = ((counts + MOE_BLOCK - 1) // MOE_BLOCK) * MOE_BLOCK
    pends = jnp.cumsum(padded)
    pstarts = pends - padded
    e = rpt[:MOE_TOPK].astype(I32)
    experts = jnp.arange(MOE_EXPERTS, dtype=I32)[:, None, None]
    dest = jnp.sum(jnp.where(e[None] == experts, pstarts[:, None, None], 0), axis=0) + rank[:MOE_TOPK]
    block_start = jnp.arange(n_blocks, dtype=I32) * MOE_BLOCK
    block_e = jnp.minimum(jnp.sum(pends[None, :] <= block_start[:, None], axis=1), MOE_EXPERTS - 1).astype(I32)
    block_valid = (block_start < pends[-1]).astype(I32)
    return dest, block_e, block_valid, n_blocks * MOE_BLOCK


SC_CORES = 2
SC_SUBCORES = 16
SC_WORKERS = SC_CORES * SC_SUBCORES
SC_ROWS = 64


def _sc_mesh():
    return plsc.VectorSubcoreMesh(core_axis_name="c", subcore_axis_name="s")


def _sc_scratch(n_chunks, width):
    return [pltpu.VMEM((n_chunks, SC_ROWS), I32), pltpu.VMEM((n_chunks, SC_ROWS), I32),
            pltpu.VMEM((SC_ROWS, width), U32), pltpu.VMEM((SC_ROWS, width), U32),
            pltpu.SemaphoreType.DMA, pltpu.SemaphoreType.DMA, pltpu.SemaphoreType.DMA, pltpu.SemaphoreType.DMA]


def _sc_scatter_rows(x, dest, P):
    T, W = x.shape
    tw = T // SC_WORKERS
    nch = tw // SC_ROWS
    assert tw * SC_WORKERS == T and nch * SC_ROWS == tw and nch % 2 == 0
    idx = dest.reshape(MOE_TOPK, SC_WORKERS, nch, SC_ROWS)

    @functools.partial(pl.kernel, mesh=_sc_mesh(), out_type=jax.ShapeDtypeStruct((P, W), U32),
                       scratch_types=_sc_scratch(nch, W))
    def scatter(x_hbm, i0_hbm, i1_hbm, o_hbm, i0_v, i1_v, b0, b1, r0, r1, s0, s1):
        wid = lax.axis_index("s") * SC_CORES + lax.axis_index("c")
        base = wid * tw
        pltpu.sync_copy(i0_hbm.at[wid], i0_v)
        pltpu.sync_copy(i1_hbm.at[wid], i1_v)

        @pl.loop(0, nch, step=2)
        def _(j):
            ca = pltpu.async_copy(x_hbm.at[pl.ds(base + j * SC_ROWS, SC_ROWS)], b0, r0)
            cb = pltpu.async_copy(x_hbm.at[pl.ds(base + (j + 1) * SC_ROWS, SC_ROWS)], b1, r1)
            ca.wait()
            a0 = pltpu.async_copy(b0, o_hbm.at[i0_v.at[j]], s0)
            a1 = pltpu.async_copy(b0, o_hbm.at[i1_v.at[j]], s0)
            cb.wait()
            e0 = pltpu.async_copy(b1, o_hbm.at[i0_v.at[j + 1]], s1)
            e1 = pltpu.async_copy(b1, o_hbm.at[i1_v.at[j + 1]], s1)
            a0.wait()
            a1.wait()
            e0.wait()
            e1.wait()

    return scatter(x, idx[0], idx[1])


def _sc_gather_rows(table, dest):
    T = dest.shape[1]
    W = table.shape[1]
    tw = T // SC_WORKERS
    nch = tw // SC_ROWS
    assert tw * SC_WORKERS == T and nch * SC_ROWS == tw
    idx = dest.reshape(MOE_TOPK, SC_WORKERS, nch, SC_ROWS)
    out = jax.ShapeDtypeStruct((T, W), U32)

    @functools.partial(pl.kernel, mesh=_sc_mesh(), out_type=[out, out], scratch_types=_sc_scratch(nch, W))
    def gather(tab_hbm, i0_hbm, i1_hbm, o0_hbm, o1_hbm, i0_v, i1_v, b0, b1, g0, g1, w0, w1):
        wid = lax.axis_index("s") * SC_CORES + lax.axis_index("c")
        base = wid * tw
        pltpu.sync_copy(i0_hbm.at[wid], i0_v)
        pltpu.sync_copy(i1_hbm.at[wid], i1_v)

        @pl.loop(0, nch)
        def _(j):
            rows = pl.ds(base + j * SC_ROWS, SC_ROWS)
            c0 = pltpu.async_copy(tab_hbm.at[i0_v.at[j]], b0, g0)
            c1 = pltpu.async_copy(tab_hbm.at[i1_v.at[j]], b1, g1)
            c0.wait()
            d0 = pltpu.async_copy(b0, o0_hbm.at[rows], w0)
            c1.wait()
            d1 = pltpu.async_copy(b1, o1_hbm.at[rows], w1)
            d0.wait()
            d1.wait()

    return gather(table, idx[0], idx[1])


def _moe(xn, rpt, w_gate_up, w_down, layer):
    dest, block_e, block_valid, P = _moe_dispatch_plan(rpt)
    xs = _sc_scatter_rows(xn, dest, P)
    y = _moe_experts(xs, block_e, block_valid, w_gate_up, w_down, layer)
    return _sc_gather_rows(y, dest)


def _final_body(h_ref, y0_ref, y1_ref, rp_ref, g_ref, o_ref):
    h = _combine(h_ref[...], y0_ref[...], y1_ref[...], rp_ref[...])
    o_ref[...] = h * lax.rsqrt(jnp.mean(h * h, axis=-1, keepdims=True) + NORM_EPS) * g_ref[...]


def _final(h, y0, y1, rp, gain, tm=1024):
    T, D = h.shape
    tm = min(tm, T)
    row = lambda i: (i, 0)
    return pl.pallas_call(
        _final_body,
        grid=(T // tm,),
        in_specs=[pl.BlockSpec((tm, D), row), pl.BlockSpec((tm, D // 2), row), pl.BlockSpec((tm, D // 2), row),
                  pl.BlockSpec((tm, LANES), row), pl.BlockSpec((1, D), lambda i: (0, 0))],
        out_specs=pl.BlockSpec((tm, D), row),
        out_shape=jax.ShapeDtypeStruct((T, D), F32),
        compiler_params=_cparams("parallel"),
        name="final_norm",
    )(h, y0, y1, rp, gain.reshape(1, D))


def _rope_tables(n_tok, head_dim):
    t = jnp.arange(n_tok, dtype=jnp.int32)
    row = (t // GRID_W).astype(F32)
    col = (t % GRID_W).astype(F32)
    half = head_dim // 2
    inv = ROPE_THETA ** (-jnp.arange(0, half, 2, dtype=F32) / half)
    ang_r = row[:, None] * inv[None, :]
    ang_c = col[:, None] * inv[None, :]
    ang = jnp.concatenate([ang_r, ang_r, ang_c, ang_c], axis=-1)
    q = half // 2
    sign = jnp.where((jnp.arange(head_dim) % half) < q, -1.0, 1.0).astype(F32)
    return jnp.cos(ang), jnp.sin(ang) * sign[None, :]


def _router_weights(w_group, b_group, w_expert, b_expert):
    D = w_group.shape[0]
    pad = LANES - MOE_EXPERTS - MOE_GROUPS
    w = jnp.concatenate([w_expert, w_group, jnp.zeros((D, pad), F32)], axis=1).astype(F32)
    b = jnp.concatenate([b_expert, b_group, jnp.zeros((pad,), F32)]).reshape(1, LANES)
    w_hi = w.astype(BF16)
    w_lo = (w - w_hi.astype(F32)).astype(BF16)
    return jnp.concatenate([w_hi, w_lo], axis=1), b.astype(F32)


def _trunk(x, p):
    B, S, D = x.shape
    T = B * S
    cos_r, sin_r = _rope_tables(S, RET_QK_DIM)
    cos_a, sin_a = [jnp.take(t, jnp.asarray(ATTN_HEAD_PERM, I32), axis=1) for t in _rope_tables(S, ATTN_HEAD_DIM)]
    h = x.reshape(T, D)

    h, proj = _in_proj(h, p["ln_mix"][0], p["ret_w_in"][0])
    y = _retention(proj, p["ret_decay_logit"][0], cos_r, sin_r, B, S)
    h, xn, rp, rpt = _out_proj(y, h, p["ret_w_out"][0], p["ln_ffn"][0], *p["router"][0])
    y0, y1 = _moe(xn, rpt, p["moe_w_gate_up"], p["moe_w_down"], 0)

    h, proj = _in_proj(h, p["ln_mix"][1], p["attn_w_in"][0], comb=(y0, y1, rp))
    qn, kn, vx = _qkv_prep(proj, p["attn_q_gain"][0], p["attn_k_gain"][0], cos_a, sin_a, S)
    y = _attention(qn, kn, vx, B, S)
    h, xn, rp, rpt = _out_proj(y, h, p["attn_w_out"][0], p["ln_ffn"][1], *p["router"][1])
    y0, y1 = _moe(xn, rpt, p["moe_w_gate_up"], p["moe_w_down"], 1)

    return _final(h, y0, y1, rp, p["ln_final"]).reshape(B, S, D)


def _prepare_params(ln_mix, ret_w_in, ret_decay_logit, ret_w_out, attn_w_in, attn_q_gain, attn_k_gain,
                    attn_w_out, ln_ffn, moe_w_group, moe_b_group, moe_w_expert, moe_b_expert, moe_w_gate_up,
                    moe_w_down, ln_final):
    d = ATTN_HEAD_DIM
    perm = jnp.asarray(ATTN_HEAD_PERM, I32)
    qk_cols = (jnp.arange((ATTN_Q_HEADS + ATTN_KV_HEADS) * d, dtype=I32) // d) * d
    qk_cols = qk_cols + jnp.tile(perm, ATTN_Q_HEADS + ATTN_KV_HEADS)
    cols = jnp.concatenate([qk_cols, jnp.arange(qk_cols.shape[0], ATTN_IN_WIDTH, dtype=I32)])
    return {
        "ln_mix": ln_mix, "ln_ffn": ln_ffn, "ln_final": ln_final,
        "ret_w_in": ret_w_in.astype(BF16), "ret_decay_logit": ret_decay_logit, "ret_w_out": ret_w_out.astype(BF16),
        "attn_w_in": jnp.take(attn_w_in, cols, axis=2).astype(BF16),
        "attn_q_gain": jnp.take(attn_q_gain, perm, axis=1), "attn_k_gain": jnp.take(attn_k_gain, perm, axis=1),
        "attn_w_out": attn_w_out.astype(BF16),
        "router": [_router_weights(moe_w_group[i], moe_b_group[i], moe_w_expert[i], moe_b_expert[i])
                   for i in range(moe_w_group.shape[0])],
        "moe_w_gate_up": moe_w_gate_up, "moe_w_down": moe_w_down,
    }


def kernel(x_prompt, x_sample, ln_mix, ret_w_in, ret_decay_logit, ret_w_out, attn_w_in, attn_q_gain, attn_k_gain,
           attn_w_out, ln_ffn, moe_w_group, moe_b_group, moe_w_expert, moe_b_expert, moe_w_gate_up, moe_w_down,
           ln_final):
    p = _prepare_params(ln_mix, ret_w_in, ret_decay_logit, ret_w_out, attn_w_in, attn_q_gain, attn_k_gain,
                        attn_w_out, ln_ffn, moe_w_group, moe_b_group, moe_w_expert, moe_b_expert, moe_w_gate_up,
                        moe_w_down, ln_final)
    return _trunk(x_prompt, p), _trunk(x_sample, p)
```

```python
import functools

import jax
import jax.numpy as jnp
from jax import lax
from jax.experimental import pallas as pl
from jax.experimental.pallas import tpu as pltpu
from jax.experimental.pallas import tpu_sc as plsc

F32 = jnp.float32
BF16 = jnp.bfloat16
U32 = jnp.uint32
I32 = jnp.int32

D_MODEL = 1024
GRID_W = 64
ROPE_THETA = 10000.0
NORM_EPS = 1e-6

RET_HEADS = 4
RET_QK_DIM = 256
RET_V_DIM = 512
RET_SCAN_CHUNK = 256
RET_QK_WIDTH = RET_HEADS * RET_QK_DIM
RET_V_WIDTH = RET_HEADS * RET_V_DIM

ATTN_Q_HEADS = 8
ATTN_KV_HEADS = 2
ATTN_HEAD_DIM = 128
ATTN_GROUP = ATTN_Q_HEADS // ATTN_KV_HEADS
ATTN_IN_WIDTH = (ATTN_Q_HEADS + 2 * ATTN_KV_HEADS) * ATTN_HEAD_DIM

MOE_GROUPS = 4
MOE_EXPERTS_PER_GROUP = 8
MOE_EXPERTS = MOE_GROUPS * MOE_EXPERTS_PER_GROUP
MOE_TOPK = 2
MOE_FF = 512
MOE_BLOCK = 512

LANES = 128
LOG2_E = 1.4426950408889634
VMEM_LIMIT = 56 * 1024 * 1024


def _cparams(*sem):
    return pltpu.CompilerParams(dimension_semantics=sem, vmem_limit_bytes=VMEM_LIMIT)


def _pack_bf16_pairs(x):
    w = x.shape[1] // 2
    lo = lax.bitcast_convert_type(x[:, :w].astype(BF16).astype(F32), U32)
    hi = lax.bitcast_convert_type(x[:, w:].astype(BF16).astype(F32), U32)
    return lax.shift_right_logical(lo, jnp.uint32(16)) | (hi & jnp.uint32(0xFFFF0000))


def _unpack_bf16_pairs(u):
    lo = lax.bitcast_convert_type(lax.shift_left(u, jnp.uint32(16)), F32)
    hi = lax.bitcast_convert_type(u & jnp.uint32(0xFFFF0000), F32)
    return lo, hi


def _combine(h, y0_u32, y1_u32, rp):
    lo0, hi0 = _unpack_bf16_pairs(y0_u32)
    lo1, hi1 = _unpack_bf16_pairs(y1_u32)
    w0 = rp[:, 2:3]
    w1 = rp[:, 3:4]
    half = h.shape[1] // 2
    return jnp.concatenate([h[:, :half] + w0 * lo0 + w1 * lo1, h[:, half:] + w0 * hi0 + w1 * hi1], axis=1)


def _in_proj_body(combine, single_col_tile, *refs):
    if combine:
        h_ref, y0_ref, y1_ref, rp_ref, g_ref, w_ref, hout_ref, o_ref, xn_ref = refs
    else:
        h_ref, g_ref, w_ref, o_ref, xn_ref = refs

    def normed(rows):
        h = h_ref[rows, :]
        if combine:
            h = _combine(h, y0_ref[rows, :], y1_ref[rows, :], rp_ref[rows, :])
            hout_ref[rows, :] = h
        xn = h * lax.rsqrt(jnp.mean(h * h, axis=-1, keepdims=True) + NORM_EPS) * g_ref[...]
        return xn.astype(BF16)

    if single_col_tile:
        tm = h_ref.shape[0]
        sub = min(IN_PROJ_SUB_ROWS, tm)
        for s in range(tm // sub):
            rows = slice(s * sub, (s + 1) * sub)
            o_ref[rows, :] = jnp.dot(normed(rows), w_ref[...], preferred_element_type=F32).astype(o_ref.dtype)
        return

    @pl.when(pl.program_id(1) == 0)
    def _():
        xn_ref[...] = normed(slice(None))

    o_ref[...] = jnp.dot(xn_ref[...], w_ref[...], preferred_element_type=F32).astype(o_ref.dtype)


IN_PROJ_SUB_ROWS = 512


def _in_proj(h, gain, w, comb=None, tm=1024, tn=2048):
    T, D = h.shape
    N = w.shape[1]
    tm = min(tm, T)
    tn = min(tn, N)
    row = lambda i, j: (i, 0)
    in_specs = [pl.BlockSpec((tm, D), row)]
    args = [h]
    if comb is not None:
        y0, y1, rp = comb
        in_specs += [pl.BlockSpec((tm, D // 2), row), pl.BlockSpec((tm, D // 2), row),
                     pl.BlockSpec((tm, LANES), row)]
        args += [y0, y1, rp]
    in_specs += [pl.BlockSpec((1, D), lambda i, j: (0, 0)), pl.BlockSpec((D, tn), lambda i, j: (0, j))]
    args += [gain.reshape(1, D), w]
    out_shape = [jax.ShapeDtypeStruct((T, N), BF16)]
    out_specs = [pl.BlockSpec((tm, tn), lambda i, j: (i, j))]
    if comb is not None:
        out_shape.insert(0, jax.ShapeDtypeStruct((T, D), F32))
        out_specs.insert(0, pl.BlockSpec((tm, D), row))
    res = pl.pallas_call(
        functools.partial(_in_proj_body, comb is not None, N == tn),
        grid=(T // tm, N // tn),
        in_specs=in_specs, out_specs=out_specs, out_shape=out_shape,
        scratch_shapes=[pltpu.VMEM((tm, D), BF16)],
        compiler_params=_cparams("parallel", "arbitrary"),
        name="in_proj",
    )(*args)
    if comb is not None:
        return res[0], res[1]
    return h, res[0]


def _rope256(x, cos, sin_signed):
    xr = jnp.concatenate([pltpu.roll(x[:, :LANES], 64, 1), pltpu.roll(x[:, LANES:], 64, 1)], axis=1)
    return x * cos + xr * sin_signed


def _retention_body(dl_ref, q_ref, k_ref, v_ref, g_ref, cos_ref, sin_ref, y_ref, state_ref, obwd_ref,
                    *, n_chunks, n_blocks):
    C = RET_SCAN_CHUNK
    h = pl.program_id(1)
    phase = pl.program_id(2)
    n = pl.program_id(3)
    sb = n_chunks * C

    def log_gamma(s):
        x = jnp.full((1, 1), s, F32)
        return jnp.minimum(x, 0.0) - jnp.log(1.0 + jnp.exp(-jnp.abs(x)))

    lg_f = log_gamma(dl_ref[h])
    lg_b = log_gamma(dl_ref[RET_HEADS + h])
    pos = lax.broadcasted_iota(jnp.int32, (C, RET_QK_DIM), 0).astype(F32)
    k_scale = RET_QK_DIM ** -0.5

    @pl.when(n == 0)
    def _():
        state_ref[...] = jnp.zeros_like(state_ref)

    def load(rows, q_dec, k_dec):
        cos = cos_ref[rows, :]
        sin = sin_ref[rows, :]
        q = _rope256(q_ref[rows, :], cos, sin)
        k = _rope256(k_ref[rows, :], cos, sin)
        return q, k, v_ref[rows, :], q * q_dec, (k * k_dec).T

    @pl.when(phase == 0)
    def _():
        q_dec = jnp.exp((C - pos) * lg_b).astype(BF16)
        k_dec = (jnp.exp(pos * lg_b) * k_scale).astype(BF16)
        chunk_dec = jnp.exp(lg_b * C)
        base = (n_blocks - 1 - n) * sb

        def body(cc, carry):
            c = n_chunks - 1 - cc
            rows = pl.ds(pl.multiple_of(c * C, C), C)
            _, _, v, qd, kt = load(rows, q_dec, k_dec)
            st = state_ref[...]
            out = jnp.dot(qd, st.astype(BF16), preferred_element_type=F32)
            state_ref[...] = st * chunk_dec + jnp.dot(kt, v, preferred_element_type=F32)
            obwd_ref[pl.ds(pl.multiple_of(base + c * C, C), C), :] = out.astype(obwd_ref.dtype)
            return carry

        lax.fori_loop(0, n_chunks, body, 0, unroll=min(4, n_chunks))

    @pl.when(phase == 1)
    def _():
        q_dec = jnp.exp((pos + 1.0) * lg_f).astype(BF16)
        k_dec = (jnp.exp((C - 1.0 - pos) * lg_f) * k_scale).astype(BF16)
        chunk_dec = jnp.exp(lg_f * C)
        ii = lax.broadcasted_iota(jnp.int32, (C, C), 0)
        jj = lax.broadcasted_iota(jnp.int32, (C, C), 1)
        dist = (ii - jj).astype(F32)
        inner = (jnp.where(dist >= 0, jnp.exp(lg_f * jnp.maximum(dist, 0.0)), 0.0)
                 + jnp.where(dist <= 0, jnp.exp(lg_b * jnp.maximum(-dist, 0.0)), 0.0)) * k_scale
        base = n * sb

        def body(c, carry):
            rows = pl.ds(pl.multiple_of(c * C, C), C)
            q, k, v, qd, kt = load(rows, q_dec, k_dec)
            scores = lax.dot_general(q, k, (((1,), (1,)), ((), ())), preferred_element_type=F32) * inner
            st = state_ref[...]
            lhs = jnp.concatenate([qd, scores.astype(BF16)], axis=1)
            rhs = jnp.concatenate([st.astype(BF16), v], axis=0)
            o = jnp.dot(lhs, rhs, preferred_element_type=F32)
            state_ref[...] = st * chunk_dec + jnp.dot(kt, v, preferred_element_type=F32)
            o = o + obwd_ref[pl.ds(pl.multiple_of(base + c * C, C), C), :].astype(F32)
            o = o * lax.rsqrt(jnp.mean(o * o, axis=-1, keepdims=True) + NORM_EPS)
            g = g_ref[rows, :]
            y_ref[rows, :] = g * jax.nn.sigmoid(g) * o.astype(BF16)
            return carry

        lax.fori_loop(0, n_chunks, body, 0, unroll=min(4, n_chunks))


def _retention(proj, decay_logit, cos, sin_signed, B, S, sb=2048):
    sb = min(sb, S)
    nb = S // sb
    proj3 = proj.reshape(B, S, proj.shape[-1])
    blk = lambda p, n: jnp.where(p == 0, nb - 1 - n, n)
    kq = RET_QK_WIDTH // RET_QK_DIM
    kv = 2 * RET_QK_WIDTH // RET_V_DIM
    kg = kv + RET_HEADS
    in_specs = [
        pl.BlockSpec(memory_space=pltpu.SMEM),
        pl.BlockSpec((None, sb, RET_QK_DIM), lambda b, h, p, n: (b, blk(p, n), h)),
        pl.BlockSpec((None, sb, RET_QK_DIM), lambda b, h, p, n: (b, blk(p, n), kq + h)),
        pl.BlockSpec((None, sb, RET_V_DIM), lambda b, h, p, n: (b, blk(p, n), kv + h)),
        pl.BlockSpec((None, sb, RET_V_DIM), lambda b, h, p, n: (b, n * p, kg + h)),
        pl.BlockSpec((sb, RET_QK_DIM), lambda b, h, p, n: (blk(p, n), 0)),
        pl.BlockSpec((sb, RET_QK_DIM), lambda b, h, p, n: (blk(p, n), 0)),
    ]
    out = pl.pallas_call(
        functools.partial(_retention_body, n_chunks=sb // RET_SCAN_CHUNK, n_blocks=nb),
        grid=(B, RET_HEADS, 2, nb),
        in_specs=in_specs,
        out_specs=pl.BlockSpec((None, sb, RET_V_DIM), lambda b, h, p, n: (b, n * p, h)),
        out_shape=jax.ShapeDtypeStruct((B, S, RET_V_WIDTH), BF16),
        scratch_shapes=[pltpu.VMEM((RET_QK_DIM, RET_V_DIM), F32), pltpu.VMEM((S, RET_V_DIM), BF16)],
        compiler_params=_cparams("parallel", "parallel", "arbitrary", "arbitrary"),
        name="retention",
    )(decay_logit.reshape(2 * RET_HEADS).astype(F32), proj3, proj3, proj3, proj3,
      cos.astype(BF16), sin_signed.astype(BF16))
    return out.reshape(B * S, RET_V_WIDTH)


ATTN_HEAD_PERM = tuple(list(range(0, 32)) + list(range(64, 96)) + list(range(32, 64)) + list(range(96, 128)))


def _row_sum128(x):
    hi = x.astype(BF16)
    lo = (x - hi.astype(F32)).astype(BF16)
    ones = jnp.ones((LANES, LANES), BF16)
    return jnp.dot(hi, ones, preferred_element_type=F32) + jnp.dot(lo, ones, preferred_element_type=F32)


def _norm_rope128(x, gain, cos, sin_signed):
    x = x.astype(F32)
    x = x * lax.rsqrt(_row_sum128(x * x) * (1.0 / ATTN_HEAD_DIM) + NORM_EPS) * gain
    return x * cos + pltpu.roll(x, 64, 1) * sin_signed


def _qkv_prep_body(p_ref, qg_ref, kg_ref, cos_ref, sin_ref, q_ref, k_ref, v_ref):
    d = ATTN_HEAD_DIM
    q_scale = (d ** -0.5) * LOG2_E
    for hq in range(ATTN_Q_HEADS):
        q = _norm_rope128(p_ref[:, hq * d:(hq + 1) * d], qg_ref[...], cos_ref[...], sin_ref[...]) * q_scale
        q_ref[:, hq * d:(hq + 1) * d] = q.astype(q_ref.dtype)
    for hk in range(ATTN_KV_HEADS):
        c0 = (ATTN_Q_HEADS + hk) * d
        k = _norm_rope128(p_ref[:, c0:c0 + d], kg_ref[...], cos_ref[...], sin_ref[...])
        k_ref[:, hk * d:(hk + 1) * d] = k.astype(k_ref.dtype)
        c1 = (ATTN_Q_HEADS + ATTN_KV_HEADS + hk) * d
        v_ref[:, 2 * hk * d:(2 * hk + 1) * d] = p_ref[:, c1:c1 + d]
        v_ref[:, (2 * hk + 1) * d:(2 * hk + 2) * d] = jnp.ones((p_ref.shape[0], d), v_ref.dtype)


def _qkv_prep(proj, q_gain, k_gain, cos, sin_signed, S, tm=1024):
    T = proj.shape[0]
    tm = min(tm, S)
    spb = S // tm
    d = ATTN_HEAD_DIM
    row = lambda i: (i, 0)
    return pl.pallas_call(
        _qkv_prep_body,
        grid=(T // tm,),
        in_specs=[pl.BlockSpec((tm, ATTN_IN_WIDTH), row),
                  pl.BlockSpec((1, d), lambda i: (0, 0)), pl.BlockSpec((1, d), lambda i: (0, 0)),
                  pl.BlockSpec((tm, d), lambda i: (i % spb, 0)), pl.BlockSpec((tm, d), lambda i: (i % spb, 0))],
        out_specs=[pl.BlockSpec((tm, ATTN_Q_HEADS * d), row), pl.BlockSpec((tm, ATTN_KV_HEADS * d), row),
                   pl.BlockSpec((tm, 2 * ATTN_KV_HEADS * d), row)],
        out_shape=[jax.ShapeDtypeStruct((T, ATTN_Q_HEADS * d), BF16),
                   jax.ShapeDtypeStruct((T, ATTN_KV_HEADS * d), BF16),
                   jax.ShapeDtypeStruct((T, 2 * ATTN_KV_HEADS * d), BF16)],
        compiler_params=_cparams("parallel"),
        name="qkv_prep",
    )(proj, q_gain.reshape(1, d), k_gain.reshape(1, d), cos, sin_signed)


def _attn_body(q_ref, k_ref, v_ref, o_ref, qs_ref, m_ref, acc_ref, *, tq, tk, n_kv):
    d = ATTN_HEAD_DIM
    for g in range(ATTN_GROUP):
        qs_ref[g * tq:(g + 1) * tq, :] = q_ref[:, g * d:(g + 1) * d]
    m_ref[...] = jnp.full_like(m_ref, -jnp.inf)
    acc_ref[...] = jnp.zeros_like(acc_ref)
    rep = tk // LANES

    def step(c, carry):
        rows = pl.ds(pl.multiple_of(c * tk, tk), tk)
        k = k_ref[rows, :]
        v = v_ref[rows, :]
        for g in range(ATTN_GROUP):
            r = slice(g * tq, (g + 1) * tq)
            s = lax.dot_general(qs_ref[r, :], k, (((1,), (1,)), ((), ())), preferred_element_type=F32)
            m_prev = m_ref[r, :]
            m_new = jnp.maximum(m_prev, jnp.max(s, axis=-1, keepdims=True))
            alpha = jnp.exp2(m_prev - m_new)
            p = jnp.exp2(s - jnp.concatenate([m_new] * rep, axis=1))
            pv = jnp.dot(p.astype(BF16), v, preferred_element_type=F32)
            acc_ref[r, :] = jnp.concatenate([alpha, alpha], axis=1) * acc_ref[r, :] + pv
            m_ref[r, :] = m_new
        return carry

    lax.fori_loop(0, n_kv, step, 0, unroll=min(8, n_kv))
    for g in range(ATTN_GROUP):
        r = slice(g * tq, (g + 1) * tq)
        o_ref[:, g * d:(g + 1) * d] = (acc_ref[r, :d] / acc_ref[r, d:]).astype(o_ref.dtype)


def _attention(qn, kn, vx, B, S, tq=512, tk=256):
    d = ATTN_HEAD_DIM
    tq = min(tq, S)
    tk = min(tk, S)
    gw = ATTN_GROUP * d
    q3 = qn.reshape(B, S, ATTN_Q_HEADS * d)
    k3 = kn.reshape(B, S, ATTN_KV_HEADS * d)
    v3 = vx.reshape(B, S, 2 * ATTN_KV_HEADS * d)
    m = ATTN_GROUP * tq
    out = pl.pallas_call(
        functools.partial(_attn_body, tq=tq, tk=tk, n_kv=S // tk),
        grid=(B, ATTN_KV_HEADS, S // tq),
        in_specs=[pl.BlockSpec((None, tq, gw), lambda b, kh, i: (b, i, kh)),
                  pl.BlockSpec((None, S, d), lambda b, kh, i: (b, 0, kh)),
                  pl.BlockSpec((None, S, 2 * d), lambda b, kh, i: (b, 0, kh))],
        out_specs=pl.BlockSpec((None, tq, gw), lambda b, kh, i: (b, i, kh)),
        out_shape=jax.ShapeDtypeStruct((B, S, ATTN_Q_HEADS * d), BF16),
        scratch_shapes=[pltpu.VMEM((m, d), BF16), pltpu.VMEM((m, LANES), F32), pltpu.VMEM((m, 2 * d), F32)],
        compiler_params=_cparams("parallel", "parallel", "arbitrary"),
        name="attention",
    )(q3, k3, v3)
    return out.reshape(B * S, ATTN_Q_HEADS * d)


ROUTER_GROUP_ROW = MOE_EXPERTS
SUBLANES = 8
assert MOE_EXPERTS_PER_GROUP == SUBLANES and MOE_GROUPS <= SUBLANES


def _route(logits_t):
    tm = logits_t.shape[1]
    row = lax.broadcasted_iota(jnp.int32, (SUBLANES, tm), 0)
    neg = jnp.float32(-jnp.inf)
    big = jnp.int32(SUBLANES)

    def first_argmax(x):
        mx = jnp.max(x, axis=0, keepdims=True)
        return mx, jnp.min(jnp.where(x == mx, row, big), axis=0, keepdims=True)

    gl = jnp.where(row < MOE_GROUPS, logits_t[ROUTER_GROUP_ROW:ROUTER_GROUP_ROW + SUBLANES, :], neg)
    gmax, g_sel = first_argmax(gl)
    g_w = 1.0 / jnp.sum(jnp.exp(gl - gmax), axis=0, keepdims=True)
    el = logits_t[0:SUBLANES, :]
    for g in range(1, MOE_GROUPS):
        el = jnp.where(g_sel == g, logits_t[g * SUBLANES:(g + 1) * SUBLANES, :], el)
    v1, i1 = first_argmax(el)
    v2, i2 = first_argmax(jnp.where(row == i1, neg, el))
    e2 = jnp.exp(v2 - v1)
    w1 = g_w / (1.0 + e2)
    w2 = g_w * e2 / (1.0 + e2)
    base = g_sel * MOE_EXPERTS_PER_GROUP
    return jnp.where(row == 0, (base + i1).astype(F32),
                     jnp.where(row == 1, (base + i2).astype(F32),
                               jnp.where(row == 2, w1, jnp.where(row == 3, w2, 0.0))))


def _out_proj_body(y_ref, h_ref, w_ref, g_ref, wr_ref, br_ref, hout_ref, xn_ref, rp_ref, rpt_ref):
    tm = h_ref.shape[0]
    sub = min(OUT_PROJ_SUB_ROWS, tm)
    wr = wr_ref[...]
    for s in range(tm // sub):
        rows = slice(s * sub, (s + 1) * sub)
        h = h_ref[rows, :] + jnp.dot(y_ref[rows, :], w_ref[...], preferred_element_type=F32)
        hout_ref[rows, :] = h
        xn = h * lax.rsqrt(jnp.mean(h * h, axis=-1, keepdims=True) + NORM_EPS) * g_ref[...]
        xn_ref[rows, :] = _pack_bf16_pairs(xn)
        xn_hi = xn.astype(BF16)
        xn_lo = (xn - xn_hi.astype(F32)).astype(BF16)
        hi = jnp.dot(xn_hi, wr, preferred_element_type=F32)
        lo = jnp.dot(xn_lo, wr[:, :LANES], preferred_element_type=F32)
        logits = hi[:, :LANES] + hi[:, LANES:] + lo + br_ref[...]
        rpt = _route(logits.T)
        rpt_ref[:, rows] = rpt
        full = jnp.concatenate([rpt, jnp.zeros((LANES - SUBLANES, sub), F32)], axis=0)
        rp_ref[rows, :] = full.T


OUT_PROJ_SUB_ROWS = 512


def _out_proj(y, h, w, gain, w_router, b_router, tm=1024):
    T, K = y.shape
    D = h.shape[1]
    tm = min(tm, T)
    row = lambda i: (i, 0)
    const = lambda i: (0, 0)
    return pl.pallas_call(
        _out_proj_body,
        grid=(T // tm,),
        in_specs=[pl.BlockSpec((tm, K), row), pl.BlockSpec((tm, D), row), pl.BlockSpec((K, D), const),
                  pl.BlockSpec((1, D), const), pl.BlockSpec((D, 2 * LANES), const), pl.BlockSpec((1, LANES), const)],
        out_specs=[pl.BlockSpec((tm, D), row), pl.BlockSpec((tm, D // 2), row), pl.BlockSpec((tm, LANES), row),
                   pl.BlockSpec((SUBLANES, tm), lambda i: (0, i))],
        out_shape=[jax.ShapeDtypeStruct((T, D), F32), jax.ShapeDtypeStruct((T, D // 2), U32),
                   jax.ShapeDtypeStruct((T, LANES), F32), jax.ShapeDtypeStruct((SUBLANES, T), F32)],
        compiler_params=_cparams("parallel"),
        name="out_proj",
    )(y, h, w, gain.reshape(1, D), w_router, b_router)


def _moe_body(be_ref, bv_ref, x_ref, wgu_ref, wd_ref, y_ref, wgu_bf_ref, wd_bf_ref):
    b = pl.program_id(0)

    @pl.when((b == 0) | (be_ref[b] != be_ref[jnp.maximum(b - 1, 0)]))
    def _():
        wgu_bf_ref[...] = wgu_ref[...].astype(BF16)
        wd_bf_ref[...] = wd_ref[...].astype(BF16)

    @pl.when(bv_ref[b] != 0)
    def _():
        lo, hi = _unpack_bf16_pairs(x_ref[...])
        x = jnp.concatenate([lo.astype(BF16), hi.astype(BF16)], axis=1)
        gu = jnp.dot(x, wgu_bf_ref[...], preferred_element_type=F32)
        gate = gu[:, :MOE_FF]
        up = gu[:, MOE_FF:]
        act = (gate * jax.nn.sigmoid(gate) * up).astype(BF16)
        y_ref[...] = _pack_bf16_pairs(jnp.dot(act, wd_bf_ref[...], preferred_element_type=F32))

    @pl.when(bv_ref[b] == 0)
    def _():
        y_ref[...] = jnp.zeros_like(y_ref)


def _moe_experts(xs, block_e, block_valid, w_gate_up, w_down, layer):
    P, W = xs.shape
    D = 2 * W
    n_blocks = P // MOE_BLOCK
    grid_spec = pltpu.PrefetchScalarGridSpec(
        num_scalar_prefetch=2,
        grid=(n_blocks,),
        in_specs=[pl.BlockSpec((MOE_BLOCK, W), lambda b, be, bv: (b, 0)),
                  pl.BlockSpec((None, None, D, 2 * MOE_FF), lambda b, be, bv: (layer, be[b], 0, 0)),
                  pl.BlockSpec((None, None, MOE_FF, D), lambda b, be, bv: (layer, be[b], 0, 0))],
        out_specs=pl.BlockSpec((MOE_BLOCK, W), lambda b, be, bv: (b, 0)),
        scratch_shapes=[pltpu.VMEM((D, 2 * MOE_FF), BF16), pltpu.VMEM((MOE_FF, D), BF16)],
    )
    return pl.pallas_call(
        _moe_body,
        grid_spec=grid_spec,
        out_shape=jax.ShapeDtypeStruct((P, W), U32),
        compiler_params=_cparams("arbitrary"),
        name="moe_experts",
    )(block_e, block_valid, xs, w_gate_up, w_down)


def _rank_body(rpt_ref, tri_ref, rank_ref, counts_ref, carry_ref, *, n_groups):
    @pl.when(pl.program_id(0) == 0)
    def _():
        carry_ref[...] = jnp.zeros_like(carry_ref)

    erow = lax.broadcasted_iota(I32, (MOE_EXPERTS, LANES), 0)
    carry = carry_ref[...]
    tri = tri_ref[...]
    r0, r1 = [], []
    for g in range(n_groups):
        c = slice(g * LANES, (g + 1) * LANES)
        oh0 = erow == rpt_ref[0:1, c].astype(I32)
        oh1 = erow == rpt_ref[1:2, c].astype(I32)
        both = jnp.where(oh0 | oh1, 1.0, 0.0)
        before = jnp.dot(both.astype(BF16), tri, preferred_element_type=F32) + carry
        r0.append(jnp.sum(jnp.where(oh0, before, 0.0), axis=0, keepdims=True))
        r1.append(jnp.sum(jnp.where(oh1, before, 0.0), axis=0, keepdims=True))
        carry = carry + jnp.sum(both, axis=1, keepdims=True)
    carry_ref[...] = carry
    counts_ref[...] = carry.astype(I32)
    tm = n_groups * LANES
    rank = jnp.concatenate([jnp.concatenate(r0, axis=1), jnp.concatenate(r1, axis=1),
                            jnp.zeros((SUBLANES - MOE_TOPK, tm), F32)], axis=0)
    rank_ref[...] = rank.astype(I32)


def _slot_ranks(rpt, tm=2048):
    T = rpt.shape[1]
    tm = min(tm, T)
    tri = (lax.broadcasted_iota(I32, (LANES, LANES), 0) < lax.broadcasted_iota(I32, (LANES, LANES), 1)).astype(BF16)
    return pl.pallas_call(
        functools.partial(_rank_body, n_groups=tm // LANES),
        grid=(T // tm,),
        in_specs=[pl.BlockSpec((SUBLANES, tm), lambda i: (0, i)), pl.BlockSpec((LANES, LANES), lambda i: (0, 0))],
        out_specs=[pl.BlockSpec((SUBLANES, tm), lambda i: (0, i)), pl.BlockSpec((MOE_EXPERTS, LANES), lambda i: (0, 0))],
        out_shape=[jax.ShapeDtypeStruct((SUBLANES, T), I32), jax.ShapeDtypeStruct((MOE_EXPERTS, LANES), I32)],
        scratch_shapes=[pltpu.VMEM((MOE_EXPERTS, LANES), F32)],
        compiler_params=_cparams("arbitrary"),
        name="slot_ranks",
    )(rpt, tri)


def _moe_dispatch_plan(rpt):
    T = rpt.shape[1]
    A = T * MOE_TOPK
    n_blocks = -(-(A + MOE_EXPERTS * (MOE_BLOCK - 1)) // MOE_BLOCK)
    rank, counts = _slot_ranks(rpt)
    counts = counts[:, 0]
    padded = ((counts + MOE_BLOCK - 1) // MOE_BLOCK) * MOE_BLOCK
    pends = jnp.cumsum(padded)
    pstarts = pends - padded
    e = rpt[:MOE_TOPK].astype(I32)
    experts = jnp.arange(MOE_EXPERTS, dtype=I32)[:, None, None]
    dest = jnp.sum(jnp.where(e[None] == experts, pstarts[:, None, None], 0), axis=0) + rank[:MOE_TOPK]
    block_start = jnp.arange(n_blocks, dtype=I32) * MOE_BLOCK
    block_e = jnp.minimum(jnp.sum(pends[None, :] <= block_start[:, None], axis=1), MOE_EXPERTS - 1).astype(I32)
    block_valid = (block_start < pends[-1]).astype(I32)
    return dest, block_e, block_valid, n_blocks * MOE_BLOCK


SC_CORES = 2
SC_SUBCORES = 16
SC_WORKERS = SC_CORES * SC_SUBCORES
SC_ROWS = 64


def _sc_mesh():
    return plsc.VectorSubcoreMesh(core_axis_name="c", subcore_axis_name="s")


def _sc_scratch(n_chunks, width):
    return [pltpu.VMEM((n_chunks, SC_ROWS), I32), pltpu.VMEM((n_chunks, SC_ROWS), I32),
            pltpu.VMEM((SC_ROWS, width), U32), pltpu.VMEM((SC_ROWS, width), U32),
            pltpu.SemaphoreType.DMA, pltpu.SemaphoreType.DMA, pltpu.SemaphoreType.DMA, pltpu.SemaphoreType.DMA]


def _sc_scatter_rows(x, dest, P):
    T, W = x.shape
    tw = T // SC_WORKERS
    nch = tw // SC_ROWS
    assert tw * SC_WORKERS == T and nch * SC_ROWS == tw and nch % 2 == 0
    idx = dest.reshape(MOE_TOPK, SC_WORKERS, nch, SC_ROWS)

    @functools.partial(pl.kernel, mesh=_sc_mesh(), out_type=jax.ShapeDtypeStruct((P, W), U32),
                       scratch_types=_sc_scratch(nch, W))
    def scatter(x_hbm, i0_hbm, i1_hbm, o_hbm, i0_v, i1_v, b0, b1, r0, r1, s0, s1):
        wid = lax.axis_index("s") * SC_CORES + lax.axis_index("c")
        base = wid * tw
        pltpu.sync_copy(i0_hbm.at[wid], i0_v)
        pltpu.sync_copy(i1_hbm.at[wid], i1_v)

        @pl.loop(0, nch, step=2)
        def _(j):
            ca = pltpu.async_copy(x_hbm.at[pl.ds(base + j * SC_ROWS, SC_ROWS)], b0, r0)
            cb = pltpu.async_copy(x_hbm.at[pl.ds(base + (j + 1) * SC_ROWS, SC_ROWS)], b1, r1)
            ca.wait()
            a0 = pltpu.async_copy(b0, o_hbm.at[i0_v.at[j]], s0)
            a1 = pltpu.async_copy(b0, o_hbm.at[i1_v.at[j]], s0)
            cb.wait()
            e0 = pltpu.async_copy(b1, o_hbm.at[i0_v.at[j + 1]], s1)
            e1 = pltpu.async_copy(b1, o_hbm.at[i1_v.at[j + 1]], s1)
            a0.wait()
            a1.wait()
            e0.wait()
            e1.wait()

    return scatter(x, idx[0], idx[1])


def _sc_gather_rows(table, dest):
    T = dest.shape[1]
    W = table.shape[1]
    tw = T // SC_WORKERS
    nch = tw // SC_ROWS
    assert tw * SC_WORKERS == T and nch * SC_ROWS == tw
    idx = dest.reshape(MOE_TOPK, SC_WORKERS, nch, SC_ROWS)
    out = jax.ShapeDtypeStruct((T, W), U32)

    @functools.partial(pl.kernel, mesh=_sc_mesh(), out_type=[out, out], scratch_types=_sc_scratch(nch, W))
    def gather(tab_hbm, i0_hbm, i1_hbm, o0_hbm, o1_hbm, i0_v, i1_v, b0, b1, g0, g1, w0, w1):
        wid = lax.axis_index("s") * SC_CORES + lax.axis_index("c")
        base = wid * tw
        pltpu.sync_copy(i0_hbm.at[wid], i0_v)
        pltpu.sync_copy(i1_hbm.at[wid], i1_v)

        @pl.loop(0, nch)
        def _(j):
            rows = pl.ds(base + j * SC_ROWS, SC_ROWS)
            c0 = pltpu.async_copy(tab_hbm.at[i0_v.at[j]], b0, g0)
            c1 = pltpu.async_copy(tab_hbm.at[i1_v.at[j]], b1, g1)
            c0.wait()
            d0 = pltpu.async_copy(b0, o0_hbm.at[rows], w0)
            c1.wait()
            d1 = pltpu.async_copy(b1, o1_hbm.at[rows], w1)
            d0.wait()
            d1.wait()

    return gather(table, idx[0], idx[1])


def _moe(xn, rpt, w_gate_up, w_down, layer):
    dest, block_e, block_valid, P = _moe_dispatch_plan(rpt)
    xs = _sc_scatter_rows(xn, dest, P)
    y = _moe_experts(xs, block_e, block_valid, w_gate_up, w_down, layer)
    return _sc_gather_rows(y, dest)


def _final_body(h_ref, y0_ref, y1_ref, rp_ref, g_ref, o_ref):
    h = _combine(h_ref[...], y0_ref[...], y1_ref[...], rp_ref[...])
    o_ref[...] = h * lax.rsqrt(jnp.mean(h * h, axis=-1, keepdims=True) + NORM_EPS) * g_ref[...]


def _final(h, y0, y1, rp, gain, tm=1024):
    T, D = h.shape
    tm = min(tm, T)
    row = lambda i: (i, 0)
    return pl.pallas_call(
        _final_body,
        grid=(T // tm,),
        in_specs=[pl.BlockSpec((tm, D), row), pl.BlockSpec((tm, D // 2), row), pl.BlockSpec((tm, D // 2), row),
                  pl.BlockSpec((tm, LANES), row), pl.BlockSpec((1, D), lambda i: (0, 0))],
        out_specs=pl.BlockSpec((tm, D), row),
        out_shape=jax.ShapeDtypeStruct((T, D), F32),
        compiler_params=_cparams("parallel"),
        name="final_norm",
    )(h, y0, y1, rp, gain.reshape(1, D))


def _rope_tables(n_tok, head_dim):
    t = jnp.arange(n_tok, dtype=jnp.int32)
    row = (t // GRID_W).astype(F32)
    col = (t % GRID_W).astype(F32)
    half = head_dim // 2
    inv = ROPE_THETA ** (-jnp.arange(0, half, 2, dtype=F32) / half)
    ang_r = row[:, None] * inv[None, :]
    ang_c = col[:, None] * inv[None, :]
    ang = jnp.concatenate([ang_r, ang_r, ang_c, ang_c], axis=-1)
    q = half // 2
    sign = jnp.where((jnp.arange(head_dim) % half) < q, -1.0, 1.0).astype(F32)
    return jnp.cos(ang), jnp.sin(ang) * sign[None, :]


def _router_weights(w_group, b_group, w_expert, b_expert):
    D = w_group.shape[0]
    pad = LANES - MOE_EXPERTS - MOE_GROUPS
    w = jnp.concatenate([w_expert, w_group, jnp.zeros((D, pad), F32)], axis=1).astype(F32)
    b = jnp.concatenate([b_expert, b_group, jnp.zeros((pad,), F32)]).reshape(1, LANES)
    w_hi = w.astype(BF16)
    w_lo = (w - w_hi.astype(F32)).astype(BF16)
    return jnp.concatenate([w_hi, w_lo], axis=1), b.astype(F32)


def _trunk(x, p):
    B, S, D = x.shape
    T = B * S
    cos_r, sin_r = _rope_tables(S, RET_QK_DIM)
    cos_a, sin_a = [jnp.take(t, jnp.asarray(ATTN_HEAD_PERM, I32), axis=1) for t in _rope_tables(S, ATTN_HEAD_DIM)]
    h = x.reshape(T, D)

    h, proj = _in_proj(h, p["ln_mix"][0], p["ret_w_in"][0])
    y = _retention(proj, p["ret_decay_logit"][0], cos_r, sin_r, B, S)
    h, xn, rp, rpt = _out_proj(y, h, p["ret_w_out"][0], p["ln_ffn"][0], *p["router"][0])
    y0, y1 = _moe(xn, rpt, p["moe_w_gate_up"], p["moe_w_down"], 0)

    h, proj = _in_proj(h, p["ln_mix"][1], p["attn_w_in"][0], comb=(y0, y1, rp))
    qn, kn, vx = _qkv_prep(proj, p["attn_q_gain"][0], p["attn_k_gain"][0], cos_a, sin_a, S)
    y = _attention(qn, kn, vx, B, S)
    h, xn, rp, rpt = _out_proj(y, h, p["attn_w_out"][0], p["ln_ffn"][1], *p["router"][1])
    y0, y1 = _moe(xn, rpt, p["moe_w_gate_up"], p["moe_w_down"], 1)

    return _final(h, y0, y1, rp, p["ln_final"]).reshape(B, S, D)


def _prepare_params(ln_mix, ret_w_in, ret_decay_logit, ret_w_out, attn_w_in, attn_q_gain, attn_k_gain,
                    attn_w_out, ln_ffn, moe_w_group, moe_b_group, moe_w_expert, moe_b_expert, moe_w_gate_up,
                    moe_w_down, ln_final):
    d = ATTN_HEAD_DIM
    perm = jnp.asarray(ATTN_HEAD_PERM, I32)
    qk_cols = (jnp.arange((ATTN_Q_HEADS + ATTN_KV_HEADS) * d, dtype=I32) // d) * d
    qk_cols = qk_cols + jnp.tile(perm, ATTN_Q_HEADS + ATTN_KV_HEADS)
    cols = jnp.concatenate([qk_cols, jnp.arange(qk_cols.shape[0], ATTN_IN_WIDTH, dtype=I32)])
    return {
        "ln_mix": ln_mix, "ln_ffn": ln_ffn, "ln_final": ln_final,
        "ret_w_in": ret_w_in.astype(BF16), "ret_decay_logit": ret_decay_logit, "ret_w_out": ret_w_out.astype(BF16),
        "attn_w_in": jnp.take(attn_w_in, cols, axis=2).astype(BF16),
        "attn_q_gain": jnp.take(attn_q_gain, perm, axis=1), "attn_k_gain": jnp.take(attn_k_gain, perm, axis=1),
        "attn_w_out": attn_w_out.astype(BF16),
        "router": [_router_weights(moe_w_group[i], moe_b_group[i], moe_w_expert[i], moe_b_expert[i])
                   for i in range(moe_w_group.shape[0])],
        "moe_w_gate_up": moe_w_gate_up, "moe_w_down": moe_w_down,
    }


def kernel(x_prompt, x_sample, ln_mix, ret_w_in, ret_decay_logit, ret_w_out, attn_w_in, attn_q_gain, attn_k_gain,
           attn_w_out, ln_ffn, moe_w_group, moe_b_group, moe_w_expert, moe_b_expert, moe_w_gate_up, moe_w_down,
           ln_final):
    p = _prepare_params(ln_mix, ret_w_in, ret_decay_logit, ret_w_out, attn_w_in, attn_q_gain, attn_k_gain,
                        attn_w_out, ln_ffn, moe_w_group, moe_b_group, moe_w_expert, moe_b_expert, moe_w_gate_up,
                        moe_w_down, ln_final)
    return _trunk(x_prompt, p), _trunk(x_sample, p)
```

```python
import functools

import jax
import jax.numpy as jnp
from jax import lax
from jax.experimental import pallas as pl
from jax.experimental.pallas import tpu as pltpu
from jax.experimental.pallas import tpu_sc as plsc

F32 = jnp.float32
BF16 = jnp.bfloat16
U32 = jnp.uint32
I32 = jnp.int32

D_MODEL = 1024
GRID_W = 64
ROPE_THETA = 10000.0
NORM_EPS = 1e-6

RET_HEADS = 4
RET_QK_DIM = 256
RET_V_DIM = 512
RET_SCAN_CHUNK = 256
RET_QK_WIDTH = RET_HEADS * RET_QK_DIM
RET_V_WIDTH = RET_HEADS * RET_V_DIM

ATTN_Q_HEADS = 8
ATTN_KV_HEADS = 2
ATTN_HEAD_DIM = 128
ATTN_GROUP = ATTN_Q_HEADS // ATTN_KV_HEADS
ATTN_IN_WIDTH = (ATTN_Q_HEADS + 2 * ATTN_KV_HEADS) * ATTN_HEAD_DIM

MOE_GROUPS = 4
MOE_EXPERTS_PER_GROUP = 8
MOE_EXPERTS = MOE_GROUPS * MOE_EXPERTS_PER_GROUP
MOE_TOPK = 2
MOE_FF = 512
MOE_BLOCK = 512

LANES = 128
LOG2_E = 1.4426950408889634
VMEM_LIMIT = 56 * 1024 * 1024


def _cparams(*sem):
    return pltpu.CompilerParams(dimension_semantics=sem, vmem_limit_bytes=VMEM_LIMIT)


def _pack_bf16_pairs(x):
    w = x.shape[1] // 2
    lo = lax.bitcast_convert_type(x[:, :w].astype(BF16).astype(F32), U32)
    hi = lax.bitcast_convert_type(x[:, w:].astype(BF16).astype(F32), U32)
    return lax.shift_right_logical(lo, jnp.uint32(16)) | (hi & jnp.uint32(0xFFFF0000))


def _unpack_bf16_pairs(u):
    lo = lax.bitcast_convert_type(lax.shift_left(u, jnp.uint32(16)), F32)
    hi = lax.bitcast_convert_type(u & jnp.uint32(0xFFFF0000), F32)
    return lo, hi


def _combine(h, y0_u32, y1_u32, rp):
    lo0, hi0 = _unpack_bf16_pairs(y0_u32)
    lo1, hi1 = _unpack_bf16_pairs(y1_u32)
    w0 = rp[:, 2:3]
    w1 = rp[:, 3:4]
    half = h.shape[1] // 2
    return jnp.concatenate([h[:, :half] + w0 * lo0 + w1 * lo1, h[:, half:] + w0 * hi0 + w1 * hi1], axis=1)


def _in_proj_body(h_ref, g_ref, w_ref, o_ref, xn_ref):
    @pl.when(pl.program_id(1) == 0)
    def _():
        h = h_ref[...]
        xn = h * lax.rsqrt(jnp.mean(h * h, axis=-1, keepdims=True) + NORM_EPS) * g_ref[...]
        xn_ref[...] = xn.astype(BF16)

    o_ref[...] = jnp.dot(xn_ref[...], w_ref[...], preferred_element_type=F32).astype(o_ref.dtype)


IN_PROJ_SUB_ROWS = 512


def _in_proj(h, gain, w, tm=1024, tn=2048):
    T, D = h.shape
    N = w.shape[1]
    tm = min(tm, T)
    tn = min(tn, N)
    return pl.pallas_call(
        _in_proj_body,
        grid=(T // tm, N // tn),
        in_specs=[pl.BlockSpec((tm, D), lambda i, j: (i, 0)), pl.BlockSpec((1, D), lambda i, j: (0, 0)),
                  pl.BlockSpec((D, tn), lambda i, j: (0, j))],
        out_specs=pl.BlockSpec((tm, tn), lambda i, j: (i, j)),
        out_shape=jax.ShapeDtypeStruct((T, N), BF16),
        scratch_shapes=[pltpu.VMEM((tm, D), BF16)],
        compiler_params=_cparams("parallel", "arbitrary"),
        name="in_proj",
    )(h, gain.reshape(1, D), w)


def _rope256(x, cos, sin_signed):
    xr = jnp.concatenate([pltpu.roll(x[:, :LANES], 64, 1), pltpu.roll(x[:, LANES:], 64, 1)], axis=1)
    return x * cos + xr * sin_signed


def _retention_body(dl_ref, q_ref, k_ref, v_ref, g_ref, cos_ref, sin_ref, y_ref, state_ref, obwd_ref,
                    *, n_chunks, n_blocks):
    C = RET_SCAN_CHUNK
    h = pl.program_id(1)
    phase = pl.program_id(2)
    n = pl.program_id(3)
    sb = n_chunks * C

    def log_gamma(s):
        x = jnp.full((1, 1), s, F32)
        return jnp.minimum(x, 0.0) - jnp.log(1.0 + jnp.exp(-jnp.abs(x)))

    lg_f = log_gamma(dl_ref[h])
    lg_b = log_gamma(dl_ref[RET_HEADS + h])
    pos = lax.broadcasted_iota(jnp.int32, (C, RET_QK_DIM), 0).astype(F32)
    k_scale = RET_QK_DIM ** -0.5

    @pl.when(n == 0)
    def _():
        state_ref[...] = jnp.zeros_like(state_ref)

    def load(rows, q_dec, k_dec):
        cos = cos_ref[rows, :]
        sin = sin_ref[rows, :]
        q = _rope256(q_ref[rows, :], cos, sin)
        k = _rope256(k_ref[rows, :], cos, sin)
        return q, k, v_ref[rows, :], q * q_dec, (k * k_dec).T

    @pl.when(phase == 0)
    def _():
        q_dec = jnp.exp((C - pos) * lg_b).astype(BF16)
        k_dec = (jnp.exp(pos * lg_b) * k_scale).astype(BF16)
        chunk_dec = jnp.exp(lg_b * C)
        base = (n_blocks - 1 - n) * sb

        def body(cc, carry):
            c = n_chunks - 1 - cc
            rows = pl.ds(pl.multiple_of(c * C, C), C)
            _, _, v, qd, kt = load(rows, q_dec, k_dec)
            st = state_ref[...]
            out = jnp.dot(qd, st.astype(BF16), preferred_element_type=F32)
            state_ref[...] = st * chunk_dec + jnp.dot(kt, v, preferred_element_type=F32)
            obwd_ref[pl.ds(pl.multiple_of(base + c * C, C), C), :] = out.astype(obwd_ref.dtype)
            return carry

        lax.fori_loop(0, n_chunks, body, 0, unroll=min(4, n_chunks))

    @pl.when(phase == 1)
    def _():
        q_dec = jnp.exp((pos + 1.0) * lg_f).astype(BF16)
        k_dec = (jnp.exp((C - 1.0 - pos) * lg_f) * k_scale).astype(BF16)
        chunk_dec = jnp.exp(lg_f * C)
        ii = lax.broadcasted_iota(jnp.int32, (C, C), 0)
        jj = lax.broadcasted_iota(jnp.int32, (C, C), 1)
        dist = (ii - jj).astype(F32)
        inner = (jnp.where(dist >= 0, jnp.exp(lg_f * jnp.maximum(dist, 0.0)), 0.0)
                 + jnp.where(dist <= 0, jnp.exp(lg_b * jnp.maximum(-dist, 0.0)), 0.0)) * k_scale
        base = n * sb

        def body(c, carry):
            rows = pl.ds(pl.multiple_of(c * C, C), C)
            q, k, v, qd, kt = load(rows, q_dec, k_dec)
            scores = lax.dot_general(q, k, (((1,), (1,)), ((), ())), preferred_element_type=F32) * inner
            st = state_ref[...]
            lhs = jnp.concatenate([qd, scores.astype(BF16)], axis=1)
            rhs = jnp.concatenate([st.astype(BF16), v], axis=0)
            o = jnp.dot(lhs, rhs, preferred_element_type=F32)
            state_ref[...] = st * chunk_dec + jnp.dot(kt, v, preferred_element_type=F32)
            o = o + obwd_ref[pl.ds(pl.multiple_of(base + c * C, C), C), :].astype(F32)
            o = o * lax.rsqrt(jnp.mean(o * o, axis=-1, keepdims=True) + NORM_EPS)
            g = g_ref[rows, :]
            y_ref[rows, :] = g * jax.nn.sigmoid(g) * o.astype(BF16)
            return carry

        lax.fori_loop(0, n_chunks, body, 0, unroll=min(4, n_chunks))


def _retention(proj, decay_logit, cos, sin_signed, B, S, sb=2048):
    sb = min(sb, S)
    nb = S // sb
    proj3 = proj.reshape(B, S, proj.shape[-1])
    blk = lambda p, n: jnp.where(p == 0, nb - 1 - n, n)
    kq = RET_QK_WIDTH // RET_QK_DIM
    kv = 2 * RET_QK_WIDTH // RET_V_DIM
    kg = kv + RET_HEADS
    in_specs = [
        pl.BlockSpec(memory_space=pltpu.SMEM),
        pl.BlockSpec((None, sb, RET_QK_DIM), lambda b, h, p, n: (b, blk(p, n), h)),
        pl.BlockSpec((None, sb, RET_QK_DIM), lambda b, h, p, n: (b, blk(p, n), kq + h)),
        pl.BlockSpec((None, sb, RET_V_DIM), lambda b, h, p, n: (b, blk(p, n), kv + h)),
        pl.BlockSpec((None, sb, RET_V_DIM), lambda b, h, p, n: (b, n * p, kg + h)),
        pl.BlockSpec((sb, RET_QK_DIM), lambda b, h, p, n: (blk(p, n), 0)),
        pl.BlockSpec((sb, RET_QK_DIM), lambda b, h, p, n: (blk(p, n), 0)),
    ]
    out = pl.pallas_call(
        functools.partial(_retention_body, n_chunks=sb // RET_SCAN_CHUNK, n_blocks=nb),
        grid=(B, RET_HEADS, 2, nb),
        in_specs=in_specs,
        out_specs=pl.BlockSpec((None, sb, RET_V_DIM), lambda b, h, p, n: (b, n * p, h)),
        out_shape=jax.ShapeDtypeStruct((B, S, RET_V_WIDTH), BF16),
        scratch_shapes=[pltpu.VMEM((RET_QK_DIM, RET_V_DIM), F32), pltpu.VMEM((S, RET_V_DIM), BF16)],
        compiler_params=_cparams("parallel", "parallel", "arbitrary", "arbitrary"),
        name="retention",
    )(decay_logit.reshape(2 * RET_HEADS).astype(F32), proj3, proj3, proj3, proj3,
      cos.astype(BF16), sin_signed.astype(BF16))
    return out.reshape(B * S, RET_V_WIDTH)


ATTN_HEAD_PERM = tuple(list(range(0, 32)) + list(range(64, 96)) + list(range(32, 64)) + list(range(96, 128)))


def _row_sum128(x):
    hi = x.astype(BF16)
    lo = (x - hi.astype(F32)).astype(BF16)
    ones = jnp.ones((LANES, LANES), BF16)
    return jnp.dot(hi, ones, preferred_element_type=F32) + jnp.dot(lo, ones, preferred_element_type=F32)


def _norm_rope128(x, gain, cos, sin_signed):
    x = x.astype(F32)
    x = x * lax.rsqrt(_row_sum128(x * x) * (1.0 / ATTN_HEAD_DIM) + NORM_EPS) * gain
    return x * cos + pltpu.roll(x, 64, 1) * sin_signed


def _attn_in_proj_body(h_ref, y0_ref, y1_ref, rp_ref, g_ref, w_ref, qg_ref, kg_ref, cos_ref, sin_ref,
                       hout_ref, q_ref, k_ref, v_ref):
    d = ATTN_HEAD_DIM
    tm = h_ref.shape[0]
    sub = min(IN_PROJ_SUB_ROWS, tm)
    q_scale = (d ** -0.5) * LOG2_E
    for s in range(tm // sub):
        rows = slice(s * sub, (s + 1) * sub)
        h = _combine(h_ref[rows, :], y0_ref[rows, :], y1_ref[rows, :], rp_ref[rows, :])
        hout_ref[rows, :] = h
        xn = (h * lax.rsqrt(jnp.mean(h * h, axis=-1, keepdims=True) + NORM_EPS) * g_ref[...]).astype(BF16)
        acc = jnp.dot(xn, w_ref[...], preferred_element_type=F32)
        cos = cos_ref[rows, :]
        sin = sin_ref[rows, :]
        for hq in range(ATTN_Q_HEADS):
            q = _norm_rope128(acc[:, hq * d:(hq + 1) * d], qg_ref[...], cos, sin) * q_scale
            q_ref[rows, hq * d:(hq + 1) * d] = q.astype(q_ref.dtype)
        for hk in range(ATTN_KV_HEADS):
            c0 = (ATTN_Q_HEADS + hk) * d
            k = _norm_rope128(acc[:, c0:c0 + d], kg_ref[...], cos, sin)
            k_ref[rows, hk * d:(hk + 1) * d] = k.astype(k_ref.dtype)
            c1 = (ATTN_Q_HEADS + ATTN_KV_HEADS + hk) * d
            v_ref[rows, 2 * hk * d:(2 * hk + 1) * d] = acc[:, c1:c1 + d].astype(v_ref.dtype)
            v_ref[rows, (2 * hk + 1) * d:(2 * hk + 2) * d] = jnp.ones((sub, d), v_ref.dtype)


def _attn_in_proj(h, gain, w, comb, q_gain, k_gain, cos, sin_signed, S, tm=1024):
    T, D = h.shape
    N = w.shape[1]
    tm = min(tm, S)
    spb = S // tm
    d = ATTN_HEAD_DIM
    y0, y1, rp = comb
    row = lambda i: (i, 0)
    const = lambda i: (0, 0)
    tab = lambda i: (i % spb, 0)
    return pl.pallas_call(
        _attn_in_proj_body,
        grid=(T // tm,),
        in_specs=[pl.BlockSpec((tm, D), row), pl.BlockSpec((tm, D // 2), row), pl.BlockSpec((tm, D // 2), row),
                  pl.BlockSpec((tm, LANES), row), pl.BlockSpec((1, D), const), pl.BlockSpec((D, N), const),
                  pl.BlockSpec((1, d), const), pl.BlockSpec((1, d), const),
                  pl.BlockSpec((tm, d), tab), pl.BlockSpec((tm, d), tab)],
        out_specs=[pl.BlockSpec((tm, D), row), pl.BlockSpec((tm, ATTN_Q_HEADS * d), row),
                   pl.BlockSpec((tm, ATTN_KV_HEADS * d), row), pl.BlockSpec((tm, 2 * ATTN_KV_HEADS * d), row)],
        out_shape=[jax.ShapeDtypeStruct((T, D), F32), jax.ShapeDtypeStruct((T, ATTN_Q_HEADS * d), BF16),
                   jax.ShapeDtypeStruct((T, ATTN_KV_HEADS * d), BF16),
                   jax.ShapeDtypeStruct((T, 2 * ATTN_KV_HEADS * d), BF16)],
        compiler_params=_cparams("parallel"),
        name="attn_in_proj",
    )(h, y0, y1, rp, gain.reshape(1, D), w, q_gain.reshape(1, d), k_gain.reshape(1, d), cos, sin_signed)


def _attn_body(q_ref, k_ref, v_ref, o_ref, qs_ref, m_ref, acc_ref, *, tq, tk, n_kv):
    d = ATTN_HEAD_DIM
    for g in range(ATTN_GROUP):
        qs_ref[g * tq:(g + 1) * tq, :] = q_ref[:, g * d:(g + 1) * d]
    m_ref[...] = jnp.full_like(m_ref, -jnp.inf)
    acc_ref[...] = jnp.zeros_like(acc_ref)
    rep = tk // LANES

    def step(c, carry):
        rows = pl.ds(pl.multiple_of(c * tk, tk), tk)
        k = k_ref[rows, :]
        v = v_ref[rows, :]
        for g in range(ATTN_GROUP):
            r = slice(g * tq, (g + 1) * tq)
            s = lax.dot_general(qs_ref[r, :], k, (((1,), (1,)), ((), ())), preferred_element_type=F32)
            m_prev = m_ref[r, :]
            m_new = jnp.maximum(m_prev, jnp.max(s, axis=-1, keepdims=True))
            alpha = jnp.exp2(m_prev - m_new)
            p = jnp.exp2(s - jnp.concatenate([m_new] * rep, axis=1))
            pv = jnp.dot(p.astype(BF16), v, preferred_element_type=F32)
            acc_ref[r, :] = jnp.concatenate([alpha, alpha], axis=1) * acc_ref[r, :] + pv
            m_ref[r, :] = m_new
        return carry

    lax.fori_loop(0, n_kv, step, 0, unroll=min(8, n_kv))
    for g in range(ATTN_GROUP):
        r = slice(g * tq, (g + 1) * tq)
        o_ref[:, g * d:(g + 1) * d] = (acc_ref[r, :d] / acc_ref[r, d:]).astype(o_ref.dtype)


def _attention(qn, kn, vx, B, S, tq=512, tk=256):
    d = ATTN_HEAD_DIM
    tq = min(tq, S)
    tk = min(tk, S)
    gw = ATTN_GROUP * d
    q3 = qn.reshape(B, S, ATTN_Q_HEADS * d)
    k3 = kn.reshape(B, S, ATTN_KV_HEADS * d)
    v3 = vx.reshape(B, S, 2 * ATTN_KV_HEADS * d)
    m = ATTN_GROUP * tq
    out = pl.pallas_call(
        functools.partial(_attn_body, tq=tq, tk=tk, n_kv=S // tk),
        grid=(B, ATTN_KV_HEADS, S // tq),
        in_specs=[pl.BlockSpec((None, tq, gw), lambda b, kh, i: (b, i, kh)),
                  pl.BlockSpec((None, S, d), lambda b, kh, i: (b, 0, kh)),
                  pl.BlockSpec((None, S, 2 * d), lambda b, kh, i: (b, 0, kh))],
        out_specs=pl.BlockSpec((None, tq, gw), lambda b, kh, i: (b, i, kh)),
        out_shape=jax.ShapeDtypeStruct((B, S, ATTN_Q_HEADS * d), BF16),
        scratch_shapes=[pltpu.VMEM((m, d), BF16), pltpu.VMEM((m, LANES), F32), pltpu.VMEM((m, 2 * d), F32)],
        compiler_params=_cparams("parallel", "parallel", "arbitrary"),
        name="attention",
    )(q3, k3, v3)
    return out.reshape(B * S, ATTN_Q_HEADS * d)


ROUTER_GROUP_ROW = MOE_EXPERTS
SUBLANES = 8
assert MOE_EXPERTS_PER_GROUP == SUBLANES and MOE_GROUPS <= SUBLANES


def _route(logits_t):
    tm = logits_t.shape[1]
    row = lax.broadcasted_iota(jnp.int32, (SUBLANES, tm), 0)
    neg = jnp.float32(-jnp.inf)
    big = jnp.int32(SUBLANES)

    def first_argmax(x):
        mx = jnp.max(x, axis=0, keepdims=True)
        return mx, jnp.min(jnp.where(x == mx, row, big), axis=0, keepdims=True)

    gl = jnp.where(row < MOE_GROUPS, logits_t[ROUTER_GROUP_ROW:ROUTER_GROUP_ROW + SUBLANES, :], neg)
    gmax, g_sel = first_argmax(gl)
    g_w = 1.0 / jnp.sum(jnp.exp(gl - gmax), axis=0, keepdims=True)
    el = logits_t[0:SUBLANES, :]
    for g in range(1, MOE_GROUPS):
        el = jnp.where(g_sel == g, logits_t[g * SUBLANES:(g + 1) * SUBLANES, :], el)
    v1, i1 = first_argmax(el)
    v2, i2 = first_argmax(jnp.where(row == i1, neg, el))
    e2 = jnp.exp(v2 - v1)
    w1 = g_w / (1.0 + e2)
    w2 = g_w * e2 / (1.0 + e2)
    base = g_sel * MOE_EXPERTS_PER_GROUP
    return jnp.where(row == 0, (base + i1).astype(F32),
                     jnp.where(row == 1, (base + i2).astype(F32),
                               jnp.where(row == 2, w1, jnp.where(row == 3, w2, 0.0))))


def _out_proj_body(y_ref, h_ref, w_ref, g_ref, wr_ref, br_ref, hout_ref, xn_ref, rp_ref, rpt_ref):
    tm = h_ref.shape[0]
    sub = min(OUT_PROJ_SUB_ROWS, tm)
    wr = wr_ref[...]
    for s in range(tm // sub):
        rows = slice(s * sub, (s + 1) * sub)
        h = h_ref[rows, :] + jnp.dot(y_ref[rows, :], w_ref[...], preferred_element_type=F32)
        hout_ref[rows, :] = h
        xn = h * lax.rsqrt(jnp.mean(h * h, axis=-1, keepdims=True) + NORM_EPS) * g_ref[...]
        xn_ref[rows, :] = _pack_bf16_pairs(xn)
        xn_hi = xn.astype(BF16)
        xn_lo = (xn - xn_hi.astype(F32)).astype(BF16)
        hi = jnp.dot(xn_hi, wr, preferred_element_type=F32)
        lo = jnp.dot(xn_lo, wr[:, :LANES], preferred_element_type=F32)
        logits = hi[:, :LANES] + hi[:, LANES:] + lo + br_ref[...]
        rpt = _route(logits.T)
        rpt_ref[:, rows] = rpt
        full = jnp.concatenate([rpt, jnp.zeros((LANES - SUBLANES, sub), F32)], axis=0)
        rp_ref[rows, :] = full.T


OUT_PROJ_SUB_ROWS = 512


def _out_proj(y, h, w, gain, w_router, b_router, tm=1024):
    T, K = y.shape
    D = h.shape[1]
    tm = min(tm, T)
    row = lambda i: (i, 0)
    const = lambda i: (0, 0)
    return pl.pallas_call(
        _out_proj_body,
        grid=(T // tm,),
        in_specs=[pl.BlockSpec((tm, K), row), pl.BlockSpec((tm, D), row), pl.BlockSpec((K, D), const),
                  pl.BlockSpec((1, D), const), pl.BlockSpec((D, 2 * LANES), const), pl.BlockSpec((1, LANES), const)],
        out_specs=[pl.BlockSpec((tm, D), row), pl.BlockSpec((tm, D // 2), row), pl.BlockSpec((tm, LANES), row),
                   pl.BlockSpec((SUBLANES, tm), lambda i: (0, i))],
        out_shape=[jax.ShapeDtypeStruct((T, D), F32), jax.ShapeDtypeStruct((T, D // 2), U32),
                   jax.ShapeDtypeStruct((T, LANES), F32), jax.ShapeDtypeStruct((SUBLANES, T), F32)],
        compiler_params=_cparams("parallel"),
        name="out_proj",
    )(y, h, w, gain.reshape(1, D), w_router, b_router)


MOE_STEP_BLOCKS = 2


def _moe_body(be_ref, bv_ref, x_ref, *refs):
    nb = MOE_STEP_BLOCKS
    wgu_refs, wd_refs = refs[:nb], refs[nb:2 * nb]
    y_ref = refs[2 * nb]
    wgu_bf_refs, wd_bf_refs = refs[2 * nb + 1:3 * nb + 1], refs[3 * nb + 1:]
    s = pl.program_id(0)

    for j in range(nb):
        b = s * nb + j

        @pl.when((s == 0) | (be_ref[b] != be_ref[jnp.maximum(b - nb, 0)]))
        def _():
            wgu_bf_refs[j][...] = wgu_refs[j][...].astype(BF16)
            wd_bf_refs[j][...] = wd_refs[j][...].astype(BF16)

    @pl.when(bv_ref[s * nb] != 0)
    def _():
        for j in range(nb):
            rows = slice(j * MOE_BLOCK, (j + 1) * MOE_BLOCK)
            lo, hi = _unpack_bf16_pairs(x_ref[rows, :])
            x = jnp.concatenate([lo.astype(BF16), hi.astype(BF16)], axis=1)
            gu = jnp.dot(x, wgu_bf_refs[j][...], preferred_element_type=F32)
            gate = gu[:, :MOE_FF]
            up = gu[:, MOE_FF:]
            act = (gate * jax.nn.sigmoid(gate) * up).astype(BF16)
            y_ref[rows, :] = _pack_bf16_pairs(jnp.dot(act, wd_bf_refs[j][...], preferred_element_type=F32))

    @pl.when(bv_ref[s * nb] == 0)
    def _():
        y_ref[...] = jnp.zeros_like(y_ref)


def _moe_experts(xs, block_e, block_valid, w_gate_up, w_down, layer):
    P, W = xs.shape
    D = 2 * W
    nb = MOE_STEP_BLOCKS
    n_steps = P // (nb * MOE_BLOCK)
    rows = lambda s, be, bv: (s, 0)
    wspec = lambda shape, j: pl.BlockSpec((None, None) + shape, lambda s, be, bv: (layer, be[s * nb + j], 0, 0))
    grid_spec = pltpu.PrefetchScalarGridSpec(
        num_scalar_prefetch=2,
        grid=(n_steps,),
        in_specs=([pl.BlockSpec((nb * MOE_BLOCK, W), rows)]
                  + [wspec((D, 2 * MOE_FF), j) for j in range(nb)] + [wspec((MOE_FF, D), j) for j in range(nb)]),
        out_specs=pl.BlockSpec((nb * MOE_BLOCK, W), rows),
        scratch_shapes=([pltpu.VMEM((D, 2 * MOE_FF), BF16)] * nb + [pltpu.VMEM((MOE_FF, D), BF16)] * nb),
    )
    return pl.pallas_call(
        _moe_body,
        grid_spec=grid_spec,
        out_shape=jax.ShapeDtypeStruct((P, W), U32),
        compiler_params=_cparams("arbitrary"),
        name="moe_experts",
    )(block_e, block_valid, xs, *([w_gate_up] * nb), *([w_down] * nb))


def _rank_body(rpt_ref, tri_ref, rank_ref, counts_ref, carry_ref, *, n_groups):
    @pl.when(pl.program_id(0) == 0)
    def _():
        carry_ref[...] = jnp.zeros_like(carry_ref)

    erow = lax.broadcasted_iota(I32, (MOE_EXPERTS, LANES), 0)
    carry = carry_ref[...]
    tri = tri_ref[...]
    r0, r1 = [], []
    for g in range(n_groups):
        c = slice(g * LANES, (g + 1) * LANES)
        oh0 = erow == rpt_ref[0:1, c].astype(I32)
        oh1 = erow == rpt_ref[1:2, c].astype(I32)
        both = jnp.where(oh0 | oh1, 1.0, 0.0)
        before = jnp.dot(both.astype(BF16), tri, preferred_element_type=F32) + carry
        r0.append(jnp.sum(jnp.where(oh0, before, 0.0), axis=0, keepdims=True))
        r1.append(jnp.sum(jnp.where(oh1, before, 0.0), axis=0, keepdims=True))
        carry = carry + jnp.sum(both, axis=1, keepdims=True)
    carry_ref[...] = carry
    counts_ref[...] = carry.astype(I32)
    tm = n_groups * LANES
    rank = jnp.concatenate([jnp.concatenate(r0, axis=1), jnp.concatenate(r1, axis=1),
                            jnp.zeros((SUBLANES - MOE_TOPK, tm), F32)], axis=0)
    rank_ref[...] = rank.astype(I32)


def _slot_ranks(rpt, tm=2048):
    T = rpt.shape[1]
    tm = min(tm, T)
    tri = (lax.broadcasted_iota(I32, (LANES, LANES), 0) < lax.broadcasted_iota(I32, (LANES, LANES), 1)).astype(BF16)
    return pl.pallas_call(
        functools.partial(_rank_body, n_groups=tm // LANES),
        grid=(T // tm,),
        in_specs=[pl.BlockSpec((SUBLANES, tm), lambda i: (0, i)), pl.BlockSpec((LANES, LANES), lambda i: (0, 0))],
        out_specs=[pl.BlockSpec((SUBLANES, tm), lambda i: (0, i)), pl.BlockSpec((MOE_EXPERTS, LANES), lambda i: (0, 0))],
        out_shape=[jax.ShapeDtypeStruct((SUBLANES, T), I32), jax.ShapeDtypeStruct((MOE_EXPERTS, LANES), I32)],
        scratch_shapes=[pltpu.VMEM((MOE_EXPERTS, LANES), F32)],
        compiler_params=_cparams("arbitrary"),
        name="slot_ranks",
    )(rpt, tri)


def _moe_dispatch_plan(rpt):
    T = rpt.shape[1]
    A = T * MOE_TOPK
    n_blocks = -(-(A + MOE_EXPERTS * (MOE_BLOCK - 1)) // MOE_BLOCK)
    n_blocks = -(-n_blocks // MOE_STEP_BLOCKS) * MOE_STEP_BLOCKS
    rank, counts = _slot_ranks(rpt)
    counts = counts[:, 0]
    padded = ((counts + MOE_BLOCK - 1) // MOE_BLOCK) * MOE_BLOCK
    pends = jnp.cumsum(padded)
    pstarts = pends - padded
    e = rpt[:MOE_TOPK].astype(I32)
    experts = jnp.arange(MOE_EXPERTS, dtype=I32)[:, None, None]
    dest = jnp.sum(jnp.where(e[None] == experts, pstarts[:, None, None], 0), axis=0) + rank[:MOE_TOPK]
    block_start = jnp.arange(n_blocks, dtype=I32) * MOE_BLOCK
    block_e = jnp.minimum(jnp.sum(pends[None, :] <= block_start[:, None], axis=1), MOE_EXPERTS - 1).astype(I32)
    block_valid = (block_start < pends[-1]).astype(I32)
    return dest, block_e, block_valid, n_blocks * MOE_BLOCK


SC_CORES = 2
SC_SUBCORES = 16
SC_WORKERS = SC_CORES * SC_SUBCORES
SC_ROWS = 64


def _sc_mesh():
    return plsc.VectorSubcoreMesh(core_axis_name="c", subcore_axis_name="s")


def _sc_scratch(n_chunks, width):
    return [pltpu.VMEM((n_chunks, SC_ROWS), I32), pltpu.VMEM((n_chunks, SC_ROWS), I32),
            pltpu.VMEM((SC_ROWS, width), U32), pltpu.VMEM((SC_ROWS, width), U32),
            pltpu.SemaphoreType.DMA, pltpu.SemaphoreType.DMA, pltpu.SemaphoreType.DMA, pltpu.SemaphoreType.DMA]


def _sc_scatter_rows(x, dest, P):
    T, W = x.shape
    tw = T // SC_WORKERS
    nch = tw // SC_ROWS
    assert tw * SC_WORKERS == T and nch * SC_ROWS == tw and nch % 2 == 0
    idx = dest.reshape(MOE_TOPK, SC_WORKERS, nch, SC_ROWS)

    @functools.partial(pl.kernel, mesh=_sc_mesh(), out_type=jax.ShapeDtypeStruct((P, W), U32),
                       scratch_types=_sc_scratch(nch, W))
    def scatter(x_hbm, i0_hbm, i1_hbm, o_hbm, i0_v, i1_v, b0, b1, r0, r1, s0, s1):
        wid = lax.axis_index("s") * SC_CORES + lax.axis_index("c")
        base = wid * tw
        pltpu.sync_copy(i0_hbm.at[wid], i0_v)
        pltpu.sync_copy(i1_hbm.at[wid], i1_v)

        @pl.loop(0, nch, step=2)
        def _(j):
            ca = pltpu.async_copy(x_hbm.at[pl.ds(base + j * SC_ROWS, SC_ROWS)], b0, r0)
            cb = pltpu.async_copy(x_hbm.at[pl.ds(base + (j + 1) * SC_ROWS, SC_ROWS)], b1, r1)
            ca.wait()
            a0 = pltpu.async_copy(b0, o_hbm.at[i0_v.at[j]], s0)
            a1 = pltpu.async_copy(b0, o_hbm.at[i1_v.at[j]], s0)
            cb.wait()
            e0 = pltpu.async_copy(b1, o_hbm.at[i0_v.at[j + 1]], s1)
            e1 = pltpu.async_copy(b1, o_hbm.at[i1_v.at[j + 1]], s1)
            a0.wait()
            a1.wait()
            e0.wait()
            e1.wait()

    return scatter(x, idx[0], idx[1])


def _sc_gather_rows(table, dest):
    T = dest.shape[1]
    W = table.shape[1]
    tw = T // SC_WORKERS
    nch = tw // SC_ROWS
    assert tw * SC_WORKERS == T and nch * SC_ROWS == tw
    idx = dest.reshape(MOE_TOPK, SC_WORKERS, nch, SC_ROWS)
    out = jax.ShapeDtypeStruct((T, W), U32)

    @functools.partial(pl.kernel, mesh=_sc_mesh(), out_type=[out, out], scratch_types=_sc_scratch(nch, W))
    def gather(tab_hbm, i0_hbm, i1_hbm, o0_hbm, o1_hbm, i0_v, i1_v, b0, b1, g0, g1, w0, w1):
        wid = lax.axis_index("s") * SC_CORES + lax.axis_index("c")
        base = wid * tw
        pltpu.sync_copy(i0_hbm.at[wid], i0_v)
        pltpu.sync_copy(i1_hbm.at[wid], i1_v)

        @pl.loop(0, nch)
        def _(j):
            rows = pl.ds(base + j * SC_ROWS, SC_ROWS)
            c0 = pltpu.async_copy(tab_hbm.at[i0_v.at[j]], b0, g0)
            c1 = pltpu.async_copy(tab_hbm.at[i1_v.at[j]], b1, g1)
            c0.wait()
            d0 = pltpu.async_copy(b0, o0_hbm.at[rows], w0)
            c1.wait()
            d1 = pltpu.async_copy(b1, o1_hbm.at[rows], w1)
            d0.wait()
            d1.wait()

    return gather(table, idx[0], idx[1])


def _moe(xn, rpt, w_gate_up, w_down, layer):
    dest, block_e, block_valid, P = _moe_dispatch_plan(rpt)
    xs = _sc_scatter_rows(xn, dest, P)
    y = _moe_experts(xs, block_e, block_valid, w_gate_up, w_down, layer)
    return _sc_gather_rows(y, dest)


def _final_body(h_ref, y0_ref, y1_ref, rp_ref, g_ref, o_ref):
    h = _combine(h_ref[...], y0_ref[...], y1_ref[...], rp_ref[...])
    o_ref[...] = h * lax.rsqrt(jnp.mean(h * h, axis=-1, keepdims=True) + NORM_EPS) * g_ref[...]


def _final(h, y0, y1, rp, gain, tm=1024):
    T, D = h.shape
    tm = min(tm, T)
    row = lambda i: (i, 0)
    return pl.pallas_call(
        _final_body,
        grid=(T // tm,),
        in_specs=[pl.BlockSpec((tm, D), row), pl.BlockSpec((tm, D // 2), row), pl.BlockSpec((tm, D // 2), row),
                  pl.BlockSpec((tm, LANES), row), pl.BlockSpec((1, D), lambda i: (0, 0))],
        out_specs=pl.BlockSpec((tm, D), row),
        out_shape=jax.ShapeDtypeStruct((T, D), F32),
        compiler_params=_cparams("parallel"),
        name="final_norm",
    )(h, y0, y1, rp, gain.reshape(1, D))


def _rope_tables(n_tok, head_dim):
    t = jnp.arange(n_tok, dtype=jnp.int32)
    row = (t // GRID_W).astype(F32)
    col = (t % GRID_W).astype(F32)
    half = head_dim // 2
    inv = ROPE_THETA ** (-jnp.arange(0, half, 2, dtype=F32) / half)
    ang_r = row[:, None] * inv[None, :]
    ang_c = col[:, None] * inv[None, :]
    ang = jnp.concatenate([ang_r, ang_r, ang_c, ang_c], axis=-1)
    q = half // 2
    sign = jnp.where((jnp.arange(head_dim) % half) < q, -1.0, 1.0).astype(F32)
    return jnp.cos(ang), jnp.sin(ang) * sign[None, :]


def _router_weights(w_group, b_group, w_expert, b_expert):
    D = w_group.shape[0]
    pad = LANES - MOE_EXPERTS - MOE_GROUPS
    w = jnp.concatenate([w_expert, w_group, jnp.zeros((D, pad), F32)], axis=1).astype(F32)
    b = jnp.concatenate([b_expert, b_group, jnp.zeros((pad,), F32)]).reshape(1, LANES)
    w_hi = w.astype(BF16)
    w_lo = (w - w_hi.astype(F32)).astype(BF16)
    return jnp.concatenate([w_hi, w_lo], axis=1), b.astype(F32)


def _trunk(x, p):
    B, S, D = x.shape
    T = B * S
    cos_r, sin_r = _rope_tables(S, RET_QK_DIM)
    cos_a, sin_a = [jnp.take(t, jnp.asarray(ATTN_HEAD_PERM, I32), axis=1) for t in _rope_tables(S, ATTN_HEAD_DIM)]
    h = x.reshape(T, D)

    proj = _in_proj(h, p["ln_mix"][0], p["ret_w_in"][0])
    y = _retention(proj, p["ret_decay_logit"][0], cos_r, sin_r, B, S)
    h, xn, rp, rpt = _out_proj(y, h, p["ret_w_out"][0], p["ln_ffn"][0], *p["router"][0])
    y0, y1 = _moe(xn, rpt, p["moe_w_gate_up"], p["moe_w_down"], 0)

    h, qn, kn, vx = _attn_in_proj(h, p["ln_mix"][1], p["attn_w_in"][0], (y0, y1, rp),
                                  p["attn_q_gain"][0], p["attn_k_gain"][0], cos_a, sin_a, S)
    y = _attention(qn, kn, vx, B, S)
    h, xn, rp, rpt = _out_proj(y, h, p["attn_w_out"][0], p["ln_ffn"][1], *p["router"][1])
    y0, y1 = _moe(xn, rpt, p["moe_w_gate_up"], p["moe_w_down"], 1)

    return _final(h, y0, y1, rp, p["ln_final"]).reshape(B, S, D)


def _prepare_params(ln_mix, ret_w_in, ret_decay_logit, ret_w_out, attn_w_in, attn_q_gain, attn_k_gain,
                    attn_w_out, ln_ffn, moe_w_group, moe_b_group, moe_w_expert, moe_b_expert, moe_w_gate_up,
                    moe_w_down, ln_final):
    d = ATTN_HEAD_DIM
    perm = jnp.asarray(ATTN_HEAD_PERM, I32)
    qk_cols = (jnp.arange((ATTN_Q_HEADS + ATTN_KV_HEADS) * d, dtype=I32) // d) * d
    qk_cols = qk_cols + jnp.tile(perm, ATTN_Q_HEADS + ATTN_KV_HEADS)
    cols = jnp.concatenate([qk_cols, jnp.arange(qk_cols.shape[0], ATTN_IN_WIDTH, dtype=I32)])
    return {
        "ln_mix": ln_mix, "ln_ffn": ln_ffn, "ln_final": ln_final,
        "ret_w_in": ret_w_in.astype(BF16), "ret_decay_logit": ret_decay_logit, "ret_w_out": ret_w_out.astype(BF16),
        "attn_w_in": jnp.take(attn_w_in, cols, axis=2).astype(BF16),
        "attn_q_gain": jnp.take(attn_q_gain, perm, axis=1), "attn_k_gain": jnp.take(attn_k_gain, perm, axis=1),
        "attn_w_out": attn_w_out.astype(BF16),
        "router": [_router_weights(moe_w_group[i], moe_b_group[i], moe_w_expert[i], moe_b_expert[i])
                   for i in range(moe_w_group.shape[0])],
        "moe_w_gate_up": moe_w_gate_up, "moe_w_down": moe_w_down,
    }


def kernel(x_prompt, x_sample, ln_mix, ret_w_in, ret_decay_logit, ret_w_out, attn_w_in, attn_q_gain, attn_k_gain,
           attn_w_out, ln_ffn, moe_w_group, moe_b_group, moe_w_expert, moe_b_expert, moe_w_gate_up, moe_w_down,
           ln_final):
    p = _prepare_params(ln_mix, ret_w_in, ret_decay_logit, ret_w_out, attn_w_in, attn_q_gain, attn_k_gain,
                        attn_w_out, ln_ffn, moe_w_group, moe_b_group, moe_w_expert, moe_b_expert, moe_w_gate_up,
                        moe_w_down, ln_final)
    return _trunk(x_prompt, p), _trunk(x_sample, p)
```

```python
import functools

import jax
import jax.numpy as jnp
from jax import lax
from jax.experimental import pallas as pl
from jax.experimental.pallas import tpu as pltpu
from jax.experimental.pallas import tpu_sc as plsc

F32 = jnp.float32
BF16 = jnp.bfloat16
U32 = jnp.uint32
I32 = jnp.int32

D_MODEL = 1024
GRID_W = 64
ROPE_THETA = 10000.0
NORM_EPS = 1e-6

RET_HEADS = 4
RET_QK_DIM = 256
RET_V_DIM = 512
RET_SCAN_CHUNK = 256
RET_QK_WIDTH = RET_HEADS * RET_QK_DIM
RET_V_WIDTH = RET_HEADS * RET_V_DIM

ATTN_Q_HEADS = 8
ATTN_KV_HEADS = 2
ATTN_HEAD_DIM = 128
ATTN_GROUP = ATTN_Q_HEADS // ATTN_KV_HEADS
ATTN_IN_WIDTH = (ATTN_Q_HEADS + 2 * ATTN_KV_HEADS) * ATTN_HEAD_DIM

MOE_GROUPS = 4
MOE_EXPERTS_PER_GROUP = 8
MOE_EXPERTS = MOE_GROUPS * MOE_EXPERTS_PER_GROUP
MOE_TOPK = 2
MOE_FF = 512
MOE_BLOCK = 512

LANES = 128
LOG2_E = 1.4426950408889634
VMEM_LIMIT = 56 * 1024 * 1024


def _cparams(*sem):
    return pltpu.CompilerParams(dimension_semantics=sem, vmem_limit_bytes=VMEM_LIMIT)


def _pack_bf16_pairs(x):
    w = x.shape[1] // 2
    lo = lax.bitcast_convert_type(x[:, :w].astype(BF16).astype(F32), U32)
    hi = lax.bitcast_convert_type(x[:, w:].astype(BF16).astype(F32), U32)
    return lax.shift_right_logical(lo, jnp.uint32(16)) | (hi & jnp.uint32(0xFFFF0000))


def _unpack_bf16_pairs(u):
    lo = lax.bitcast_convert_type(lax.shift_left(u, jnp.uint32(16)), F32)
    hi = lax.bitcast_convert_type(u & jnp.uint32(0xFFFF0000), F32)
    return lo, hi


def _combine(h, y0_u32, y1_u32, rp):
    lo0, hi0 = _unpack_bf16_pairs(y0_u32)
    lo1, hi1 = _unpack_bf16_pairs(y1_u32)
    w0 = rp[:, 2:3]
    w1 = rp[:, 3:4]
    half = h.shape[1] // 2
    return jnp.concatenate([h[:, :half] + w0 * lo0 + w1 * lo1, h[:, half:] + w0 * hi0 + w1 * hi1], axis=1)


def _in_proj_body(h_ref, g_ref, w_ref, o_ref, xn_ref):
    @pl.when(pl.program_id(1) == 0)
    def _():
        h = h_ref[...]
        xn = h * lax.rsqrt(jnp.mean(h * h, axis=-1, keepdims=True) + NORM_EPS) * g_ref[...]
        xn_ref[...] = xn.astype(BF16)

    o_ref[...] = jnp.dot(xn_ref[...], w_ref[...], preferred_element_type=F32).astype(o_ref.dtype)


IN_PROJ_SUB_ROWS = 512


def _in_proj(h, gain, w, tm=1024, tn=2048):
    T, D = h.shape
    N = w.shape[1]
    tm = min(tm, T)
    tn = min(tn, N)
    return pl.pallas_call(
        _in_proj_body,
        grid=(T // tm, N // tn),
        in_specs=[pl.BlockSpec((tm, D), lambda i, j: (i, 0)), pl.BlockSpec((1, D), lambda i, j: (0, 0)),
                  pl.BlockSpec((D, tn), lambda i, j: (0, j))],
        out_specs=pl.BlockSpec((tm, tn), lambda i, j: (i, j)),
        out_shape=jax.ShapeDtypeStruct((T, N), BF16),
        scratch_shapes=[pltpu.VMEM((tm, D), BF16)],
        compiler_params=_cparams("parallel", "arbitrary"),
        name="in_proj",
    )(h, gain.reshape(1, D), w)


def _rope256(x, cos, sin_signed):
    xr = jnp.concatenate([pltpu.roll(x[:, :LANES], 64, 1), pltpu.roll(x[:, LANES:], 64, 1)], axis=1)
    return x * cos + xr * sin_signed


def _retention_body(dl_ref, q_ref, k_ref, v_ref, g_ref, cos_ref, sin_ref, y_ref, state_ref, obwd_ref,
                    *, n_chunks, n_blocks):
    C = RET_SCAN_CHUNK
    h = pl.program_id(1)
    phase = pl.program_id(2)
    n = pl.program_id(3)
    sb = n_chunks * C

    def log_gamma(s):
        x = jnp.full((1, 1), s, F32)
        return jnp.minimum(x, 0.0) - jnp.log(1.0 + jnp.exp(-jnp.abs(x)))

    lg_f = log_gamma(dl_ref[h])
    lg_b = log_gamma(dl_ref[RET_HEADS + h])
    pos = lax.broadcasted_iota(jnp.int32, (C, RET_QK_DIM), 0).astype(F32)
    k_scale = RET_QK_DIM ** -0.5

    @pl.when(n == 0)
    def _():
        state_ref[...] = jnp.zeros_like(state_ref)

    def load(rows, q_dec, k_dec):
        cos = cos_ref[rows, :]
        sin = sin_ref[rows, :]
        q = _rope256(q_ref[rows, :], cos, sin)
        k = _rope256(k_ref[rows, :], cos, sin)
        return q, k, v_ref[rows, :], q * q_dec, (k * k_dec).T

    @pl.when(phase == 0)
    def _():
        q_dec = jnp.exp((C - pos) * lg_b).astype(BF16)
        k_dec = (jnp.exp(pos * lg_b) * k_scale).astype(BF16)
        chunk_dec = jnp.exp(lg_b * C)
        base = (n_blocks - 1 - n) * sb

        def body(cc, carry):
            c = n_chunks - 1 - cc
            rows = pl.ds(pl.multiple_of(c * C, C), C)
            _, _, v, qd, kt = load(rows, q_dec, k_dec)
            st = state_ref[...]
            out = jnp.dot(qd, st.astype(BF16), preferred_element_type=F32)
            state_ref[...] = st * chunk_dec + jnp.dot(kt, v, preferred_element_type=F32)
            obwd_ref[pl.ds(pl.multiple_of(base + c * C, C), C), :] = out.astype(obwd_ref.dtype)
            return carry

        lax.fori_loop(0, n_chunks, body, 0, unroll=min(4, n_chunks))

    @pl.when(phase == 1)
    def _():
        q_dec = jnp.exp((pos + 1.0) * lg_f).astype(BF16)
        k_dec = (jnp.exp((C - 1.0 - pos) * lg_f) * k_scale).astype(BF16)
        chunk_dec = jnp.exp(lg_f * C)
        ii = lax.broadcasted_iota(jnp.int32, (C, C), 0)
        jj = lax.broadcasted_iota(jnp.int32, (C, C), 1)
        dist = (ii - jj).astype(F32)
        inner = (jnp.where(dist >= 0, jnp.exp(lg_f * jnp.maximum(dist, 0.0)), 0.0)
                 + jnp.where(dist <= 0, jnp.exp(lg_b * jnp.maximum(-dist, 0.0)), 0.0)) * k_scale
        base = n * sb

        def body(c, carry):
            rows = pl.ds(pl.multiple_of(c * C, C), C)
            q, k, v, qd, kt = load(rows, q_dec, k_dec)
            scores = lax.dot_general(q, k, (((1,), (1,)), ((), ())), preferred_element_type=F32) * inner
            st = state_ref[...]
            lhs = jnp.concatenate([qd, scores.astype(BF16)], axis=1)
            rhs = jnp.concatenate([st.astype(BF16), v], axis=0)
            o = jnp.dot(lhs, rhs, preferred_element_type=F32)
            state_ref[...] = st * chunk_dec + jnp.dot(kt, v, preferred_element_type=F32)
            o = o + obwd_ref[pl.ds(pl.multiple_of(base + c * C, C), C), :].astype(F32)
            o = o * lax.rsqrt(jnp.mean(o * o, axis=-1, keepdims=True) + NORM_EPS)
            g = g_ref[rows, :]
            y_ref[rows, :] = g * jax.nn.sigmoid(g) * o.astype(BF16)
            return carry

        lax.fori_loop(0, n_chunks, body, 0, unroll=min(4, n_chunks))


def _retention(proj, decay_logit, cos, sin_signed, B, S, sb=2048):
    sb = min(sb, S)
    nb = S // sb
    proj3 = proj.reshape(B, S, proj.shape[-1])
    blk = lambda p, n: jnp.where(p == 0, nb - 1 - n, n)
    kq = RET_QK_WIDTH // RET_QK_DIM
    kv = 2 * RET_QK_WIDTH // RET_V_DIM
    kg = kv + RET_HEADS
    in_specs = [
        pl.BlockSpec(memory_space=pltpu.SMEM),
        pl.BlockSpec((None, sb, RET_QK_DIM), lambda b, h, p, n: (b, blk(p, n), h)),
        pl.BlockSpec((None, sb, RET_QK_DIM), lambda b, h, p, n: (b, blk(p, n), kq + h)),
        pl.BlockSpec((None, sb, RET_V_DIM), lambda b, h, p, n: (b, blk(p, n), kv + h)),
        pl.BlockSpec((None, sb, RET_V_DIM), lambda b, h, p, n: (b, n * p, kg + h)),
        pl.BlockSpec((sb, RET_QK_DIM), lambda b, h, p, n: (blk(p, n), 0)),
        pl.BlockSpec((sb, RET_QK_DIM), lambda b, h, p, n: (blk(p, n), 0)),
    ]
    out = pl.pallas_call(
        functools.partial(_retention_body, n_chunks=sb // RET_SCAN_CHUNK, n_blocks=nb),
        grid=(B, RET_HEADS, 2, nb),
        in_specs=in_specs,
        out_specs=pl.BlockSpec((None, sb, RET_V_DIM), lambda b, h, p, n: (b, n * p, h)),
        out_shape=jax.ShapeDtypeStruct((B, S, RET_V_WIDTH), BF16),
        scratch_shapes=[pltpu.VMEM((RET_QK_DIM, RET_V_DIM), F32), pltpu.VMEM((S, RET_V_DIM), BF16)],
        compiler_params=_cparams("parallel", "parallel", "arbitrary", "arbitrary"),
        name="retention",
    )(decay_logit.reshape(2 * RET_HEADS).astype(F32), proj3, proj3, proj3, proj3,
      cos.astype(BF16), sin_signed.astype(BF16))
    return out.reshape(B * S, RET_V_WIDTH)


ATTN_HEAD_PERM = tuple(list(range(0, 32)) + list(range(64, 96)) + list(range(32, 64)) + list(range(96, 128)))


def _norm_rope_head_pair(x, gain2, cos2, sin2):
    d = ATTN_HEAD_DIM
    bi = lax.broadcasted_iota(I32, (2 * d, 2 * d), 0) // d
    bj = lax.broadcasted_iota(I32, (2 * d, 2 * d), 1) // d
    ones_bd = jnp.where(bi == bj, 1.0, 0.0).astype(BF16)
    ss = jnp.dot((x * x).astype(BF16), ones_bd, preferred_element_type=F32)
    x = x * lax.rsqrt(ss * (1.0 / d) + NORM_EPS) * gain2
    return _rope256(x, cos2, sin2)


def _attn_in_proj_body(h_ref, y0_ref, y1_ref, rp_ref, g_ref, w_ref, qg_ref, kg_ref, cos_ref, sin_ref,
                       hout_ref, q_ref, k_ref, v_ref):
    d = ATTN_HEAD_DIM
    tm = h_ref.shape[0]
    sub = min(IN_PROJ_SUB_ROWS, tm)
    q_scale = (d ** -0.5) * LOG2_E
    for s in range(tm // sub):
        rows = slice(s * sub, (s + 1) * sub)
        h = _combine(h_ref[rows, :], y0_ref[rows, :], y1_ref[rows, :], rp_ref[rows, :])
        hout_ref[rows, :] = h
        xn = (h * lax.rsqrt(jnp.mean(h * h, axis=-1, keepdims=True) + NORM_EPS) * g_ref[...]).astype(BF16)
        acc = jnp.dot(xn, w_ref[...], preferred_element_type=F32)
        cos2 = jnp.concatenate([cos_ref[rows, :]] * 2, axis=1)
        sin2 = jnp.concatenate([sin_ref[rows, :]] * 2, axis=1)
        qg2 = jnp.concatenate([qg_ref[...]] * 2, axis=1) * q_scale
        kg2 = jnp.concatenate([kg_ref[...]] * 2, axis=1)
        for hp in range(ATTN_Q_HEADS // 2):
            c = slice(2 * hp * d, (2 * hp + 2) * d)
            q_ref[rows, c] = _norm_rope_head_pair(acc[:, c], qg2, cos2, sin2).astype(q_ref.dtype)
        for hp in range(ATTN_KV_HEADS // 2):
            c0 = (ATTN_Q_HEADS + 2 * hp) * d
            k = _norm_rope_head_pair(acc[:, c0:c0 + 2 * d], kg2, cos2, sin2)
            k_ref[rows, 2 * hp * d:(2 * hp + 2) * d] = k.astype(k_ref.dtype)
        for hk in range(ATTN_KV_HEADS):
            c1 = (ATTN_Q_HEADS + ATTN_KV_HEADS + hk) * d
            v_ref[rows, 2 * hk * d:(2 * hk + 1) * d] = acc[:, c1:c1 + d].astype(v_ref.dtype)
            v_ref[rows, (2 * hk + 1) * d:(2 * hk + 2) * d] = jnp.ones((sub, d), v_ref.dtype)


def _attn_in_proj(h, gain, w, comb, q_gain, k_gain, cos, sin_signed, S, tm=1024):
    T, D = h.shape
    N = w.shape[1]
    tm = min(tm, S)
    spb = S // tm
    d = ATTN_HEAD_DIM
    y0, y1, rp = comb
    row = lambda i: (i, 0)
    const = lambda i: (0, 0)
    tab = lambda i: (i % spb, 0)
    return pl.pallas_call(
        _attn_in_proj_body,
        grid=(T // tm,),
        in_specs=[pl.BlockSpec((tm, D), row), pl.BlockSpec((tm, D // 2), row), pl.BlockSpec((tm, D // 2), row),
                  pl.BlockSpec((tm, LANES), row), pl.BlockSpec((1, D), const), pl.BlockSpec((D, N), const),
                  pl.BlockSpec((1, d), const), pl.BlockSpec((1, d), const),
                  pl.BlockSpec((tm, d), tab), pl.BlockSpec((tm, d), tab)],
        out_specs=[pl.BlockSpec((tm, D), row), pl.BlockSpec((tm, ATTN_Q_HEADS * d), row),
                   pl.BlockSpec((tm, ATTN_KV_HEADS * d), row), pl.BlockSpec((tm, 2 * ATTN_KV_HEADS * d), row)],
        out_shape=[jax.ShapeDtypeStruct((T, D), F32), jax.ShapeDtypeStruct((T, ATTN_Q_HEADS * d), BF16),
                   jax.ShapeDtypeStruct((T, ATTN_KV_HEADS * d), BF16),
                   jax.ShapeDtypeStruct((T, 2 * ATTN_KV_HEADS * d), BF16)],
        compiler_params=_cparams("parallel"),
        name="attn_in_proj",
    )(h, y0, y1, rp, gain.reshape(1, D), w, q_gain.reshape(1, d), k_gain.reshape(1, d), cos, sin_signed)


def _attn_body(q_ref, k_ref, v_ref, o_ref, qs_ref, m_ref, acc_ref, *, tq, tk, n_kv):
    d = ATTN_HEAD_DIM
    for g in range(ATTN_GROUP):
        qs_ref[g * tq:(g + 1) * tq, :] = q_ref[:, g * d:(g + 1) * d]
    m_ref[...] = jnp.full_like(m_ref, -jnp.inf)
    acc_ref[...] = jnp.zeros_like(acc_ref)
    rep = tk // LANES

    def step(c, carry):
        rows = pl.ds(pl.multiple_of(c * tk, tk), tk)
        k = k_ref[rows, :]
        v = v_ref[rows, :]
        for g in range(ATTN_GROUP):
            r = slice(g * tq, (g + 1) * tq)
            s = lax.dot_general(qs_ref[r, :], k, (((1,), (1,)), ((), ())), preferred_element_type=F32)
            m_prev = m_ref[r, :]
            m_new = jnp.maximum(m_prev, jnp.max(s, axis=-1, keepdims=True))
            alpha = jnp.exp2(m_prev - m_new)
            p = jnp.exp2(s - jnp.concatenate([m_new] * rep, axis=1))
            pv = jnp.dot(p.astype(BF16), v, preferred_element_type=F32)
            acc_ref[r, :] = jnp.concatenate([alpha, alpha], axis=1) * acc_ref[r, :] + pv
            m_ref[r, :] = m_new
        return carry

    lax.fori_loop(0, n_kv, step, 0, unroll=min(8, n_kv))
    for g in range(ATTN_GROUP):
        r = slice(g * tq, (g + 1) * tq)
        o_ref[:, g * d:(g + 1) * d] = (acc_ref[r, :d] / acc_ref[r, d:]).astype(o_ref.dtype)


def _attention(qn, kn, vx, B, S, tq=512, tk=256):
    d = ATTN_HEAD_DIM
    tq = min(tq, S)
    tk = min(tk, S)
    gw = ATTN_GROUP * d
    q3 = qn.reshape(B, S, ATTN_Q_HEADS * d)
    k3 = kn.reshape(B, S, ATTN_KV_HEADS * d)
    v3 = vx.reshape(B, S, 2 * ATTN_KV_HEADS * d)
    m = ATTN_GROUP * tq
    out = pl.pallas_call(
        functools.partial(_attn_body, tq=tq, tk=tk, n_kv=S // tk),
        grid=(B, ATTN_KV_HEADS, S // tq),
        in_specs=[pl.BlockSpec((None, tq, gw), lambda b, kh, i: (b, i, kh)),
                  pl.BlockSpec((None, S, d), lambda b, kh, i: (b, 0, kh)),
                  pl.BlockSpec((None, S, 2 * d), lambda b, kh, i: (b, 0, kh))],
        out_specs=pl.BlockSpec((None, tq, gw), lambda b, kh, i: (b, i, kh)),
        out_shape=jax.ShapeDtypeStruct((B, S, ATTN_Q_HEADS * d), BF16),
        scratch_shapes=[pltpu.VMEM((m, d), BF16), pltpu.VMEM((m, LANES), F32), pltpu.VMEM((m, 2 * d), F32)],
        compiler_params=_cparams("parallel", "parallel", "arbitrary"),
        name="attention",
    )(q3, k3, v3)
    return out.reshape(B * S, ATTN_Q_HEADS * d)


ROUTER_GROUP_ROW = MOE_EXPERTS
SUBLANES = 8
assert MOE_EXPERTS_PER_GROUP == SUBLANES and MOE_GROUPS <= SUBLANES


def _route(logits_t):
    tm = logits_t.shape[1]
    row = lax.broadcasted_iota(jnp.int32, (SUBLANES, tm), 0)
    neg = jnp.float32(-jnp.inf)
    big = jnp.int32(SUBLANES)

    def first_argmax(x):
        mx = jnp.max(x, axis=0, keepdims=True)
        return mx, jnp.min(jnp.where(x == mx, row, big), axis=0, keepdims=True)

    gl = jnp.where(row < MOE_GROUPS, logits_t[ROUTER_GROUP_ROW:ROUTER_GROUP_ROW + SUBLANES, :], neg)
    gmax, g_sel = first_argmax(gl)
    g_w = 1.0 / jnp.sum(jnp.exp(gl - gmax), axis=0, keepdims=True)
    el = logits_t[0:SUBLANES, :]
    for g in range(1, MOE_GROUPS):
        el = jnp.where(g_sel == g, logits_t[g * SUBLANES:(g + 1) * SUBLANES, :], el)
    v1, i1 = first_argmax(el)
    v2, i2 = first_argmax(jnp.where(row == i1, neg, el))
    e2 = jnp.exp(v2 - v1)
    w1 = g_w / (1.0 + e2)
    w2 = g_w * e2 / (1.0 + e2)
    base = g_sel * MOE_EXPERTS_PER_GROUP
    return jnp.where(row == 0, (base + i1).astype(F32),
                     jnp.where(row == 1, (base + i2).astype(F32),
                               jnp.where(row == 2, w1, jnp.where(row == 3, w2, 0.0))))


def _out_proj_body(y_ref, h_ref, w_ref, g_ref, wr_ref, br_ref, hout_ref, xn_ref, rp_ref, rpt_ref):
    tm = h_ref.shape[0]
    sub = min(OUT_PROJ_SUB_ROWS, tm)
    wr = wr_ref[...]
    for s in range(tm // sub):
        rows = slice(s * sub, (s + 1) * sub)
        h = h_ref[rows, :] + jnp.dot(y_ref[rows, :], w_ref[...], preferred_element_type=F32)
        hout_ref[rows, :] = h
        xn = h * lax.rsqrt(jnp.mean(h * h, axis=-1, keepdims=True) + NORM_EPS) * g_ref[...]
        xn_ref[rows, :] = _pack_bf16_pairs(xn)
        xn_hi = xn.astype(BF16)
        xn_lo = (xn - xn_hi.astype(F32)).astype(BF16)
        hi = jnp.dot(xn_hi, wr, preferred_element_type=F32)
        lo = jnp.dot(xn_lo, wr[:, :LANES], preferred_element_type=F32)
        logits = hi[:, :LANES] + hi[:, LANES:] + lo + br_ref[...]
        rpt = _route(logits.T)
        rpt_ref[:, rows] = rpt
        full = jnp.concatenate([rpt, jnp.zeros((LANES - SUBLANES, sub), F32)], axis=0)
        rp_ref[rows, :] = full.T


OUT_PROJ_SUB_ROWS = 512


def _out_proj(y, h, w, gain, w_router, b_router, tm=1024):
    T, K = y.shape
    D = h.shape[1]
    tm = min(tm, T)
    row = lambda i: (i, 0)
    const = lambda i: (0, 0)
    return pl.pallas_call(
        _out_proj_body,
        grid=(T // tm,),
        in_specs=[pl.BlockSpec((tm, K), row), pl.BlockSpec((tm, D), row), pl.BlockSpec((K, D), const),
                  pl.BlockSpec((1, D), const), pl.BlockSpec((D, 2 * LANES), const), pl.BlockSpec((1, LANES), const)],
        out_specs=[pl.BlockSpec((tm, D), row), pl.BlockSpec((tm, D // 2), row), pl.BlockSpec((tm, LANES), row),
                   pl.BlockSpec((SUBLANES, tm), lambda i: (0, i))],
        out_shape=[jax.ShapeDtypeStruct((T, D), F32), jax.ShapeDtypeStruct((T, D // 2), U32),
                   jax.ShapeDtypeStruct((T, LANES), F32), jax.ShapeDtypeStruct((SUBLANES, T), F32)],
        compiler_params=_cparams("parallel"),
        name="out_proj",
    )(y, h, w, gain.reshape(1, D), w_router, b_router)


MOE_STEP_BLOCKS = 2


def _moe_body(be_ref, bv_ref, x_ref, *refs):
    nb = MOE_STEP_BLOCKS
    wgu_refs, wd_refs = refs[:nb], refs[nb:2 * nb]
    y_ref = refs[2 * nb]
    wgu_bf_refs, wd_bf_refs = refs[2 * nb + 1:3 * nb + 1], refs[3 * nb + 1:]
    s = pl.program_id(0)

    for j in range(nb):
        b = s * nb + j

        @pl.when((s == 0) | (be_ref[b] != be_ref[jnp.maximum(b - nb, 0)]))
        def _():
            wgu_bf_refs[j][...] = wgu_refs[j][...].astype(BF16)
            wd_bf_refs[j][...] = wd_refs[j][...].astype(BF16)

    @pl.when(bv_ref[s * nb] != 0)
    def _():
        for j in range(nb):
            rows = slice(j * MOE_BLOCK, (j + 1) * MOE_BLOCK)
            lo, hi = _unpack_bf16_pairs(x_ref[rows, :])
            x = jnp.concatenate([lo.astype(BF16), hi.astype(BF16)], axis=1)
            gu = jnp.dot(x, wgu_bf_refs[j][...], preferred_element_type=F32)
            gate = gu[:, :MOE_FF]
            up = gu[:, MOE_FF:]
            act = (gate * jax.nn.sigmoid(gate) * up).astype(BF16)
            y_ref[rows, :] = _pack_bf16_pairs(jnp.dot(act, wd_bf_refs[j][...], preferred_element_type=F32))

    @pl.when(bv_ref[s * nb] == 0)
    def _():
        y_ref[...] = jnp.zeros_like(y_ref)


def _moe_experts(xs, block_e, block_valid, w_gate_up, w_down, layer):
    P, W = xs.shape
    D = 2 * W
    nb = MOE_STEP_BLOCKS
    n_steps = P // (nb * MOE_BLOCK)
    rows = lambda s, be, bv: (s, 0)
    wspec = lambda shape, j: pl.BlockSpec((None, None) + shape, lambda s, be, bv: (layer, be[s * nb + j], 0, 0))
    grid_spec = pltpu.PrefetchScalarGridSpec(
        num_scalar_prefetch=2,
        grid=(n_steps,),
        in_specs=([pl.BlockSpec((nb * MOE_BLOCK, W), rows)]
                  + [wspec((D, 2 * MOE_FF), j) for j in range(nb)] + [wspec((MOE_FF, D), j) for j in range(nb)]),
        out_specs=pl.BlockSpec((nb * MOE_BLOCK, W), rows),
        scratch_shapes=([pltpu.VMEM((D, 2 * MOE_FF), BF16)] * nb + [pltpu.VMEM((MOE_FF, D), BF16)] * nb),
    )
    return pl.pallas_call(
        _moe_body,
        grid_spec=grid_spec,
        out_shape=jax.ShapeDtypeStruct((P, W), U32),
        compiler_params=_cparams("arbitrary"),
        name="moe_experts",
    )(block_e, block_valid, xs, *([w_gate_up] * nb), *([w_down] * nb))


def _rank_body(rpt_ref, tri_ref, rank_ref, counts_ref, carry_ref, *, n_groups):
    @pl.when(pl.program_id(0) == 0)
    def _():
        carry_ref[...] = jnp.zeros_like(carry_ref)

    erow = lax.broadcasted_iota(I32, (MOE_EXPERTS, LANES), 0)
    carry = carry_ref[...]
    tri = tri_ref[...]
    r0, r1 = [], []
    for g in range(n_groups):
        c = slice(g * LANES, (g + 1) * LANES)
        oh0 = erow == rpt_ref[0:1, c].astype(I32)
        oh1 = erow == rpt_ref[1:2, c].astype(I32)
        both = jnp.where(oh0 | oh1, 1.0, 0.0)
        before = jnp.dot(both.astype(BF16), tri, preferred_element_type=F32) + carry
        r0.append(jnp.sum(jnp.where(oh0, before, 0.0), axis=0, keepdims=True))
        r1.append(jnp.sum(jnp.where(oh1, before, 0.0), axis=0, keepdims=True))
        carry = carry + jnp.sum(both, axis=1, keepdims=True)
    carry_ref[...] = carry
    counts_ref[...] = carry.astype(I32)
    tm = n_groups * LANES
    rank = jnp.concatenate([jnp.concatenate(r0, axis=1), jnp.concatenate(r1, axis=1),
                            jnp.zeros((SUBLANES - MOE_TOPK, tm), F32)], axis=0)
    rank_ref[...] = rank.astype(I32)


def _slot_ranks(rpt, tm=2048):
    T = rpt.shape[1]
    tm = min(tm, T)
    tri = (lax.broadcasted_iota(I32, (LANES, LANES), 0) < lax.broadcasted_iota(I32, (LANES, LANES), 1)).astype(BF16)
    return pl.pallas_call(
        functools.partial(_rank_body, n_groups=tm // LANES),
        grid=(T // tm,),
        in_specs=[pl.BlockSpec((SUBLANES, tm), lambda i: (0, i)), pl.BlockSpec((LANES, LANES), lambda i: (0, 0))],
        out_specs=[pl.BlockSpec((SUBLANES, tm), lambda i: (0, i)), pl.BlockSpec((MOE_EXPERTS, LANES), lambda i: (0, 0))],
        out_shape=[jax.ShapeDtypeStruct((SUBLANES, T), I32), jax.ShapeDtypeStruct((MOE_EXPERTS, LANES), I32)],
        scratch_shapes=[pltpu.VMEM((MOE_EXPERTS, LANES), F32)],
        compiler_params=_cparams("arbitrary"),
        name="slot_ranks",
    )(rpt, tri)


def _moe_dispatch_plan(rpt):
    T = rpt.shape[1]
    A = T * MOE_TOPK
    n_blocks = -(-(A + MOE_EXPERTS * (MOE_BLOCK - 1)) // MOE_BLOCK)
    n_blocks = -(-n_blocks // MOE_STEP_BLOCKS) * MOE_STEP_BLOCKS
    rank, counts = _slot_ranks(rpt)
    counts = counts[:, 0]
    padded = ((counts + MOE_BLOCK - 1) // MOE_BLOCK) * MOE_BLOCK
    pends = jnp.cumsum(padded)
    pstarts = pends - padded
    e = rpt[:MOE_TOPK].astype(I32)
    experts = jnp.arange(MOE_EXPERTS, dtype=I32)[:, None, None]
    dest = jnp.sum(jnp.where(e[None] == experts, pstarts[:, None, None], 0), axis=0) + rank[:MOE_TOPK]
    block_start = jnp.arange(n_blocks, dtype=I32) * MOE_BLOCK
    block_e = jnp.minimum(jnp.sum(pends[None, :] <= block_start[:, None], axis=1), MOE_EXPERTS - 1).astype(I32)
    block_valid = (block_start < pends[-1]).astype(I32)
    return dest, block_e, block_valid, n_blocks * MOE_BLOCK


SC_CORES = 2
SC_SUBCORES = 16
SC_WORKERS = SC_CORES * SC_SUBCORES
SC_ROWS = 64


def _sc_mesh():
    return plsc.VectorSubcoreMesh(core_axis_name="c", subcore_axis_name="s")


def _sc_scratch(n_chunks, width):
    return [pltpu.VMEM((n_chunks, SC_ROWS), I32), pltpu.VMEM((n_chunks, SC_ROWS), I32),
            pltpu.VMEM((SC_ROWS, width), U32), pltpu.VMEM((SC_ROWS, width), U32),
            pltpu.SemaphoreType.DMA, pltpu.SemaphoreType.DMA, pltpu.SemaphoreType.DMA, pltpu.SemaphoreType.DMA]


def _sc_scatter_rows(x, dest, P):
    T, W = x.shape
    tw = T // SC_WORKERS
    nch = tw // SC_ROWS
    assert tw * SC_WORKERS == T and nch * SC_ROWS == tw and nch % 2 == 0
    idx = dest.reshape(MOE_TOPK, SC_WORKERS, nch, SC_ROWS)

    @functools.partial(pl.kernel, mesh=_sc_mesh(), out_type=jax.ShapeDtypeStruct((P, W), U32),
                       scratch_types=_sc_scratch(nch, W))
    def scatter(x_hbm, i0_hbm, i1_hbm, o_hbm, i0_v, i1_v, b0, b1, r0, r1, s0, s1):
        wid = lax.axis_index("s") * SC_CORES + lax.axis_index("c")
        base = wid * tw
        pltpu.sync_copy(i0_hbm.at[wid], i0_v)
        pltpu.sync_copy(i1_hbm.at[wid], i1_v)

        @pl.loop(0, nch, step=2)
        def _(j):
            ca = pltpu.async_copy(x_hbm.at[pl.ds(base + j * SC_ROWS, SC_ROWS)], b0, r0)
            cb = pltpu.async_copy(x_hbm.at[pl.ds(base + (j + 1) * SC_ROWS, SC_ROWS)], b1, r1)
            ca.wait()
            a0 = pltpu.async_copy(b0, o_hbm.at[i0_v.at[j]], s0)
            a1 = pltpu.async_copy(b0, o_hbm.at[i1_v.at[j]], s0)
            cb.wait()
            e0 = pltpu.async_copy(b1, o_hbm.at[i0_v.at[j + 1]], s1)
            e1 = pltpu.async_copy(b1, o_hbm.at[i1_v.at[j + 1]], s1)
            a0.wait()
            a1.wait()
            e0.wait()
            e1.wait()

    return scatter(x, idx[0], idx[1])


def _sc_gather_rows(table, dest):
    T = dest.shape[1]
    W = table.shape[1]
    tw = T // SC_WORKERS
    nch = tw // SC_ROWS
    assert tw * SC_WORKERS == T and nch * SC_ROWS == tw
    idx = dest.reshape(MOE_TOPK, SC_WORKERS, nch, SC_ROWS)
    out = jax.ShapeDtypeStruct((T, W), U32)

    @functools.partial(pl.kernel, mesh=_sc_mesh(), out_type=[out, out], scratch_types=_sc_scratch(nch, W))
    def gather(tab_hbm, i0_hbm, i1_hbm, o0_hbm, o1_hbm, i0_v, i1_v, b0, b1, g0, g1, w0, w1):
        wid = lax.axis_index("s") * SC_CORES + lax.axis_index("c")
        base = wid * tw
        pltpu.sync_copy(i0_hbm.at[wid], i0_v)
        pltpu.sync_copy(i1_hbm.at[wid], i1_v)

        @pl.loop(0, nch)
        def _(j):
            rows = pl.ds(base + j * SC_ROWS, SC_ROWS)
            c0 = pltpu.async_copy(tab_hbm.at[i0_v.at[j]], b0, g0)
            c1 = pltpu.async_copy(tab_hbm.at[i1_v.at[j]], b1, g1)
            c0.wait()
            d0 = pltpu.async_copy(b0, o0_hbm.at[rows], w0)
            c1.wait()
            d1 = pltpu.async_copy(b1, o1_hbm.at[rows], w1)
            d0.wait()
            d1.wait()

    return gather(table, idx[0], idx[1])


def _moe(xn, rpt, w_gate_up, w_down, layer):
    dest, block_e, block_valid, P = _moe_dispatch_plan(rpt)
    xs = _sc_scatter_rows(xn, dest, P)
    y = _moe_experts(xs, block_e, block_valid, w_gate_up, w_down, layer)
    return _sc_gather_rows(y, dest)


def _final_body(h_ref, y0_ref, y1_ref, rp_ref, g_ref, o_ref):
    h = _combine(h_ref[...], y0_ref[...], y1_ref[...], rp_ref[...])
    o_ref[...] = h * lax.rsqrt(jnp.mean(h * h, axis=-1, keepdims=True) + NORM_EPS) * g_ref[...]


def _final(h, y0, y1, rp, gain, tm=1024):
    T, D = h.shape
    tm = min(tm, T)
    row = lambda i: (i, 0)
    return pl.pallas_call(
        _final_body,
        grid=(T // tm,),
        in_specs=[pl.BlockSpec((tm, D), row), pl.BlockSpec((tm, D // 2), row), pl.BlockSpec((tm, D // 2), row),
                  pl.BlockSpec((tm, LANES), row), pl.BlockSpec((1, D), lambda i: (0, 0))],
        out_specs=pl.BlockSpec((tm, D), row),
        out_shape=jax.ShapeDtypeStruct((T, D), F32),
        compiler_params=_cparams("parallel"),
        name="final_norm",
    )(h, y0, y1, rp, gain.reshape(1, D))


def _rope_tables(n_tok, head_dim):
    t = jnp.arange(n_tok, dtype=jnp.int32)
    row = (t // GRID_W).astype(F32)
    col = (t % GRID_W).astype(F32)
    half = head_dim // 2
    inv = ROPE_THETA ** (-jnp.arange(0, half, 2, dtype=F32) / half)
    ang_r = row[:, None] * inv[None, :]
    ang_c = col[:, None] * inv[None, :]
    ang = jnp.concatenate([ang_r, ang_r, ang_c, ang_c], axis=-1)
    q = half // 2
    sign = jnp.where((jnp.arange(head_dim) % half) < q, -1.0, 1.0).astype(F32)
    return jnp.cos(ang), jnp.sin(ang) * sign[None, :]


def _router_weights(w_group, b_group, w_expert, b_expert):
    D = w_group.shape[0]
    pad = LANES - MOE_EXPERTS - MOE_GROUPS
    w = jnp.concatenate([w_expert, w_group, jnp.zeros((D, pad), F32)], axis=1).astype(F32)
    b = jnp.concatenate([b_expert, b_group, jnp.zeros((pad,), F32)]).reshape(1, LANES)
    w_hi = w.astype(BF16)
    w_lo = (w - w_hi.astype(F32)).astype(BF16)
    return jnp.concatenate([w_hi, w_lo], axis=1), b.astype(F32)


def _trunk(x, p):
    B, S, D = x.shape
    T = B * S
    cos_r, sin_r = _rope_tables(S, RET_QK_DIM)
    cos_a, sin_a = [jnp.take(t, jnp.asarray(ATTN_HEAD_PERM, I32), axis=1) for t in _rope_tables(S, ATTN_HEAD_DIM)]
    h = x.reshape(T, D)

    proj = _in_proj(h, p["ln_mix"][0], p["ret_w_in"][0])
    y = _retention(proj, p["ret_decay_logit"][0], cos_r, sin_r, B, S)
    h, xn, rp, rpt = _out_proj(y, h, p["ret_w_out"][0], p["ln_ffn"][0], *p["router"][0])
    y0, y1 = _moe(xn, rpt, p["moe_w_gate_up"], p["moe_w_down"], 0)

    h, qn, kn, vx = _attn_in_proj(h, p["ln_mix"][1], p["attn_w_in"][0], (y0, y1, rp),
                                  p["attn_q_gain"][0], p["attn_k_gain"][0], cos_a, sin_a, S)
    y = _attention(qn, kn, vx, B, S)
    h, xn, rp, rpt = _out_proj(y, h, p["attn_w_out"][0], p["ln_ffn"][1], *p["router"][1])
    y0, y1 = _moe(xn, rpt, p["moe_w_gate_up"], p["moe_w_down"], 1)

    return _final(h, y0, y1, rp, p["ln_final"]).reshape(B, S, D)


def _prepare_params(ln_mix, ret_w_in, ret_decay_logit, ret_w_out, attn_w_in, attn_q_gain, attn_k_gain,
                    attn_w_out, ln_ffn, moe_w_group, moe_b_group, moe_w_expert, moe_b_expert, moe_w_gate_up,
                    moe_w_down, ln_final):
    d = ATTN_HEAD_DIM
    perm = jnp.asarray(ATTN_HEAD_PERM, I32)
    qk_cols = (jnp.arange((ATTN_Q_HEADS + ATTN_KV_HEADS) * d, dtype=I32) // d) * d
    qk_cols = qk_cols + jnp.tile(perm, ATTN_Q_HEADS + ATTN_KV_HEADS)
    cols = jnp.concatenate([qk_cols, jnp.arange(qk_cols.shape[0], ATTN_IN_WIDTH, dtype=I32)])
    return {
        "ln_mix": ln_mix, "ln_ffn": ln_ffn, "ln_final": ln_final,
        "ret_w_in": ret_w_in.astype(BF16), "ret_decay_logit": ret_decay_logit, "ret_w_out": ret_w_out.astype(BF16),
        "attn_w_in": jnp.take(attn_w_in, cols, axis=2).astype(BF16),
        "attn_q_gain": jnp.take(attn_q_gain, perm, axis=1), "attn_k_gain": jnp.take(attn_k_gain, perm, axis=1),
        "attn_w_out": attn_w_out.astype(BF16),
        "router": [_router_weights(moe_w_group[i], moe_b_group[i], moe_w_expert[i], moe_b_expert[i])
                   for i in range(moe_w_group.shape[0])],
        "moe_w_gate_up": moe_w_gate_up, "moe_w_down": moe_w_down,
    }


def kernel(x_prompt, x_sample, ln_mix, ret_w_in, ret_decay_logit, ret_w_out, attn_w_in, attn_q_gain, attn_k_gain,
           attn_w_out, ln_ffn, moe_w_group, moe_b_group, moe_w_expert, moe_b_expert, moe_w_gate_up, moe_w_down,
           ln_final):
    p = _prepare_params(ln_mix, ret_w_in, ret_decay_logit, ret_w_out, attn_w_in, attn_q_gain, attn_k_gain,
                        attn_w_out, ln_ffn, moe_w_group, moe_b_group, moe_w_expert, moe_b_expert, moe_w_gate_up,
                        moe_w_down, ln_final)
    return _trunk(x_prompt, p), _trunk(x_sample, p)
```

```python
import functools

import jax
import jax.numpy as jnp
from jax import lax
from jax.experimental import pallas as pl
from jax.experimental.pallas import tpu as pltpu
from jax.experimental.pallas import tpu_sc as plsc

F32 = jnp.float32
BF16 = jnp.bfloat16
U32 = jnp.uint32
I32 = jnp.int32

D_MODEL = 1024
GRID_W = 64
ROPE_THETA = 10000.0
NORM_EPS = 1e-6

RET_HEADS = 4
RET_QK_DIM = 256
RET_V_DIM = 512
RET_SCAN_CHUNK = 256
RET_QK_WIDTH = RET_HEADS * RET_QK_DIM
RET_V_WIDTH = RET_HEADS * RET_V_DIM

ATTN_Q_HEADS = 8
ATTN_KV_HEADS = 2
ATTN_HEAD_DIM = 128
ATTN_GROUP = ATTN_Q_HEADS // ATTN_KV_HEADS
ATTN_IN_WIDTH = (ATTN_Q_HEADS + 2 * ATTN_KV_HEADS) * ATTN_HEAD_DIM

MOE_GROUPS = 4
MOE_EXPERTS_PER_GROUP = 8
MOE_EXPERTS = MOE_GROUPS * MOE_EXPERTS_PER_GROUP
MOE_TOPK = 2
MOE_FF = 512
MOE_BLOCK = 512

LANES = 128
LOG2_E = 1.4426950408889634
VMEM_LIMIT = 56 * 1024 * 1024


def _cparams(*sem):
    return pltpu.CompilerParams(dimension_semantics=sem, vmem_limit_bytes=VMEM_LIMIT)


def _pack_bf16_pairs(x):
    w = x.shape[1] // 2
    lo = lax.bitcast_convert_type(x[:, :w].astype(BF16).astype(F32), U32)
    hi = lax.bitcast_convert_type(x[:, w:].astype(BF16).astype(F32), U32)
    return lax.shift_right_logical(lo, jnp.uint32(16)) | (hi & jnp.uint32(0xFFFF0000))


def _unpack_bf16_pairs(u):
    lo = lax.bitcast_convert_type(lax.shift_left(u, jnp.uint32(16)), F32)
    hi = lax.bitcast_convert_type(u & jnp.uint32(0xFFFF0000), F32)
    return lo, hi


def _combine(h, y0_u32, y1_u32, rp):
    lo0, hi0 = _unpack_bf16_pairs(y0_u32)
    lo1, hi1 = _unpack_bf16_pairs(y1_u32)
    w0 = rp[:, 2:3]
    w1 = rp[:, 3:4]
    half = h.shape[1] // 2
    return jnp.concatenate([h[:, :half] + w0 * lo0 + w1 * lo1, h[:, half:] + w0 * hi0 + w1 * hi1], axis=1)


def _in_proj_body(h_ref, g_ref, w_ref, o_ref, xn_ref):
    @pl.when(pl.program_id(1) == 0)
    def _():
        h = h_ref[...]
        xn = h * lax.rsqrt(jnp.mean(h * h, axis=-1, keepdims=True) + NORM_EPS) * g_ref[...]
        xn_ref[...] = xn.astype(BF16)

    o_ref[...] = jnp.dot(xn_ref[...], w_ref[...], preferred_element_type=F32).astype(o_ref.dtype)


IN_PROJ_SUB_ROWS = 512


def _in_proj(h, gain, w, tm=1024, tn=2048):
    T, D = h.shape
    N = w.shape[1]
    tm = min(tm, T)
    tn = min(tn, N)
    return pl.pallas_call(
        _in_proj_body,
        grid=(T // tm, N // tn),
        in_specs=[pl.BlockSpec((tm, D), lambda i, j: (i, 0)), pl.BlockSpec((1, D), lambda i, j: (0, 0)),
                  pl.BlockSpec((D, tn), lambda i, j: (0, j))],
        out_specs=pl.BlockSpec((tm, tn), lambda i, j: (i, j)),
        out_shape=jax.ShapeDtypeStruct((T, N), BF16),
        scratch_shapes=[pltpu.VMEM((tm, D), BF16)],
        compiler_params=_cparams("parallel", "arbitrary"),
        name="in_proj",
    )(h, gain.reshape(1, D), w)


def _rope256(x, cos, sin_signed):
    xr = jnp.concatenate([pltpu.roll(x[:, :LANES], 64, 1), pltpu.roll(x[:, LANES:], 64, 1)], axis=1)
    return x * cos + xr * sin_signed


def _retention_body(dl_ref, q_ref, k_ref, v_ref, g_ref, cos_ref, sin_ref, y_ref, state_ref, obwd_ref,
                    *, n_chunks, n_blocks):
    C = RET_SCAN_CHUNK
    h = pl.program_id(1)
    phase = pl.program_id(2)
    n = pl.program_id(3)
    sb = n_chunks * C

    def log_gamma(s):
        x = jnp.full((1, 1), s, F32)
        return jnp.minimum(x, 0.0) - jnp.log(1.0 + jnp.exp(-jnp.abs(x)))

    lg_f = log_gamma(dl_ref[h])
    lg_b = log_gamma(dl_ref[RET_HEADS + h])
    pos = lax.broadcasted_iota(jnp.int32, (C, RET_QK_DIM), 0).astype(F32)
    k_scale = RET_QK_DIM ** -0.5

    @pl.when(n == 0)
    def _():
        state_ref[...] = jnp.zeros_like(state_ref)

    def load(rows, q_dec, k_dec):
        cos = cos_ref[rows, :]
        sin = sin_ref[rows, :]
        q = _rope256(q_ref[rows, :], cos, sin)
        k = _rope256(k_ref[rows, :], cos, sin)
        return q, k, v_ref[rows, :], q * q_dec, (k * k_dec).T

    @pl.when(phase == 0)
    def _():
        q_dec = jnp.exp((C - pos) * lg_b).astype(BF16)
        k_dec = (jnp.exp(pos * lg_b) * k_scale).astype(BF16)
        chunk_dec = jnp.exp(lg_b * C)
        base = (n_blocks - 1 - n) * sb

        def body(cc, carry):
            c = n_chunks - 1 - cc
            rows = pl.ds(pl.multiple_of(c * C, C), C)
            _, _, v, qd, kt = load(rows, q_dec, k_dec)
            st = state_ref[...]
            out = jnp.dot(qd, st.astype(BF16), preferred_element_type=F32)
            state_ref[...] = st * chunk_dec + jnp.dot(kt, v, preferred_element_type=F32)
            obwd_ref[pl.ds(pl.multiple_of(base + c * C, C), C), :] = out.astype(obwd_ref.dtype)
            return carry

        lax.fori_loop(0, n_chunks, body, 0, unroll=min(4, n_chunks))

    @pl.when(phase == 1)
    def _():
        q_dec = jnp.exp((pos + 1.0) * lg_f).astype(BF16)
        k_dec = (jnp.exp((C - 1.0 - pos) * lg_f) * k_scale).astype(BF16)
        chunk_dec = jnp.exp(lg_f * C)
        ii = lax.broadcasted_iota(jnp.int32, (C, C), 0)
        jj = lax.broadcasted_iota(jnp.int32, (C, C), 1)
        dist = (ii - jj).astype(F32)
        inner = (jnp.where(dist >= 0, jnp.exp(lg_f * jnp.maximum(dist, 0.0)), 0.0)
                 + jnp.where(dist <= 0, jnp.exp(lg_b * jnp.maximum(-dist, 0.0)), 0.0)) * k_scale
        base = n * sb

        def body(c, carry):
            rows = pl.ds(pl.multiple_of(c * C, C), C)
            q, k, v, qd, kt = load(rows, q_dec, k_dec)
            scores = lax.dot_general(q, k, (((1,), (1,)), ((), ())), preferred_element_type=F32) * inner
            st = state_ref[...]
            lhs = jnp.concatenate([qd, scores.astype(BF16)], axis=1)
            rhs = jnp.concatenate([st.astype(BF16), v], axis=0)
            o = jnp.dot(lhs, rhs, preferred_element_type=F32)
            state_ref[...] = st * chunk_dec + jnp.dot(kt, v, preferred_element_type=F32)
            o = o + obwd_ref[pl.ds(pl.multiple_of(base + c * C, C), C), :].astype(F32)
            o = o * lax.rsqrt(jnp.mean(o * o, axis=-1, keepdims=True) + NORM_EPS)
            g = g_ref[rows, :]
            y_ref[rows, :] = g * jax.nn.sigmoid(g) * o.astype(BF16)
            return carry

        lax.fori_loop(0, n_chunks, body, 0, unroll=min(4, n_chunks))


def _retention(proj, decay_logit, cos, sin_signed, B, S, sb=2048):
    sb = min(sb, S)
    nb = S // sb
    proj3 = proj.reshape(B, S, proj.shape[-1])
    blk = lambda p, n: jnp.where(p == 0, nb - 1 - n, n)
    kq = RET_QK_WIDTH // RET_QK_DIM
    kv = 2 * RET_QK_WIDTH // RET_V_DIM
    kg = kv + RET_HEADS
    in_specs = [
        pl.BlockSpec(memory_space=pltpu.SMEM),
        pl.BlockSpec((None, sb, RET_QK_DIM), lambda b, h, p, n: (b, blk(p, n), h)),
        pl.BlockSpec((None, sb, RET_QK_DIM), lambda b, h, p, n: (b, blk(p, n), kq + h)),
        pl.BlockSpec((None, sb, RET_V_DIM), lambda b, h, p, n: (b, blk(p, n), kv + h)),
        pl.BlockSpec((None, sb, RET_V_DIM), lambda b, h, p, n: (b, n * p, kg + h)),
        pl.BlockSpec((sb, RET_QK_DIM), lambda b, h, p, n: (blk(p, n), 0)),
        pl.BlockSpec((sb, RET_QK_DIM), lambda b, h, p, n: (blk(p, n), 0)),
    ]
    out = pl.pallas_call(
        functools.partial(_retention_body, n_chunks=sb // RET_SCAN_CHUNK, n_blocks=nb),
        grid=(B, RET_HEADS, 2, nb),
        in_specs=in_specs,
        out_specs=pl.BlockSpec((None, sb, RET_V_DIM), lambda b, h, p, n: (b, n * p, h)),
        out_shape=jax.ShapeDtypeStruct((B, S, RET_V_WIDTH), BF16),
        scratch_shapes=[pltpu.VMEM((RET_QK_DIM, RET_V_DIM), F32), pltpu.VMEM((S, RET_V_DIM), BF16)],
        compiler_params=_cparams("parallel", "parallel", "arbitrary", "arbitrary"),
        name="retention",
    )(decay_logit.reshape(2 * RET_HEADS).astype(F32), proj3, proj3, proj3, proj3,
      cos.astype(BF16), sin_signed.astype(BF16))
    return out.reshape(B * S, RET_V_WIDTH)


ATTN_HEAD_PERM = tuple(list(range(0, 32)) + list(range(64, 96)) + list(range(32, 64)) + list(range(96, 128)))


def _norm_rope_head_pair(x, gain2, cos2, sin2):
    d = ATTN_HEAD_DIM
    bi = lax.broadcasted_iota(I32, (2 * d, 2 * d), 0) // d
    bj = lax.broadcasted_iota(I32, (2 * d, 2 * d), 1) // d
    ones_bd = jnp.where(bi == bj, 1.0, 0.0).astype(BF16)
    ss = jnp.dot((x * x).astype(BF16), ones_bd, preferred_element_type=F32)
    x = x * lax.rsqrt(ss * (1.0 / d) + NORM_EPS) * gain2
    return _rope256(x, cos2, sin2)


def _attn_in_proj_body(h_ref, y0_ref, y1_ref, rp_ref, g_ref, w_ref, qg_ref, kg_ref, cos_ref, sin_ref,
                       hout_ref, q_ref, k_ref, v_ref):
    d = ATTN_HEAD_DIM
    tm = h_ref.shape[0]
    sub = min(IN_PROJ_SUB_ROWS, tm)
    q_scale = (d ** -0.5) * LOG2_E
    for s in range(tm // sub):
        rows = slice(s * sub, (s + 1) * sub)
        h = _combine(h_ref[rows, :], y0_ref[rows, :], y1_ref[rows, :], rp_ref[rows, :])
        hout_ref[rows, :] = h
        xn = (h * lax.rsqrt(jnp.mean(h * h, axis=-1, keepdims=True) + NORM_EPS) * g_ref[...]).astype(BF16)
        acc = jnp.dot(xn, w_ref[...], preferred_element_type=F32)
        cos2 = jnp.concatenate([cos_ref[rows, :]] * 2, axis=1)
        sin2 = jnp.concatenate([sin_ref[rows, :]] * 2, axis=1)
        qg2 = jnp.concatenate([qg_ref[...]] * 2, axis=1) * q_scale
        kg2 = jnp.concatenate([kg_ref[...]] * 2, axis=1)
        for hp in range(ATTN_Q_HEADS // 2):
            c = slice(2 * hp * d, (2 * hp + 2) * d)
            q = _norm_rope_head_pair(acc[:, c], qg2, cos2, sin2).astype(q_ref.dtype)
            q_ref[2 * hp, rows, :] = q[:, :d]
            q_ref[2 * hp + 1, rows, :] = q[:, d:]
        for hp in range(ATTN_KV_HEADS // 2):
            c0 = (ATTN_Q_HEADS + 2 * hp) * d
            k = _norm_rope_head_pair(acc[:, c0:c0 + 2 * d], kg2, cos2, sin2)
            k_ref[rows, 2 * hp * d:(2 * hp + 2) * d] = k.astype(k_ref.dtype)
        for hk in range(ATTN_KV_HEADS):
            c1 = (ATTN_Q_HEADS + ATTN_KV_HEADS + hk) * d
            v_ref[rows, 2 * hk * d:(2 * hk + 1) * d] = acc[:, c1:c1 + d].astype(v_ref.dtype)
            v_ref[rows, (2 * hk + 1) * d:(2 * hk + 2) * d] = jnp.ones((sub, d), v_ref.dtype)


def _attn_in_proj(h, gain, w, comb, q_gain, k_gain, cos, sin_signed, S, tm=1024):
    T, D = h.shape
    N = w.shape[1]
    tm = min(tm, S)
    spb = S // tm
    d = ATTN_HEAD_DIM
    y0, y1, rp = comb
    row = lambda i: (i, 0)
    const = lambda i: (0, 0)
    tab = lambda i: (i % spb, 0)
    return pl.pallas_call(
        _attn_in_proj_body,
        grid=(T // tm,),
        in_specs=[pl.BlockSpec((tm, D), row), pl.BlockSpec((tm, D // 2), row), pl.BlockSpec((tm, D // 2), row),
                  pl.BlockSpec((tm, LANES), row), pl.BlockSpec((1, D), const), pl.BlockSpec((D, N), const),
                  pl.BlockSpec((1, d), const), pl.BlockSpec((1, d), const),
                  pl.BlockSpec((tm, d), tab), pl.BlockSpec((tm, d), tab)],
        out_specs=[pl.BlockSpec((tm, D), row), pl.BlockSpec((ATTN_Q_HEADS, tm, d), lambda i: (0, i, 0)),
                   pl.BlockSpec((tm, ATTN_KV_HEADS * d), row), pl.BlockSpec((tm, 2 * ATTN_KV_HEADS * d), row)],
        out_shape=[jax.ShapeDtypeStruct((T, D), F32), jax.ShapeDtypeStruct((ATTN_Q_HEADS, T, d), BF16),
                   jax.ShapeDtypeStruct((T, ATTN_KV_HEADS * d), BF16),
                   jax.ShapeDtypeStruct((T, 2 * ATTN_KV_HEADS * d), BF16)],
        compiler_params=_cparams("parallel"),
        name="attn_in_proj",
    )(h, y0, y1, rp, gain.reshape(1, D), w, q_gain.reshape(1, d), k_gain.reshape(1, d), cos, sin_signed)


def _attn_body(q_ref, k_ref, v_ref, o_ref, m_ref, acc_ref, *, tq, tk, n_kv):
    d = ATTN_HEAD_DIM
    m_ref[...] = jnp.full_like(m_ref, -jnp.inf)
    acc_ref[...] = jnp.zeros_like(acc_ref)
    rep = tk // LANES

    def step(c, carry):
        rows = pl.ds(pl.multiple_of(c * tk, tk), tk)
        k = k_ref[rows, :]
        v = v_ref[rows, :]
        for g in range(ATTN_GROUP):
            r = slice(g * tq, (g + 1) * tq)
            s = lax.dot_general(q_ref[g], k, (((1,), (1,)), ((), ())), preferred_element_type=F32)
            m_prev = m_ref[r, :]
            m_new = jnp.maximum(m_prev, jnp.max(s, axis=-1, keepdims=True))
            alpha = jnp.exp2(m_prev - m_new)
            p = jnp.exp2(s - jnp.concatenate([m_new] * rep, axis=1))
            pv = jnp.dot(p.astype(BF16), v, preferred_element_type=F32)
            acc_ref[r, :] = jnp.concatenate([alpha, alpha], axis=1) * acc_ref[r, :] + pv
            m_ref[r, :] = m_new
        return carry

    lax.fori_loop(0, n_kv, step, 0, unroll=min(16, n_kv))
    for g in range(ATTN_GROUP):
        r = slice(g * tq, (g + 1) * tq)
        o_ref[:, g * d:(g + 1) * d] = (acc_ref[r, :d] / acc_ref[r, d:]).astype(o_ref.dtype)


def _attention(qn, kn, vx, B, S, tq=512, tk=256):
    d = ATTN_HEAD_DIM
    tq = min(tq, S)
    tk = min(tk, S)
    gw = ATTN_GROUP * d
    nq = S // tq
    k3 = kn.reshape(B, S, ATTN_KV_HEADS * d)
    v3 = vx.reshape(B, S, 2 * ATTN_KV_HEADS * d)
    m = ATTN_GROUP * tq
    out = pl.pallas_call(
        functools.partial(_attn_body, tq=tq, tk=tk, n_kv=S // tk),
        grid=(B, ATTN_KV_HEADS, S // tq),
        in_specs=[pl.BlockSpec((ATTN_GROUP, tq, d), lambda b, kh, i: (kh, b * nq + i, 0)),
                  pl.BlockSpec((None, S, d), lambda b, kh, i: (b, 0, kh)),
                  pl.BlockSpec((None, S, 2 * d), lambda b, kh, i: (b, 0, kh))],
        out_specs=pl.BlockSpec((None, tq, gw), lambda b, kh, i: (b, i, kh)),
        out_shape=jax.ShapeDtypeStruct((B, S, ATTN_Q_HEADS * d), BF16),
        scratch_shapes=[pltpu.VMEM((m, LANES), F32), pltpu.VMEM((m, 2 * d), F32)],
        compiler_params=_cparams("parallel", "parallel", "arbitrary"),
        name="attention",
    )(qn, k3, v3)
    return out.reshape(B * S, ATTN_Q_HEADS * d)


ROUTER_GROUP_ROW = MOE_EXPERTS
SUBLANES = 8
assert MOE_EXPERTS_PER_GROUP == SUBLANES and MOE_GROUPS <= SUBLANES


def _route(logits_t):
    tm = logits_t.shape[1]
    row = lax.broadcasted_iota(jnp.int32, (SUBLANES, tm), 0)
    neg = jnp.float32(-jnp.inf)
    big = jnp.int32(SUBLANES)

    def first_argmax(x):
        mx = jnp.max(x, axis=0, keepdims=True)
        return mx, jnp.min(jnp.where(x == mx, row, big), axis=0, keepdims=True)

    gl = jnp.where(row < MOE_GROUPS, logits_t[ROUTER_GROUP_ROW:ROUTER_GROUP_ROW + SUBLANES, :], neg)
    gmax, g_sel = first_argmax(gl)
    g_w = 1.0 / jnp.sum(jnp.exp(gl - gmax), axis=0, keepdims=True)
    el = logits_t[0:SUBLANES, :]
    for g in range(1, MOE_GROUPS):
        el = jnp.where(g_sel == g, logits_t[g * SUBLANES:(g + 1) * SUBLANES, :], el)
    v1, i1 = first_argmax(el)
    v2, i2 = first_argmax(jnp.where(row == i1, neg, el))
    e2 = jnp.exp(v2 - v1)
    w1 = g_w / (1.0 + e2)
    w2 = g_w * e2 / (1.0 + e2)
    base = g_sel * MOE_EXPERTS_PER_GROUP
    return jnp.where(row == 0, (base + i1).astype(F32),
                     jnp.where(row == 1, (base + i2).astype(F32),
                               jnp.where(row == 2, w1, jnp.where(row == 3, w2, 0.0))))


def _out_proj_body(y_ref, h_ref, w_ref, g_ref, wr_ref, br_ref, hout_ref, xn_ref, rp_ref, rpt_ref):
    tm = h_ref.shape[0]
    sub = min(OUT_PROJ_SUB_ROWS, tm)
    wr = wr_ref[...]
    for s in range(tm // sub):
        rows = slice(s * sub, (s + 1) * sub)
        h = h_ref[rows, :] + jnp.dot(y_ref[rows, :], w_ref[...], preferred_element_type=F32)
        hout_ref[rows, :] = h
        xn = h * lax.rsqrt(jnp.mean(h * h, axis=-1, keepdims=True) + NORM_EPS) * g_ref[...]
        xn_ref[rows, :] = _pack_bf16_pairs(xn)
        xn_hi = xn.astype(BF16)
        xn_lo = (xn - xn_hi.astype(F32)).astype(BF16)
        hi = jnp.dot(xn_hi, wr, preferred_element_type=F32)
        lo = jnp.dot(xn_lo, wr[:, :LANES], preferred_element_type=F32)
        logits = hi[:, :LANES] + hi[:, LANES:] + lo + br_ref[...]
        rpt = _route(logits.T)
        rpt_ref[:, rows] = rpt
        full = jnp.concatenate([rpt, jnp.zeros((LANES - SUBLANES, sub), F32)], axis=0)
        rp_ref[rows, :] = full.T


OUT_PROJ_SUB_ROWS = 512


def _out_proj(y, h, w, gain, w_router, b_router, tm=1024):
    T, K = y.shape
    D = h.shape[1]
    tm = min(tm, T)
    row = lambda i: (i, 0)
    const = lambda i: (0, 0)
    return pl.pallas_call(
        _out_proj_body,
        grid=(T // tm,),
        in_specs=[pl.BlockSpec((tm, K), row), pl.BlockSpec((tm, D), row), pl.BlockSpec((K, D), const),
                  pl.BlockSpec((1, D), const), pl.BlockSpec((D, 2 * LANES), const), pl.BlockSpec((1, LANES), const)],
        out_specs=[pl.BlockSpec((tm, D), row), pl.BlockSpec((tm, D // 2), row), pl.BlockSpec((tm, LANES), row),
                   pl.BlockSpec((SUBLANES, tm), lambda i: (0, i))],
        out_shape=[jax.ShapeDtypeStruct((T, D), F32), jax.ShapeDtypeStruct((T, D // 2), U32),
                   jax.ShapeDtypeStruct((T, LANES), F32), jax.ShapeDtypeStruct((SUBLANES, T), F32)],
        compiler_params=_cparams("parallel"),
        name="out_proj",
    )(y, h, w, gain.reshape(1, D), w_router, b_router)


MOE_STEP_BLOCKS = 2


def _moe_body(be_ref, bv_ref, x_ref, *refs):
    nb = MOE_STEP_BLOCKS
    wgu_refs, wd_refs = refs[:nb], refs[nb:2 * nb]
    y_ref = refs[2 * nb]
    wgu_bf_refs, wd_bf_refs = refs[2 * nb + 1:3 * nb + 1], refs[3 * nb + 1:]
    s = pl.program_id(0)

    for j in range(nb):
        b = s * nb + j

        @pl.when((s == 0) | (be_ref[b] != be_ref[jnp.maximum(b - nb, 0)]))
        def _():
            wgu_bf_refs[j][...] = wgu_refs[j][...].astype(BF16)
            wd_bf_refs[j][...] = wd_refs[j][...].astype(BF16)

    @pl.when(bv_ref[s * nb] != 0)
    def _():
        for j in range(nb):
            rows = slice(j * MOE_BLOCK, (j + 1) * MOE_BLOCK)
            lo, hi = _unpack_bf16_pairs(x_ref[rows, :])
            x = jnp.concatenate([lo.astype(BF16), hi.astype(BF16)], axis=1)
            gu = jnp.dot(x, wgu_bf_refs[j][...], preferred_element_type=F32)
            gate = gu[:, :MOE_FF]
            up = gu[:, MOE_FF:]
            act = (gate * jax.nn.sigmoid(gate) * up).astype(BF16)
            y_ref[rows, :] = _pack_bf16_pairs(jnp.dot(act, wd_bf_refs[j][...], preferred_element_type=F32))

    @pl.when(bv_ref[s * nb] == 0)
    def _():
        y_ref[...] = jnp.zeros_like(y_ref)


def _moe_experts(xs, block_e, block_valid, w_gate_up, w_down, layer):
    P, W = xs.shape
    D = 2 * W
    nb = MOE_STEP_BLOCKS
    n_steps = P // (nb * MOE_BLOCK)
    rows = lambda s, be, bv: (s, 0)
    wspec = lambda shape, j: pl.BlockSpec((None, None) + shape, lambda s, be, bv: (layer, be[s * nb + j], 0, 0))
    grid_spec = pltpu.PrefetchScalarGridSpec(
        num_scalar_prefetch=2,
        grid=(n_steps,),
        in_specs=([pl.BlockSpec((nb * MOE_BLOCK, W), rows)]
                  + [wspec((D, 2 * MOE_FF), j) for j in range(nb)] + [wspec((MOE_FF, D), j) for j in range(nb)]),
        out_specs=pl.BlockSpec((nb * MOE_BLOCK, W), rows),
        scratch_shapes=([pltpu.VMEM((D, 2 * MOE_FF), BF16)] * nb + [pltpu.VMEM((MOE_FF, D), BF16)] * nb),
    )
    return pl.pallas_call(
        _moe_body,
        grid_spec=grid_spec,
        out_shape=jax.ShapeDtypeStruct((P, W), U32),
        compiler_params=_cparams("arbitrary"),
        name="moe_experts",
    )(block_e, block_valid, xs, *([w_gate_up] * nb), *([w_down] * nb))


def _rank_body(rpt_ref, tri_ref, rank_ref, counts_ref, carry_ref, *, n_groups):
    @pl.when(pl.program_id(0) == 0)
    def _():
        carry_ref[...] = jnp.zeros_like(carry_ref)

    erow = lax.broadcasted_iota(I32, (MOE_EXPERTS, LANES), 0)
    carry = carry_ref[...]
    tri = tri_ref[...]
    r0, r1 = [], []
    for g in range(n_groups):
        c = slice(g * LANES, (g + 1) * LANES)
        oh0 = erow == rpt_ref[0:1, c].astype(I32)
        oh1 = erow == rpt_ref[1:2, c].astype(I32)
        both = jnp.where(oh0 | oh1, 1.0, 0.0)
        before = jnp.dot(both.astype(BF16), tri, preferred_element_type=F32) + carry
        r0.append(jnp.sum(jnp.where(oh0, before, 0.0), axis=0, keepdims=True))
        r1.append(jnp.sum(jnp.where(oh1, before, 0.0), axis=0, keepdims=True))
        carry = carry + jnp.sum(both, axis=1, keepdims=True)
    carry_ref[...] = carry
    counts_ref[...] = carry.astype(I32)
    tm = n_groups * LANES
    rank = jnp.concatenate([jnp.concatenate(r0, axis=1), jnp.concatenate(r1, axis=1),
                            jnp.zeros((SUBLANES - MOE_TOPK, tm), F32)], axis=0)
    rank_ref[...] = rank.astype(I32)


def _slot_ranks(rpt, tm=2048):
    T = rpt.shape[1]
    tm = min(tm, T)
    tri = (lax.broadcasted_iota(I32, (LANES, LANES), 0) < lax.broadcasted_iota(I32, (LANES, LANES), 1)).astype(BF16)
    return pl.pallas_call(
        functools.partial(_rank_body, n_groups=tm // LANES),
        grid=(T // tm,),
        in_specs=[pl.BlockSpec((SUBLANES, tm), lambda i: (0, i)), pl.BlockSpec((LANES, LANES), lambda i: (0, 0))],
        out_specs=[pl.BlockSpec((SUBLANES, tm), lambda i: (0, i)), pl.BlockSpec((MOE_EXPERTS, LANES), lambda i: (0, 0))],
        out_shape=[jax.ShapeDtypeStruct((SUBLANES, T), I32), jax.ShapeDtypeStruct((MOE_EXPERTS, LANES), I32)],
        scratch_shapes=[pltpu.VMEM((MOE_EXPERTS, LANES), F32)],
        compiler_params=_cparams("arbitrary"),
        name="slot_ranks",
    )(rpt, tri)


def _moe_dispatch_plan(rpt):
    T = rpt.shape[1]
    A = T * MOE_TOPK
    n_blocks = -(-(A + MOE_EXPERTS * (MOE_BLOCK - 1)) // MOE_BLOCK)
    n_blocks = -(-n_blocks // MOE_STEP_BLOCKS) * MOE_STEP_BLOCKS
    rank, counts = _slot_ranks(rpt)
    counts = counts[:, 0]
    padded = ((counts + MOE_BLOCK - 1) // MOE_BLOCK) * MOE_BLOCK
    pends = jnp.cumsum(padded)
    pstarts = pends - padded
    e = rpt[:MOE_TOPK].astype(I32)
    experts = jnp.arange(MOE_EXPERTS, dtype=I32)[:, None, None]
    dest = jnp.sum(jnp.where(e[None] == experts, pstarts[:, None, None], 0), axis=0) + rank[:MOE_TOPK]
    block_start = jnp.arange(n_blocks, dtype=I32) * MOE_BLOCK
    block_e = jnp.minimum(jnp.sum(pends[None, :] <= block_start[:, None], axis=1), MOE_EXPERTS - 1).astype(I32)
    block_valid = (block_start < pends[-1]).astype(I32)
    return dest, block_e, block_valid, n_blocks * MOE_BLOCK


SC_CORES = 2
SC_SUBCORES = 16
SC_WORKERS = SC_CORES * SC_SUBCORES
SC_ROWS = 64


def _sc_mesh():
    return plsc.VectorSubcoreMesh(core_axis_name="c", subcore_axis_name="s")


def _sc_scratch(n_chunks, width):
    return [pltpu.VMEM((n_chunks, SC_ROWS), I32), pltpu.VMEM((n_chunks, SC_ROWS), I32),
            pltpu.VMEM((SC_ROWS, width), U32), pltpu.VMEM((SC_ROWS, width), U32),
            pltpu.SemaphoreType.DMA, pltpu.SemaphoreType.DMA, pltpu.SemaphoreType.DMA, pltpu.SemaphoreType.DMA]


def _sc_scatter_rows(x, dest, P):
    T, W = x.shape
    tw = T // SC_WORKERS
    nch = tw // SC_ROWS
    assert tw * SC_WORKERS == T and nch * SC_ROWS == tw and nch % 2 == 0
    idx = dest.reshape(MOE_TOPK, SC_WORKERS, nch, SC_ROWS)

    @functools.partial(pl.kernel, mesh=_sc_mesh(), out_type=jax.ShapeDtypeStruct((P, W), U32),
                       scratch_types=_sc_scratch(nch, W))
    def scatter(x_hbm, i0_hbm, i1_hbm, o_hbm, i0_v, i1_v, b0, b1, r0, r1, s0, s1):
        wid = lax.axis_index("s") * SC_CORES + lax.axis_index("c")
        base = wid * tw
        pltpu.sync_copy(i0_hbm.at[wid], i0_v)
        pltpu.sync_copy(i1_hbm.at[wid], i1_v)

        @pl.loop(0, nch, step=2)
        def _(j):
            ca = pltpu.async_copy(x_hbm.at[pl.ds(base + j * SC_ROWS, SC_ROWS)], b0, r0)
            cb = pltpu.async_copy(x_hbm.at[pl.ds(base + (j + 1) * SC_ROWS, SC_ROWS)], b1, r1)
            ca.wait()
            a0 = pltpu.async_copy(b0, o_hbm.at[i0_v.at[j]], s0)
            a1 = pltpu.async_copy(b0, o_hbm.at[i1_v.at[j]], s0)
            cb.wait()
            e0 = pltpu.async_copy(b1, o_hbm.at[i0_v.at[j + 1]], s1)
            e1 = pltpu.async_copy(b1, o_hbm.at[i1_v.at[j + 1]], s1)
            a0.wait()
            a1.wait()
            e0.wait()
            e1.wait()

    return scatter(x, idx[0], idx[1])


def _sc_gather_rows(table, dest):
    T = dest.shape[1]
    W = table.shape[1]
    tw = T // SC_WORKERS
    nch = tw // SC_ROWS
    assert tw * SC_WORKERS == T and nch * SC_ROWS == tw
    idx = dest.reshape(MOE_TOPK, SC_WORKERS, nch, SC_ROWS)
    out = jax.ShapeDtypeStruct((T, W), U32)

    @functools.partial(pl.kernel, mesh=_sc_mesh(), out_type=[out, out], scratch_types=_sc_scratch(nch, W))
    def gather(tab_hbm, i0_hbm, i1_hbm, o0_hbm, o1_hbm, i0_v, i1_v, b0, b1, g0, g1, w0, w1):
        wid = lax.axis_index("s") * SC_CORES + lax.axis_index("c")
        base = wid * tw
        pltpu.sync_copy(i0_hbm.at[wid], i0_v)
        pltpu.sync_copy(i1_hbm.at[wid], i1_v)

        @pl.loop(0, nch)
        def _(j):
            rows = pl.ds(base + j * SC_ROWS, SC_ROWS)
            c0 = pltpu.async_copy(tab_hbm.at[i0_v.at[j]], b0, g0)
            c1 = pltpu.async_copy(tab_hbm.at[i1_v.at[j]], b1, g1)
            c0.wait()
            d0 = pltpu.async_copy(b0, o0_hbm.at[rows], w0)
            c1.wait()
            d1 = pltpu.async_copy(b1, o1_hbm.at[rows], w1)
            d0.wait()
            d1.wait()

    return gather(table, idx[0], idx[1])


def _moe(xn, rpt, w_gate_up, w_down, layer):
    dest, block_e, block_valid, P = _moe_dispatch_plan(rpt)
    xs = _sc_scatter_rows(xn, dest, P)
    y = _moe_experts(xs, block_e, block_valid, w_gate_up, w_down, layer)
    return _sc_gather_rows(y, dest)


def _final_body(h_ref, y0_ref, y1_ref, rp_ref, g_ref, o_ref):
    h = _combine(h_ref[...], y0_ref[...], y1_ref[...], rp_ref[...])
    o_ref[...] = h * lax.rsqrt(jnp.mean(h * h, axis=-1, keepdims=True) + NORM_EPS) * g_ref[...]


def _final(h, y0, y1, rp, gain, tm=1024):
    T, D = h.shape
    tm = min(tm, T)
    row = lambda i: (i, 0)
    return pl.pallas_call(
        _final_body,
        grid=(T // tm,),
        in_specs=[pl.BlockSpec((tm, D), row), pl.BlockSpec((tm, D // 2), row), pl.BlockSpec((tm, D // 2), row),
                  pl.BlockSpec((tm, LANES), row), pl.BlockSpec((1, D), lambda i: (0, 0))],
        out_specs=pl.BlockSpec((tm, D), row),
        out_shape=jax.ShapeDtypeStruct((T, D), F32),
        compiler_params=_cparams("parallel"),
        name="final_norm",
    )(h, y0, y1, rp, gain.reshape(1, D))


def _rope_tables(n_tok, head_dim):
    t = jnp.arange(n_tok, dtype=jnp.int32)
    row = (t // GRID_W).astype(F32)
    col = (t % GRID_W).astype(F32)
    half = head_dim // 2
    inv = ROPE_THETA ** (-jnp.arange(0, half, 2, dtype=F32) / half)
    ang_r = row[:, None] * inv[None, :]
    ang_c = col[:, None] * inv[None, :]
    ang = jnp.concatenate([ang_r, ang_r, ang_c, ang_c], axis=-1)
    q = half // 2
    sign = jnp.where((jnp.arange(head_dim) % half) < q, -1.0, 1.0).astype(F32)
    return jnp.cos(ang), jnp.sin(ang) * sign[None, :]


def _router_weights(w_group, b_group, w_expert, b_expert):
    D = w_group.shape[0]
    pad = LANES - MOE_EXPERTS - MOE_GROUPS
    w = jnp.concatenate([w_expert, w_group, jnp.zeros((D, pad), F32)], axis=1).astype(F32)
    b = jnp.concatenate([b_expert, b_group, jnp.zeros((pad,), F32)]).reshape(1, LANES)
    w_hi = w.astype(BF16)
    w_lo = (w - w_hi.astype(F32)).astype(BF16)
    return jnp.concatenate([w_hi, w_lo], axis=1), b.astype(F32)


def _trunk(x, p):
    B, S, D = x.shape
    T = B * S
    cos_r, sin_r = _rope_tables(S, RET_QK_DIM)
    cos_a, sin_a = [jnp.take(t, jnp.asarray(ATTN_HEAD_PERM, I32), axis=1) for t in _rope_tables(S, ATTN_HEAD_DIM)]
    h = x.reshape(T, D)

    proj = _in_proj(h, p["ln_mix"][0], p["ret_w_in"][0])
    y = _retention(proj, p["ret_decay_logit"][0], cos_r, sin_r, B, S)
    h, xn, rp, rpt = _out_proj(y, h, p["ret_w_out"][0], p["ln_ffn"][0], *p["router"][0])
    y0, y1 = _moe(xn, rpt, p["moe_w_gate_up"], p["moe_w_down"], 0)

    h, qn, kn, vx = _attn_in_proj(h, p["ln_mix"][1], p["attn_w_in"][0], (y0, y1, rp),
                                  p["attn_q_gain"][0], p["attn_k_gain"][0], cos_a, sin_a, S)
    y = _attention(qn, kn, vx, B, S)
    h, xn, rp, rpt = _out_proj(y, h, p["attn_w_out"][0], p["ln_ffn"][1], *p["router"][1])
    y0, y1 = _moe(xn, rpt, p["moe_w_gate_up"], p["moe_w_down"], 1)

    return _final(h, y0, y1, rp, p["ln_final"]).reshape(B, S, D)


def _prepare_params(ln_mix, ret_w_in, ret_decay_logit, ret_w_out, attn_w_in, attn_q_gain, attn_k_gain,
                    attn_w_out, ln_ffn, moe_w_group, moe_b_group, moe_w_expert, moe_b_expert, moe_w_gate_up,
                    moe_w_down, ln_final):
    d = ATTN_HEAD_DIM
    perm = jnp.asarray(ATTN_HEAD_PERM, I32)
    qk_cols = (jnp.arange((ATTN_Q_HEADS + ATTN_KV_HEADS) * d, dtype=I32) // d) * d
    qk_cols = qk_cols + jnp.tile(perm, ATTN_Q_HEADS + ATTN_KV_HEADS)
    cols = jnp.concatenate([qk_cols, jnp.arange(qk_cols.shape[0], ATTN_IN_WIDTH, dtype=I32)])
    return {
        "ln_mix": ln_mix, "ln_ffn": ln_ffn, "ln_final": ln_final,
        "ret_w_in": ret_w_in.astype(BF16), "ret_decay_logit": ret_decay_logit, "ret_w_out": ret_w_out.astype(BF16),
        "attn_w_in": jnp.take(attn_w_in, cols, axis=2).astype(BF16),
        "attn_q_gain": jnp.take(attn_q_gain, perm, axis=1), "attn_k_gain": jnp.take(attn_k_gain, perm, axis=1),
        "attn_w_out": attn_w_out.astype(BF16),
        "router": [_router_weights(moe_w_group[i], moe_b_group[i], moe_w_expert[i], moe_b_expert[i])
                   for i in range(moe_w_group.shape[0])],
        "moe_w_gate_up": moe_w_gate_up, "moe_w_down": moe_w_down,
    }


def kernel(x_prompt, x_sample, ln_mix, ret_w_in, ret_decay_logit, ret_w_out, attn_w_in, attn_q_gain, attn_k_gain,
           attn_w_out, ln_ffn, moe_w_group, moe_b_group, moe_w_expert, moe_b_expert, moe_w_gate_up, moe_w_down,
           ln_final):
    p = _prepare_params(ln_mix, ret_w_in, ret_decay_logit, ret_w_out, attn_w_in, attn_q_gain, attn_k_gain,
                        attn_w_out, ln_ffn, moe_w_group, moe_b_group, moe_w_expert, moe_b_expert, moe_w_gate_up,
                        moe_w_down, ln_final)
    return _trunk(x_prompt, p), _trunk(x_sample, p)
```

```python
import functools

import jax
import jax.numpy as jnp
from jax import lax
from jax.experimental import pallas as pl
from jax.experimental.pallas import tpu as pltpu
from jax.experimental.pallas import tpu_sc as plsc

F32 = jnp.float32
BF16 = jnp.bfloat16
U32 = jnp.uint32
I32 = jnp.int32

D_MODEL = 1024
GRID_W = 64
ROPE_THETA = 10000.0
NORM_EPS = 1e-6

RET_HEADS = 4
RET_QK_DIM = 256
RET_V_DIM = 512
RET_SCAN_CHUNK = 256
RET_QK_WIDTH = RET_HEADS * RET_QK_DIM
RET_V_WIDTH = RET_HEADS * RET_V_DIM

ATTN_Q_HEADS = 8
ATTN_KV_HEADS = 2
ATTN_HEAD_DIM = 128
ATTN_GROUP = ATTN_Q_HEADS // ATTN_KV_HEADS
ATTN_IN_WIDTH = (ATTN_Q_HEADS + 2 * ATTN_KV_HEADS) * ATTN_HEAD_DIM

MOE_GROUPS = 4
MOE_EXPERTS_PER_GROUP = 8
MOE_EXPERTS = MOE_GROUPS * MOE_EXPERTS_PER_GROUP
MOE_TOPK = 2
MOE_FF = 512
MOE_BLOCK = 512

LANES = 128
LOG2_E = 1.4426950408889634
VMEM_LIMIT = 56 * 1024 * 1024


def _cparams(*sem):
    return pltpu.CompilerParams(dimension_semantics=sem, vmem_limit_bytes=VMEM_LIMIT)


def _pack_bf16_pairs(x):
    w = x.shape[1] // 2
    lo = lax.bitcast_convert_type(x[:, :w].astype(BF16).astype(F32), U32)
    hi = lax.bitcast_convert_type(x[:, w:].astype(BF16).astype(F32), U32)
    return lax.shift_right_logical(lo, jnp.uint32(16)) | (hi & jnp.uint32(0xFFFF0000))


def _unpack_bf16_pairs(u):
    lo = lax.bitcast_convert_type(lax.shift_left(u, jnp.uint32(16)), F32)
    hi = lax.bitcast_convert_type(u & jnp.uint32(0xFFFF0000), F32)
    return lo, hi


def _combine(h, y0_u32, y1_u32, rp):
    lo0, hi0 = _unpack_bf16_pairs(y0_u32)
    lo1, hi1 = _unpack_bf16_pairs(y1_u32)
    w0 = rp[:, 2:3]
    w1 = rp[:, 3:4]
    half = h.shape[1] // 2
    return jnp.concatenate([h[:, :half] + w0 * lo0 + w1 * lo1, h[:, half:] + w0 * hi0 + w1 * hi1], axis=1)


IN_PROJ_SUB_ROWS = 256


def _in_proj_body(h_ref, g_ref, w_ref, o_ref):
    tm = h_ref.shape[0]
    sub = min(IN_PROJ_SUB_ROWS, tm)
    for s in range(tm // sub):
        rows = slice(s * sub, (s + 1) * sub)
        h = h_ref[rows, :]
        xn = (h * lax.rsqrt(jnp.mean(h * h, axis=-1, keepdims=True) + NORM_EPS) * g_ref[...]).astype(BF16)
        o_ref[rows, :] = jnp.dot(xn, w_ref[...], preferred_element_type=F32).astype(o_ref.dtype)


def _in_proj(h, gain, w, tm=512):
    T, D = h.shape
    N = w.shape[1]
    tm = min(tm, T)
    return pl.pallas_call(
        _in_proj_body,
        grid=(T // tm,),
        in_specs=[pl.BlockSpec((tm, D), lambda i: (i, 0)), pl.BlockSpec((1, D), lambda i: (0, 0)),
                  pl.BlockSpec((D, N), lambda i: (0, 0))],
        out_specs=pl.BlockSpec((tm, N), lambda i: (i, 0)),
        out_shape=jax.ShapeDtypeStruct((T, N), BF16),
        compiler_params=_cparams("parallel"),
        name="in_proj",
    )(h, gain.reshape(1, D), w)


def _rope256(x, cos, sin_signed):
    xr = jnp.concatenate([pltpu.roll(x[:, :LANES], 64, 1), pltpu.roll(x[:, LANES:], 64, 1)], axis=1)
    return x * cos + xr * sin_signed


def _retention_body(dl_ref, q_ref, k_ref, v_ref, g_ref, cos_ref, sin_ref, y_ref, state_ref, obwd_ref,
                    *, n_chunks, n_blocks):
    C = RET_SCAN_CHUNK
    h = pl.program_id(1)
    phase = pl.program_id(2)
    n = pl.program_id(3)
    sb = n_chunks * C

    def log_gamma(s):
        x = jnp.full((1, 1), s, F32)
        return jnp.minimum(x, 0.0) - jnp.log(1.0 + jnp.exp(-jnp.abs(x)))

    lg_f = log_gamma(dl_ref[h])
    lg_b = log_gamma(dl_ref[RET_HEADS + h])
    pos = lax.broadcasted_iota(jnp.int32, (C, RET_QK_DIM), 0).astype(F32)
    k_scale = RET_QK_DIM ** -0.5

    @pl.when(n == 0)
    def _():
        state_ref[...] = jnp.zeros_like(state_ref)

    def load(rows, q_dec, k_dec):
        cos = cos_ref[rows, :]
        sin = sin_ref[rows, :]
        q = _rope256(q_ref[rows, :], cos, sin)
        k = _rope256(k_ref[rows, :], cos, sin)
        return q, k, v_ref[rows, :], q * q_dec, (k * k_dec).T

    @pl.when(phase == 0)
    def _():
        q_dec = jnp.exp((C - pos) * lg_b).astype(BF16)
        k_dec = (jnp.exp(pos * lg_b) * k_scale).astype(BF16)
        chunk_dec = jnp.exp(lg_b * C)
        base = (n_blocks - 1 - n) * sb

        def body(cc, carry):
            c = n_chunks - 1 - cc
            rows = pl.ds(pl.multiple_of(c * C, C), C)
            _, _, v, qd, kt = load(rows, q_dec, k_dec)
            st = state_ref[...]
            out = jnp.dot(qd, st.astype(BF16), preferred_element_type=F32)
            state_ref[...] = st * chunk_dec + jnp.dot(kt, v, preferred_element_type=F32)
            obwd_ref[pl.ds(pl.multiple_of(base + c * C, C), C), :] = out.astype(obwd_ref.dtype)
            return carry

        lax.fori_loop(0, n_chunks, body, 0, unroll=min(4, n_chunks))

    @pl.when(phase == 1)
    def _():
        q_dec = jnp.exp((pos + 1.0) * lg_f).astype(BF16)
        k_dec = (jnp.exp((C - 1.0 - pos) * lg_f) * k_scale).astype(BF16)
        chunk_dec = jnp.exp(lg_f * C)
        ii = lax.broadcasted_iota(jnp.int32, (C, C), 0)
        jj = lax.broadcasted_iota(jnp.int32, (C, C), 1)
        dist = (ii - jj).astype(F32)
        inner = (jnp.where(dist >= 0, jnp.exp(lg_f * jnp.maximum(dist, 0.0)), 0.0)
                 + jnp.where(dist <= 0, jnp.exp(lg_b * jnp.maximum(-dist, 0.0)), 0.0)) * k_scale
        base = n * sb

        def body(c, carry):
            rows = pl.ds(pl.multiple_of(c * C, C), C)
            q, k, v, qd, kt = load(rows, q_dec, k_dec)
            scores = lax.dot_general(q, k, (((1,), (1,)), ((), ())), preferred_element_type=F32) * inner
            st = state_ref[...]
            lhs = jnp.concatenate([qd, scores.astype(BF16)], axis=1)
            rhs = jnp.concatenate([st.astype(BF16), v], axis=0)
            o = jnp.dot(lhs, rhs, preferred_element_type=F32)
            state_ref[...] = st * chunk_dec + jnp.dot(kt, v, preferred_element_type=F32)
            o = o + obwd_ref[pl.ds(pl.multiple_of(base + c * C, C), C), :].astype(F32)
            o = o * lax.rsqrt(jnp.mean(o * o, axis=-1, keepdims=True) + NORM_EPS)
            g = g_ref[rows, :]
            y_ref[rows, :] = g * jax.nn.sigmoid(g) * o.astype(BF16)
            return carry

        lax.fori_loop(0, n_chunks, body, 0, unroll=min(4, n_chunks))


def _retention(proj, decay_logit, cos, sin_signed, B, S, sb=4096):
    sb = min(sb, S)
    nb = S // sb
    proj3 = proj.reshape(B, S, proj.shape[-1])
    blk = lambda p, n: jnp.where(p == 0, nb - 1 - n, n)
    kq = RET_QK_WIDTH // RET_QK_DIM
    kv = 2 * RET_QK_WIDTH // RET_V_DIM
    kg = kv + RET_HEADS
    in_specs = [
        pl.BlockSpec(memory_space=pltpu.SMEM),
        pl.BlockSpec((None, sb, RET_QK_DIM), lambda b, h, p, n: (b, blk(p, n), h)),
        pl.BlockSpec((None, sb, RET_QK_DIM), lambda b, h, p, n: (b, blk(p, n), kq + h)),
        pl.BlockSpec((None, sb, RET_V_DIM), lambda b, h, p, n: (b, blk(p, n), kv + h)),
        pl.BlockSpec((None, sb, RET_V_DIM), lambda b, h, p, n: (b, n * p, kg + h)),
        pl.BlockSpec((sb, RET_QK_DIM), lambda b, h, p, n: (blk(p, n), 0)),
        pl.BlockSpec((sb, RET_QK_DIM), lambda b, h, p, n: (blk(p, n), 0)),
    ]
    out = pl.pallas_call(
        functools.partial(_retention_body, n_chunks=sb // RET_SCAN_CHUNK, n_blocks=nb),
        grid=(B, RET_HEADS, 2, nb),
        in_specs=in_specs,
        out_specs=pl.BlockSpec((None, sb, RET_V_DIM), lambda b, h, p, n: (b, n * p, h)),
        out_shape=jax.ShapeDtypeStruct((B, S, RET_V_WIDTH), BF16),
        scratch_shapes=[pltpu.VMEM((RET_QK_DIM, RET_V_DIM), F32), pltpu.VMEM((S, RET_V_DIM), BF16)],
        compiler_params=_cparams("parallel", "parallel", "arbitrary", "arbitrary"),
        name="retention",
    )(decay_logit.reshape(2 * RET_HEADS).astype(F32), proj3, proj3, proj3, proj3,
      cos.astype(BF16), sin_signed.astype(BF16))
    return out.reshape(B * S, RET_V_WIDTH)


ATTN_HEAD_PERM = tuple(list(range(0, 32)) + list(range(64, 96)) + list(range(32, 64)) + list(range(96, 128)))


def _norm_rope_head_pair(x, gain2, cos2, sin2):
    d = ATTN_HEAD_DIM
    bi = lax.broadcasted_iota(I32, (2 * d, 2 * d), 0) // d
    bj = lax.broadcasted_iota(I32, (2 * d, 2 * d), 1) // d
    ones_bd = jnp.where(bi == bj, 1.0, 0.0).astype(BF16)
    ss = jnp.dot((x * x).astype(BF16), ones_bd, preferred_element_type=F32)
    x = x * lax.rsqrt(ss * (1.0 / d) + NORM_EPS) * gain2
    return _rope256(x, cos2, sin2)


def _attn_in_proj_body(h_ref, y0_ref, y1_ref, rp_ref, g_ref, w_ref, qg_ref, kg_ref, cos_ref, sin_ref,
                       hout_ref, q_ref, k_ref, v_ref):
    d = ATTN_HEAD_DIM
    tm = h_ref.shape[0]
    sub = min(IN_PROJ_SUB_ROWS, tm)
    q_scale = (d ** -0.5) * LOG2_E
    for s in range(tm // sub):
        rows = slice(s * sub, (s + 1) * sub)
        h = _combine(h_ref[rows, :], y0_ref[rows, :], y1_ref[rows, :], rp_ref[rows, :])
        hout_ref[rows, :] = h
        xn = (h * lax.rsqrt(jnp.mean(h * h, axis=-1, keepdims=True) + NORM_EPS) * g_ref[...]).astype(BF16)
        acc = jnp.dot(xn, w_ref[...], preferred_element_type=F32)
        cos2 = jnp.concatenate([cos_ref[rows, :]] * 2, axis=1)
        sin2 = jnp.concatenate([sin_ref[rows, :]] * 2, axis=1)
        qg2 = jnp.concatenate([qg_ref[...]] * 2, axis=1) * q_scale
        kg2 = jnp.concatenate([kg_ref[...]] * 2, axis=1)
        for hp in range(ATTN_Q_HEADS // 2):
            c = slice(2 * hp * d, (2 * hp + 2) * d)
            q = _norm_rope_head_pair(acc[:, c], qg2, cos2, sin2).astype(q_ref.dtype)
            q_ref[2 * hp, rows, :] = q[:, :d]
            q_ref[2 * hp + 1, rows, :] = q[:, d:]
        for hp in range(ATTN_KV_HEADS // 2):
            c0 = (ATTN_Q_HEADS + 2 * hp) * d
            k = _norm_rope_head_pair(acc[:, c0:c0 + 2 * d], kg2, cos2, sin2)
            k_ref[rows, 2 * hp * d:(2 * hp + 2) * d] = k.astype(k_ref.dtype)
        for hk in range(ATTN_KV_HEADS):
            c1 = (ATTN_Q_HEADS + ATTN_KV_HEADS + hk) * d
            v_ref[rows, 2 * hk * d:(2 * hk + 1) * d] = acc[:, c1:c1 + d].astype(v_ref.dtype)
            v_ref[rows, (2 * hk + 1) * d:(2 * hk + 2) * d] = jnp.ones((sub, d), v_ref.dtype)


def _attn_in_proj(h, gain, w, comb, q_gain, k_gain, cos, sin_signed, S, tm=1024):
    T, D = h.shape
    N = w.shape[1]
    tm = min(tm, S)
    spb = S // tm
    d = ATTN_HEAD_DIM
    y0, y1, rp = comb
    row = lambda i: (i, 0)
    const = lambda i: (0, 0)
    tab = lambda i: (i % spb, 0)
    return pl.pallas_call(
        _attn_in_proj_body,
        grid=(T // tm,),
        in_specs=[pl.BlockSpec((tm, D), row), pl.BlockSpec((tm, D // 2), row), pl.BlockSpec((tm, D // 2), row),
                  pl.BlockSpec((tm, LANES), row), pl.BlockSpec((1, D), const), pl.BlockSpec((D, N), const),
                  pl.BlockSpec((1, d), const), pl.BlockSpec((1, d), const),
                  pl.BlockSpec((tm, d), tab), pl.BlockSpec((tm, d), tab)],
        out_specs=[pl.BlockSpec((tm, D), row), pl.BlockSpec((ATTN_Q_HEADS, tm, d), lambda i: (0, i, 0)),
                   pl.BlockSpec((tm, ATTN_KV_HEADS * d), row), pl.BlockSpec((tm, 2 * ATTN_KV_HEADS * d), row)],
        out_shape=[jax.ShapeDtypeStruct((T, D), F32), jax.ShapeDtypeStruct((ATTN_Q_HEADS, T, d), BF16),
                   jax.ShapeDtypeStruct((T, ATTN_KV_HEADS * d), BF16),
                   jax.ShapeDtypeStruct((T, 2 * ATTN_KV_HEADS * d), BF16)],
        compiler_params=_cparams("parallel"),
        name="attn_in_proj",
    )(h, y0, y1, rp, gain.reshape(1, D), w, q_gain.reshape(1, d), k_gain.reshape(1, d), cos, sin_signed)


def _attn_body(q_ref, k_ref, v_ref, o_ref, m_ref, acc_ref, *, tq, tk, n_kv):
    d = ATTN_HEAD_DIM
    m_ref[...] = jnp.full_like(m_ref, -jnp.inf)
    acc_ref[...] = jnp.zeros_like(acc_ref)
    rep = tk // LANES

    def step(c, carry):
        rows = pl.ds(pl.multiple_of(c * tk, tk), tk)
        k = k_ref[rows, :]
        v = v_ref[rows, :]
        for g in range(ATTN_GROUP):
            r = slice(g * tq, (g + 1) * tq)
            s = lax.dot_general(q_ref[g], k, (((1,), (1,)), ((), ())), preferred_element_type=F32)
            m_prev = m_ref[r, :]
            m_new = jnp.maximum(m_prev, jnp.max(s, axis=-1, keepdims=True))
            alpha = jnp.exp2(m_prev - m_new)
            p = jnp.exp2(s - jnp.concatenate([m_new] * rep, axis=1))
            pv = jnp.dot(p.astype(BF16), v, preferred_element_type=F32)
            acc_ref[r, :] = jnp.concatenate([alpha, alpha], axis=1) * acc_ref[r, :] + pv
            m_ref[r, :] = m_new
        return carry

    lax.fori_loop(0, n_kv, step, 0, unroll=min(16, n_kv))
    for g in range(ATTN_GROUP):
        r = slice(g * tq, (g + 1) * tq)
        o_ref[:, g * d:(g + 1) * d] = (acc_ref[r, :d] / acc_ref[r, d:]).astype(o_ref.dtype)


def _attention(qn, kn, vx, B, S, tq=1024, tk=256):
    d = ATTN_HEAD_DIM
    tq = min(tq, S)
    tk = min(tk, S)
    gw = ATTN_GROUP * d
    nq = S // tq
    k3 = kn.reshape(B, S, ATTN_KV_HEADS * d)
    v3 = vx.reshape(B, S, 2 * ATTN_KV_HEADS * d)
    m = ATTN_GROUP * tq
    out = pl.pallas_call(
        functools.partial(_attn_body, tq=tq, tk=tk, n_kv=S // tk),
        grid=(B, ATTN_KV_HEADS, S // tq),
        in_specs=[pl.BlockSpec((ATTN_GROUP, tq, d), lambda b, kh, i: (kh, b * nq + i, 0)),
                  pl.BlockSpec((None, S, d), lambda b, kh, i: (b, 0, kh)),
                  pl.BlockSpec((None, S, 2 * d), lambda b, kh, i: (b, 0, kh))],
        out_specs=pl.BlockSpec((None, tq, gw), lambda b, kh, i: (b, i, kh)),
        out_shape=jax.ShapeDtypeStruct((B, S, ATTN_Q_HEADS * d), BF16),
        scratch_shapes=[pltpu.VMEM((m, LANES), F32), pltpu.VMEM((m, 2 * d), F32)],
        compiler_params=_cparams("parallel", "parallel", "arbitrary"),
        name="attention",
    )(qn, k3, v3)
    return out.reshape(B * S, ATTN_Q_HEADS * d)


ROUTER_GROUP_ROW = MOE_EXPERTS
SUBLANES = 8
assert MOE_EXPERTS_PER_GROUP == SUBLANES and MOE_GROUPS <= SUBLANES


def _route(logits_t):
    tm = logits_t.shape[1]
    row = lax.broadcasted_iota(jnp.int32, (SUBLANES, tm), 0)
    neg = jnp.float32(-jnp.inf)
    big = jnp.int32(SUBLANES)

    def first_argmax(x):
        mx = jnp.max(x, axis=0, keepdims=True)
        return mx, jnp.min(jnp.where(x == mx, row, big), axis=0, keepdims=True)

    gl = jnp.where(row < MOE_GROUPS, logits_t[ROUTER_GROUP_ROW:ROUTER_GROUP_ROW + SUBLANES, :], neg)
    gmax, g_sel = first_argmax(gl)
    g_w = 1.0 / jnp.sum(jnp.exp(gl - gmax), axis=0, keepdims=True)
    el = logits_t[0:SUBLANES, :]
    for g in range(1, MOE_GROUPS):
        el = jnp.where(g_sel == g, logits_t[g * SUBLANES:(g + 1) * SUBLANES, :], el)
    v1, i1 = first_argmax(el)
    v2, i2 = first_argmax(jnp.where(row == i1, neg, el))
    e2 = jnp.exp(v2 - v1)
    w1 = g_w / (1.0 + e2)
    w2 = g_w * e2 / (1.0 + e2)
    base = g_sel * MOE_EXPERTS_PER_GROUP
    return jnp.where(row == 0, (base + i1).astype(F32),
                     jnp.where(row == 1, (base + i2).astype(F32),
                               jnp.where(row == 2, w1, jnp.where(row == 3, w2, 0.0))))


def _out_proj_body(y_ref, h_ref, w_ref, g_ref, wr_ref, br_ref, hout_ref, xn_ref, rp_ref, rpt_ref):
    tm = h_ref.shape[0]
    sub = min(OUT_PROJ_SUB_ROWS, tm)
    wr = wr_ref[...]
    for s in range(tm // sub):
        rows = slice(s * sub, (s + 1) * sub)
        h = h_ref[rows, :] + jnp.dot(y_ref[rows, :], w_ref[...], preferred_element_type=F32)
        hout_ref[rows, :] = h
        xn = h * lax.rsqrt(jnp.mean(h * h, axis=-1, keepdims=True) + NORM_EPS) * g_ref[...]
        xn_ref[rows, :] = _pack_bf16_pairs(xn)
        xn_hi = xn.astype(BF16)
        xn_lo = (xn - xn_hi.astype(F32)).astype(BF16)
        hi = jnp.dot(xn_hi, wr, preferred_element_type=F32)
        lo = jnp.dot(xn_lo, wr[:, :LANES], preferred_element_type=F32)
        logits = hi[:, :LANES] + hi[:, LANES:] + lo + br_ref[...]
        rpt = _route(logits.T)
        rpt_ref[:, rows] = rpt
        full = jnp.concatenate([rpt, jnp.zeros((LANES - SUBLANES, sub), F32)], axis=0)
        rp_ref[rows, :] = full.T


OUT_PROJ_SUB_ROWS = 512


def _out_proj(y, h, w, gain, w_router, b_router, tm=1024):
    T, K = y.shape
    D = h.shape[1]
    tm = min(tm, T)
    row = lambda i: (i, 0)
    const = lambda i: (0, 0)
    return pl.pallas_call(
        _out_proj_body,
        grid=(T // tm,),
        in_specs=[pl.BlockSpec((tm, K), row), pl.BlockSpec((tm, D), row), pl.BlockSpec((K, D), const),
                  pl.BlockSpec((1, D), const), pl.BlockSpec((D, 2 * LANES), const), pl.BlockSpec((1, LANES), const)],
        out_specs=[pl.BlockSpec((tm, D), row), pl.BlockSpec((tm, D // 2), row), pl.BlockSpec((tm, LANES), row),
                   pl.BlockSpec((SUBLANES, tm), lambda i: (0, i))],
        out_shape=[jax.ShapeDtypeStruct((T, D), F32), jax.ShapeDtypeStruct((T, D // 2), U32),
                   jax.ShapeDtypeStruct((T, LANES), F32), jax.ShapeDtypeStruct((SUBLANES, T), F32)],
        compiler_params=_cparams("parallel"),
        name="out_proj",
    )(y, h, w, gain.reshape(1, D), w_router, b_router)


MOE_STEP_BLOCKS = 2


def _moe_body(be_ref, bv_ref, x_ref, *refs):
    nb = MOE_STEP_BLOCKS
    wgu_refs, wd_refs = refs[:nb], refs[nb:2 * nb]
    y_ref = refs[2 * nb]
    wgu_bf_refs, wd_bf_refs = refs[2 * nb + 1:3 * nb + 1], refs[3 * nb + 1:]
    s = pl.program_id(0)

    for j in range(nb):
        b = s * nb + j

        @pl.when((s == 0) | (be_ref[b] != be_ref[jnp.maximum(b - nb, 0)]))
        def _():
            wgu_bf_refs[j][...] = wgu_refs[j][...].astype(BF16)
            wd_bf_refs[j][...] = wd_refs[j][...].astype(BF16)

    @pl.when(bv_ref[s * nb] != 0)
    def _():
        for j in range(nb):
            rows = slice(j * MOE_BLOCK, (j + 1) * MOE_BLOCK)
            lo, hi = _unpack_bf16_pairs(x_ref[rows, :])
            x = jnp.concatenate([lo.astype(BF16), hi.astype(BF16)], axis=1)
            gu = jnp.dot(x, wgu_bf_refs[j][...], preferred_element_type=F32)
            gate = gu[:, :MOE_FF]
            up = gu[:, MOE_FF:]
            act = (gate * jax.nn.sigmoid(gate) * up).astype(BF16)
            y_ref[rows, :] = _pack_bf16_pairs(jnp.dot(act, wd_bf_refs[j][...], preferred_element_type=F32))

    @pl.when(bv_ref[s * nb] == 0)
    def _():
        y_ref[...] = jnp.zeros_like(y_ref)


def _moe_experts(xs, block_e, block_valid, w_gate_up, w_down, layer):
    P, W = xs.shape
    D = 2 * W
    nb = MOE_STEP_BLOCKS
    n_steps = P // (nb * MOE_BLOCK)
    rows = lambda s, be, bv: (s, 0)
    wspec = lambda shape, j: pl.BlockSpec((None, None) + shape, lambda s, be, bv: (layer, be[s * nb + j], 0, 0))
    grid_spec = pltpu.PrefetchScalarGridSpec(
        num_scalar_prefetch=2,
        grid=(n_steps,),
        in_specs=([pl.BlockSpec((nb * MOE_BLOCK, W), rows)]
                  + [wspec((D, 2 * MOE_FF), j) for j in range(nb)] + [wspec((MOE_FF, D), j) for j in range(nb)]),
        out_specs=pl.BlockSpec((nb * MOE_BLOCK, W), rows),
        scratch_shapes=([pltpu.VMEM((D, 2 * MOE_FF), BF16)] * nb + [pltpu.VMEM((MOE_FF, D), BF16)] * nb),
    )
    return pl.pallas_call(
        _moe_body,
        grid_spec=grid_spec,
        out_shape=jax.ShapeDtypeStruct((P, W), U32),
        compiler_params=_cparams("arbitrary"),
        name="moe_experts",
    )(block_e, block_valid, xs, *([w_gate_up] * nb), *([w_down] * nb))


def _rank_body(rpt_ref, tri_ref, rank_ref, counts_ref, carry_ref, *, n_groups):
    @pl.when(pl.program_id(0) == 0)
    def _():
        carry_ref[...] = jnp.zeros_like(carry_ref)

    erow = lax.broadcasted_iota(I32, (MOE_EXPERTS, LANES), 0)
    carry = carry_ref[...]
    tri = tri_ref[...]
    r0, r1 = [], []
    for g in range(n_groups):
        c = slice(g * LANES, (g + 1) * LANES)
        oh0 = erow == rpt_ref[0:1, c].astype(I32)
        oh1 = erow == rpt_ref[1:2, c].astype(I32)
        both = jnp.where(oh0 | oh1, 1.0, 0.0)
        before = jnp.dot(both.astype(BF16), tri, preferred_element_type=F32) + carry
        r0.append(jnp.sum(jnp.where(oh0, before, 0.0), axis=0, keepdims=True))
        r1.append(jnp.sum(jnp.where(oh1, before, 0.0), axis=0, keepdims=True))
        carry = carry + jnp.sum(both, axis=1, keepdims=True)
    carry_ref[...] = carry
    counts_ref[...] = carry.astype(I32)
    tm = n_groups * LANES
    rank = jnp.concatenate([jnp.concatenate(r0, axis=1), jnp.concatenate(r1, axis=1),
                            jnp.zeros((SUBLANES - MOE_TOPK, tm), F32)], axis=0)
    rank_ref[...] = rank.astype(I32)


def _slot_ranks(rpt, tm=2048):
    T = rpt.shape[1]
    tm = min(tm, T)
    tri = (lax.broadcasted_iota(I32, (LANES, LANES), 0) < lax.broadcasted_iota(I32, (LANES, LANES), 1)).astype(BF16)
    return pl.pallas_call(
        functools.partial(_rank_body, n_groups=tm // LANES),
        grid=(T // tm,),
        in_specs=[pl.BlockSpec((SUBLANES, tm), lambda i: (0, i)), pl.BlockSpec((LANES, LANES), lambda i: (0, 0))],
        out_specs=[pl.BlockSpec((SUBLANES, tm), lambda i: (0, i)), pl.BlockSpec((MOE_EXPERTS, LANES), lambda i: (0, 0))],
        out_shape=[jax.ShapeDtypeStruct((SUBLANES, T), I32), jax.ShapeDtypeStruct((MOE_EXPERTS, LANES), I32)],
        scratch_shapes=[pltpu.VMEM((MOE_EXPERTS, LANES), F32)],
        compiler_params=_cparams("arbitrary"),
        name="slot_ranks",
    )(rpt, tri)


def _moe_dispatch_plan(rpt):
    T = rpt.shape[1]
    A = T * MOE_TOPK
    n_blocks = -(-(A + MOE_EXPERTS * (MOE_BLOCK - 1)) // MOE_BLOCK)
    n_blocks = -(-n_blocks // MOE_STEP_BLOCKS) * MOE_STEP_BLOCKS
    rank, counts = _slot_ranks(rpt)
    counts = counts[:, 0]
    padded = ((counts + MOE_BLOCK - 1) // MOE_BLOCK) * MOE_BLOCK
    pends = jnp.cumsum(padded)
    pstarts = pends - padded
    e = rpt[:MOE_TOPK].astype(I32)
    experts = jnp.arange(MOE_EXPERTS, dtype=I32)[:, None, None]
    dest = jnp.sum(jnp.where(e[None] == experts, pstarts[:, None, None], 0), axis=0) + rank[:MOE_TOPK]
    block_start = jnp.arange(n_blocks, dtype=I32) * MOE_BLOCK
    block_e = jnp.minimum(jnp.sum(pends[None, :] <= block_start[:, None], axis=1), MOE_EXPERTS - 1).astype(I32)
    block_valid = (block_start < pends[-1]).astype(I32)
    return dest, block_e, block_valid, n_blocks * MOE_BLOCK


SC_CORES = 2
SC_SUBCORES = 16
SC_WORKERS = SC_CORES * SC_SUBCORES
SC_ROWS = 64


def _sc_mesh():
    return plsc.VectorSubcoreMesh(core_axis_name="c", subcore_axis_name="s")


def _sc_scratch(n_chunks, width):
    return [pltpu.VMEM((n_chunks, SC_ROWS), I32), pltpu.VMEM((n_chunks, SC_ROWS), I32),
            pltpu.VMEM((SC_ROWS, width), U32), pltpu.VMEM((SC_ROWS, width), U32),
            pltpu.SemaphoreType.DMA, pltpu.SemaphoreType.DMA, pltpu.SemaphoreType.DMA, pltpu.SemaphoreType.DMA]


def _sc_scatter_rows(x, dest, P):
    T, W = x.shape
    tw = T // SC_WORKERS
    nch = tw // SC_ROWS
    assert tw * SC_WORKERS == T and nch * SC_ROWS == tw and nch % 2 == 0
    idx = dest.reshape(MOE_TOPK, SC_WORKERS, nch, SC_ROWS)

    @functools.partial(pl.kernel, mesh=_sc_mesh(), out_type=jax.ShapeDtypeStruct((P, W), U32),
                       scratch_types=_sc_scratch(nch, W))
    def scatter(x_hbm, i0_hbm, i1_hbm, o_hbm, i0_v, i1_v, b0, b1, r0, r1, s0, s1):
        wid = lax.axis_index("s") * SC_CORES + lax.axis_index("c")
        base = wid * tw
        pltpu.sync_copy(i0_hbm.at[wid], i0_v)
        pltpu.sync_copy(i1_hbm.at[wid], i1_v)

        @pl.loop(0, nch, step=2)
        def _(j):
            ca = pltpu.async_copy(x_hbm.at[pl.ds(base + j * SC_ROWS, SC_ROWS)], b0, r0)
            cb = pltpu.async_copy(x_hbm.at[pl.ds(base + (j + 1) * SC_ROWS, SC_ROWS)], b1, r1)
            ca.wait()
            a0 = pltpu.async_copy(b0, o_hbm.at[i0_v.at[j]], s0)
            a1 = pltpu.async_copy(b0, o_hbm.at[i1_v.at[j]], s0)
            cb.wait()
            e0 = pltpu.async_copy(b1, o_hbm.at[i0_v.at[j + 1]], s1)
            e1 = pltpu.async_copy(b1, o_hbm.at[i1_v.at[j + 1]], s1)
            a0.wait()
            a1.wait()
            e0.wait()
            e1.wait()

    return scatter(x, idx[0], idx[1])


def _sc_gather_rows(table, dest):
    T = dest.shape[1]
    W = table.shape[1]
    tw = T // SC_WORKERS
    nch = tw // SC_ROWS
    assert tw * SC_WORKERS == T and nch * SC_ROWS == tw
    idx = dest.reshape(MOE_TOPK, SC_WORKERS, nch, SC_ROWS)
    out = jax.ShapeDtypeStruct((T, W), U32)

    @functools.partial(pl.kernel, mesh=_sc_mesh(), out_type=[out, out], scratch_types=_sc_scratch(nch, W))
    def gather(tab_hbm, i0_hbm, i1_hbm, o0_hbm, o1_hbm, i0_v, i1_v, b0, b1, g0, g1, w0, w1):
        wid = lax.axis_index("s") * SC_CORES + lax.axis_index("c")
        base = wid * tw
        pltpu.sync_copy(i0_hbm.at[wid], i0_v)
        pltpu.sync_copy(i1_hbm.at[wid], i1_v)

        @pl.loop(0, nch)
        def _(j):
            rows = pl.ds(base + j * SC_ROWS, SC_ROWS)
            c0 = pltpu.async_copy(tab_hbm.at[i0_v.at[j]], b0, g0)
            c1 = pltpu.async_copy(tab_hbm.at[i1_v.at[j]], b1, g1)
            c0.wait()
            d0 = pltpu.async_copy(b0, o0_hbm.at[rows], w0)
            c1.wait()
            d1 = pltpu.async_copy(b1, o1_hbm.at[rows], w1)
            d0.wait()
            d1.wait()

    return gather(table, idx[0], idx[1])


def _moe(xn, rpt, w_gate_up, w_down, layer):
    dest, block_e, block_valid, P = _moe_dispatch_plan(rpt)
    xs = _sc_scatter_rows(xn, dest, P)
    y = _moe_experts(xs, block_e, block_valid, w_gate_up, w_down, layer)
    return _sc_gather_rows(y, dest)


def _final_body(h_ref, y0_ref, y1_ref, rp_ref, g_ref, o_ref):
    h = _combine(h_ref[...], y0_ref[...], y1_ref[...], rp_ref[...])
    o_ref[...] = h * lax.rsqrt(jnp.mean(h * h, axis=-1, keepdims=True) + NORM_EPS) * g_ref[...]


def _final(h, y0, y1, rp, gain, tm=1024):
    T, D = h.shape
    tm = min(tm, T)
    row = lambda i: (i, 0)
    return pl.pallas_call(
        _final_body,
        grid=(T // tm,),
        in_specs=[pl.BlockSpec((tm, D), row), pl.BlockSpec((tm, D // 2), row), pl.BlockSpec((tm, D // 2), row),
                  pl.BlockSpec((tm, LANES), row), pl.BlockSpec((1, D), lambda i: (0, 0))],
        out_specs=pl.BlockSpec((tm, D), row),
        out_shape=jax.ShapeDtypeStruct((T, D), F32),
        compiler_params=_cparams("parallel"),
        name="final_norm",
    )(h, y0, y1, rp, gain.reshape(1, D))


def _rope_tables(n_tok, head_dim):
    t = jnp.arange(n_tok, dtype=jnp.int32)
    row = (t // GRID_W).astype(F32)
    col = (t % GRID_W).astype(F32)
    half = head_dim // 2
    inv = ROPE_THETA ** (-jnp.arange(0, half, 2, dtype=F32) / half)
    ang_r = row[:, None] * inv[None, :]
    ang_c = col[:, None] * inv[None, :]
    ang = jnp.concatenate([ang_r, ang_r, ang_c, ang_c], axis=-1)
    q = half // 2
    sign = jnp.where((jnp.arange(head_dim) % half) < q, -1.0, 1.0).astype(F32)
    return jnp.cos(ang), jnp.sin(ang) * sign[None, :]


def _router_weights(w_group, b_group, w_expert, b_expert):
    D = w_group.shape[0]
    pad = LANES - MOE_EXPERTS - MOE_GROUPS
    w = jnp.concatenate([w_expert, w_group, jnp.zeros((D, pad), F32)], axis=1).astype(F32)
    b = jnp.concatenate([b_expert, b_group, jnp.zeros((pad,), F32)]).reshape(1, LANES)
    w_hi = w.astype(BF16)
    w_lo = (w - w_hi.astype(F32)).astype(BF16)
    return jnp.concatenate([w_hi, w_lo], axis=1), b.astype(F32)


def _trunk(x, p):
    B, S, D = x.shape
    T = B * S
    cos_r, sin_r = _rope_tables(S, RET_QK_DIM)
    cos_a, sin_a = [jnp.take(t, jnp.asarray(ATTN_HEAD_PERM, I32), axis=1) for t in _rope_tables(S, ATTN_HEAD_DIM)]
    h = x.reshape(T, D)

    proj = _in_proj(h, p["ln_mix"][0], p["ret_w_in"][0])
    y = _retention(proj, p["ret_decay_logit"][0], cos_r, sin_r, B, S)
    h, xn, rp, rpt = _out_proj(y, h, p["ret_w_out"][0], p["ln_ffn"][0], *p["router"][0])
    y0, y1 = _moe(xn, rpt, p["moe_w_gate_up"], p["moe_w_down"], 0)

    h, qn, kn, vx = _attn_in_proj(h, p["ln_mix"][1], p["attn_w_in"][0], (y0, y1, rp),
                                  p["attn_q_gain"][0], p["attn_k_gain"][0], cos_a, sin_a, S)
    y = _attention(qn, kn, vx, B, S)
    h, xn, rp, rpt = _out_proj(y, h, p["attn_w_out"][0], p["ln_ffn"][1], *p["router"][1])
    y0, y1 = _moe(xn, rpt, p["moe_w_gate_up"], p["moe_w_down"], 1)

    return _final(h, y0, y1, rp, p["ln_final"]).reshape(B, S, D)


def _prepare_params(ln_mix, ret_w_in, ret_decay_logit, ret_w_out, attn_w_in, attn_q_gain, attn_k_gain,
                    attn_w_out, ln_ffn, moe_w_group, moe_b_group, moe_w_expert, moe_b_expert, moe_w_gate_up,
                    moe_w_down, ln_final):
    d = ATTN_HEAD_DIM
    perm = jnp.asarray(ATTN_HEAD_PERM, I32)
    qk_cols = (jnp.arange((ATTN_Q_HEADS + ATTN_KV_HEADS) * d, dtype=I32) // d) * d
    qk_cols = qk_cols + jnp.tile(perm, ATTN_Q_HEADS + ATTN_KV_HEADS)
    cols = jnp.concatenate([qk_cols, jnp.arange(qk_cols.shape[0], ATTN_IN_WIDTH, dtype=I32)])
    return {
        "ln_mix": ln_mix, "ln_ffn": ln_ffn, "ln_final": ln_final,
        "ret_w_in": ret_w_in.astype(BF16), "ret_decay_logit": ret_decay_logit, "ret_w_out": ret_w_out.astype(BF16),
        "attn_w_in": jnp.take(attn_w_in, cols, axis=2).astype(BF16),
        "attn_q_gain": jnp.take(attn_q_gain, perm, axis=1), "attn_k_gain": jnp.take(attn_k_gain, perm, axis=1),
        "attn_w_out": attn_w_out.astype(BF16),
        "router": [_router_weights(moe_w_group[i], moe_b_group[i], moe_w_expert[i], moe_b_expert[i])
                   for i in range(moe_w_group.shape[0])],
        "moe_w_gate_up": moe_w_gate_up, "moe_w_down": moe_w_down,
    }


def kernel(x_prompt, x_sample, ln_mix, ret_w_in, ret_decay_logit, ret_w_out, attn_w_in, attn_q_gain, attn_k_gain,
           attn_w_out, ln_ffn, moe_w_group, moe_b_group, moe_w_expert, moe_b_expert, moe_w_gate_up, moe_w_down,
           ln_final):
    p = _prepare_params(ln_mix, ret_w_in, ret_decay_logit, ret_w_out, attn_w_in, attn_q_gain, attn_k_gain,
                        attn_w_out, ln_ffn, moe_w_group, moe_b_group, moe_w_expert, moe_b_expert, moe_w_gate_up,
                        moe_w_down, ln_final)
    return _trunk(x_prompt, p), _trunk(x_sample, p)
```

```python
import functools

import jax
import jax.numpy as jnp
from jax import lax
from jax.experimental import pallas as pl
from jax.experimental.pallas import tpu as pltpu
from jax.experimental.pallas import tpu_sc as plsc

F32 = jnp.float32
BF16 = jnp.bfloat16
U32 = jnp.uint32
I32 = jnp.int32

D_MODEL = 1024
GRID_W = 64
ROPE_THETA = 10000.0
NORM_EPS = 1e-6

RET_HEADS = 4
RET_QK_DIM = 256
RET_V_DIM = 512
RET_SCAN_CHUNK = 256
RET_QK_WIDTH = RET_HEADS * RET_QK_DIM
RET_V_WIDTH = RET_HEADS * RET_V_DIM

ATTN_Q_HEADS = 8
ATTN_KV_HEADS = 2
ATTN_HEAD_DIM = 128
ATTN_GROUP = ATTN_Q_HEADS // ATTN_KV_HEADS
ATTN_IN_WIDTH = (ATTN_Q_HEADS + 2 * ATTN_KV_HEADS) * ATTN_HEAD_DIM

MOE_GROUPS = 4
MOE_EXPERTS_PER_GROUP = 8
MOE_EXPERTS = MOE_GROUPS * MOE_EXPERTS_PER_GROUP
MOE_TOPK = 2
MOE_FF = 512
MOE_BLOCK = 512

LANES = 128
LOG2_E = 1.4426950408889634
VMEM_LIMIT = 56 * 1024 * 1024


def _cparams(*sem):
    return pltpu.CompilerParams(dimension_semantics=sem, vmem_limit_bytes=VMEM_LIMIT)


def _pack_bf16_pairs(x):
    w = x.shape[1] // 2
    lo = lax.bitcast_convert_type(x[:, :w].astype(BF16).astype(F32), U32)
    hi = lax.bitcast_convert_type(x[:, w:].astype(BF16).astype(F32), U32)
    return lax.shift_right_logical(lo, jnp.uint32(16)) | (hi & jnp.uint32(0xFFFF0000))


def _unpack_bf16_pairs(u):
    lo = lax.bitcast_convert_type(lax.shift_left(u, jnp.uint32(16)), F32)
    hi = lax.bitcast_convert_type(u & jnp.uint32(0xFFFF0000), F32)
    return lo, hi


def _combine(h, y0_u32, y1_u32, rp):
    lo0, hi0 = _unpack_bf16_pairs(y0_u32)
    lo1, hi1 = _unpack_bf16_pairs(y1_u32)
    w0 = rp[:, 2:3]
    w1 = rp[:, 3:4]
    half = h.shape[1] // 2
    return jnp.concatenate([h[:, :half] + w0 * lo0 + w1 * lo1, h[:, half:] + w0 * hi0 + w1 * hi1], axis=1)


IN_PROJ_SUB_ROWS = 256


def _in_proj_body(h_ref, g_ref, w_ref, o_ref):
    tm = h_ref.shape[0]
    sub = min(IN_PROJ_SUB_ROWS, tm)
    for s in range(tm // sub):
        rows = slice(s * sub, (s + 1) * sub)
        h = h_ref[rows, :]
        xn = (h * lax.rsqrt(jnp.mean(h * h, axis=-1, keepdims=True) + NORM_EPS) * g_ref[...]).astype(BF16)
        o_ref[rows, :] = jnp.dot(xn, w_ref[...], preferred_element_type=F32).astype(o_ref.dtype)


def _in_proj(h, gain, w, tm=512):
    T, D = h.shape
    N = w.shape[1]
    tm = min(tm, T)
    return pl.pallas_call(
        _in_proj_body,
        grid=(T // tm,),
        in_specs=[pl.BlockSpec((tm, D), lambda i: (i, 0)), pl.BlockSpec((1, D), lambda i: (0, 0)),
                  pl.BlockSpec((D, N), lambda i: (0, 0))],
        out_specs=pl.BlockSpec((tm, N), lambda i: (i, 0)),
        out_shape=jax.ShapeDtypeStruct((T, N), BF16),
        compiler_params=_cparams("parallel"),
        name="in_proj",
    )(h, gain.reshape(1, D), w)


def _rope256(x, cos, sin_signed):
    xr = jnp.concatenate([pltpu.roll(x[:, :LANES], 64, 1), pltpu.roll(x[:, LANES:], 64, 1)], axis=1)
    return x * cos + xr * sin_signed


def _retention_body(dl_ref, q_ref, k_ref, v_ref, cos_ref, sin_ref, y_ref, state_ref, obwd_ref,
                    *, n_chunks, n_blocks):
    C = RET_SCAN_CHUNK
    h = pl.program_id(1)
    phase = pl.program_id(2)
    n = pl.program_id(3)
    sb = n_chunks * C

    def log_gamma(s):
        x = jnp.full((1, 1), s, F32)
        return jnp.minimum(x, 0.0) - jnp.log(1.0 + jnp.exp(-jnp.abs(x)))

    lg_f = log_gamma(dl_ref[h])
    lg_b = log_gamma(dl_ref[RET_HEADS + h])
    pos = lax.broadcasted_iota(jnp.int32, (C, RET_QK_DIM), 0).astype(F32)
    k_scale = RET_QK_DIM ** -0.5

    @pl.when(n == 0)
    def _():
        state_ref[...] = jnp.zeros_like(state_ref)

    def load(rows, q_dec, k_dec):
        cos = cos_ref[rows, :]
        sin = sin_ref[rows, :]
        q = _rope256(q_ref[rows, :], cos, sin)
        k = _rope256(k_ref[rows, :], cos, sin)
        return q, k, v_ref[rows, :], q * q_dec, (k * k_dec).T

    @pl.when(phase == 0)
    def _():
        q_dec = jnp.exp((C - pos) * lg_b).astype(BF16)
        k_dec = (jnp.exp(pos * lg_b) * k_scale).astype(BF16)
        chunk_dec = jnp.exp(lg_b * C)
        base = (n_blocks - 1 - n) * sb

        def body(cc, carry):
            c = n_chunks - 1 - cc
            rows = pl.ds(pl.multiple_of(c * C, C), C)
            _, _, v, qd, kt = load(rows, q_dec, k_dec)
            st = state_ref[...]
            out = jnp.dot(qd, st.astype(BF16), preferred_element_type=F32)
            state_ref[...] = st * chunk_dec + jnp.dot(kt, v, preferred_element_type=F32)
            obwd_ref[pl.ds(pl.multiple_of(base + c * C, C), C), :] = out.astype(obwd_ref.dtype)
            return carry

        lax.fori_loop(0, n_chunks, body, 0, unroll=min(RET_UNROLL, n_chunks))

    @pl.when(phase == 1)
    def _():
        q_dec = jnp.exp((pos + 1.0) * lg_f).astype(BF16)
        k_dec = (jnp.exp((C - 1.0 - pos) * lg_f) * k_scale).astype(BF16)
        chunk_dec = jnp.exp(lg_f * C)
        ii = lax.broadcasted_iota(jnp.int32, (C, C), 0)
        jj = lax.broadcasted_iota(jnp.int32, (C, C), 1)
        dist = (ii - jj).astype(F32)
        inner = (jnp.where(dist >= 0, jnp.exp(lg_f * jnp.maximum(dist, 0.0)), 0.0)
                 + jnp.where(dist <= 0, jnp.exp(lg_b * jnp.maximum(-dist, 0.0)), 0.0)) * k_scale
        base = n * sb

        def body(c, carry):
            rows = pl.ds(pl.multiple_of(c * C, C), C)
            q, k, v, qd, kt = load(rows, q_dec, k_dec)
            scores = lax.dot_general(q, k, (((1,), (1,)), ((), ())), preferred_element_type=F32) * inner
            st = state_ref[...]
            lhs = jnp.concatenate([qd, scores.astype(BF16)], axis=1)
            rhs = jnp.concatenate([st.astype(BF16), v], axis=0)
            o = jnp.dot(lhs, rhs, preferred_element_type=F32)
            state_ref[...] = st * chunk_dec + jnp.dot(kt, v, preferred_element_type=F32)
            o = o + obwd_ref[pl.ds(pl.multiple_of(base + c * C, C), C), :].astype(F32)
            y_ref[rows, :] = o.astype(y_ref.dtype)
            return carry

        lax.fori_loop(0, n_chunks, body, 0, unroll=min(RET_UNROLL, n_chunks))


RET_UNROLL = 8


def _retention(proj, decay_logit, cos, sin_signed, B, S, sb=4096):
    sb = min(sb, S)
    nb = S // sb
    proj3 = proj.reshape(B, S, proj.shape[-1])
    blk = lambda p, n: jnp.where(p == 0, nb - 1 - n, n)
    kq = RET_QK_WIDTH // RET_QK_DIM
    kv = 2 * RET_QK_WIDTH // RET_V_DIM
    in_specs = [
        pl.BlockSpec(memory_space=pltpu.SMEM),
        pl.BlockSpec((None, sb, RET_QK_DIM), lambda b, h, p, n: (b, blk(p, n), h)),
        pl.BlockSpec((None, sb, RET_QK_DIM), lambda b, h, p, n: (b, blk(p, n), kq + h)),
        pl.BlockSpec((None, sb, RET_V_DIM), lambda b, h, p, n: (b, blk(p, n), kv + h)),
        pl.BlockSpec((sb, RET_QK_DIM), lambda b, h, p, n: (blk(p, n), 0)),
        pl.BlockSpec((sb, RET_QK_DIM), lambda b, h, p, n: (blk(p, n), 0)),
    ]
    out = pl.pallas_call(
        functools.partial(_retention_body, n_chunks=sb // RET_SCAN_CHUNK, n_blocks=nb),
        grid=(B, RET_HEADS, 2, nb),
        in_specs=in_specs,
        out_specs=pl.BlockSpec((None, sb, RET_V_DIM), lambda b, h, p, n: (b, n * p, h)),
        out_shape=jax.ShapeDtypeStruct((B, S, RET_V_WIDTH), BF16),
        scratch_shapes=[pltpu.VMEM((RET_QK_DIM, RET_V_DIM), F32), pltpu.VMEM((S, RET_V_DIM), BF16)],
        compiler_params=_cparams("parallel", "parallel", "arbitrary", "arbitrary"),
        name="retention",
    )(decay_logit.reshape(2 * RET_HEADS).astype(F32), proj3, proj3, proj3,
      cos.astype(BF16), sin_signed.astype(BF16))
    return out.reshape(B * S, RET_V_WIDTH)


ATTN_HEAD_PERM = tuple(list(range(0, 32)) + list(range(64, 96)) + list(range(32, 64)) + list(range(96, 128)))


def _norm_rope_head_pair(x, gain2, cos2, sin2):
    d = ATTN_HEAD_DIM
    bi = lax.broadcasted_iota(I32, (2 * d, 2 * d), 0) // d
    bj = lax.broadcasted_iota(I32, (2 * d, 2 * d), 1) // d
    ones_bd = jnp.where(bi == bj, 1.0, 0.0).astype(BF16)
    ss = jnp.dot((x * x).astype(BF16), ones_bd, preferred_element_type=F32)
    x = x * lax.rsqrt(ss * (1.0 / d) + NORM_EPS) * gain2
    return _rope256(x, cos2, sin2)


def _attn_in_proj_body(h_ref, y0_ref, y1_ref, rp_ref, g_ref, w_ref, qg_ref, kg_ref, cos_ref, sin_ref,
                       hout_ref, q_ref, k_ref, v_ref):
    d = ATTN_HEAD_DIM
    tm = h_ref.shape[0]
    sub = min(IN_PROJ_SUB_ROWS, tm)
    q_scale = (d ** -0.5) * LOG2_E
    for s in range(tm // sub):
        rows = slice(s * sub, (s + 1) * sub)
        h = _combine(h_ref[rows, :], y0_ref[rows, :], y1_ref[rows, :], rp_ref[rows, :])
        hout_ref[rows, :] = h
        xn = (h * lax.rsqrt(jnp.mean(h * h, axis=-1, keepdims=True) + NORM_EPS) * g_ref[...]).astype(BF16)
        acc = jnp.dot(xn, w_ref[...], preferred_element_type=F32)
        cos2 = jnp.concatenate([cos_ref[rows, :]] * 2, axis=1)
        sin2 = jnp.concatenate([sin_ref[rows, :]] * 2, axis=1)
        qg2 = jnp.concatenate([qg_ref[...]] * 2, axis=1) * q_scale
        kg2 = jnp.concatenate([kg_ref[...]] * 2, axis=1)
        for hp in range(ATTN_Q_HEADS // 2):
            c = slice(2 * hp * d, (2 * hp + 2) * d)
            q = _norm_rope_head_pair(acc[:, c], qg2, cos2, sin2).astype(q_ref.dtype)
            q_ref[2 * hp, rows, :] = q[:, :d]
            q_ref[2 * hp + 1, rows, :] = q[:, d:]
        for hp in range(ATTN_KV_HEADS // 2):
            c0 = (ATTN_Q_HEADS + 2 * hp) * d
            k = _norm_rope_head_pair(acc[:, c0:c0 + 2 * d], kg2, cos2, sin2)
            k_ref[rows, 2 * hp * d:(2 * hp + 2) * d] = k.astype(k_ref.dtype)
        for hk in range(ATTN_KV_HEADS):
            c1 = (ATTN_Q_HEADS + ATTN_KV_HEADS + hk) * d
            v_ref[rows, 2 * hk * d:(2 * hk + 1) * d] = acc[:, c1:c1 + d].astype(v_ref.dtype)
            v_ref[rows, (2 * hk + 1) * d:(2 * hk + 2) * d] = jnp.ones((sub, d), v_ref.dtype)


def _attn_in_proj(h, gain, w, comb, q_gain, k_gain, cos, sin_signed, S, tm=1024):
    T, D = h.shape
    N = w.shape[1]
    tm = min(tm, S)
    spb = S // tm
    d = ATTN_HEAD_DIM
    y0, y1, rp = comb
    row = lambda i: (i, 0)
    const = lambda i: (0, 0)
    tab = lambda i: (i % spb, 0)
    return pl.pallas_call(
        _attn_in_proj_body,
        grid=(T // tm,),
        in_specs=[pl.BlockSpec((tm, D), row), pl.BlockSpec((tm, D // 2), row), pl.BlockSpec((tm, D // 2), row),
                  pl.BlockSpec((tm, LANES), row), pl.BlockSpec((1, D), const), pl.BlockSpec((D, N), const),
                  pl.BlockSpec((1, d), const), pl.BlockSpec((1, d), const),
                  pl.BlockSpec((tm, d), tab), pl.BlockSpec((tm, d), tab)],
        out_specs=[pl.BlockSpec((tm, D), row), pl.BlockSpec((ATTN_Q_HEADS, tm, d), lambda i: (0, i, 0)),
                   pl.BlockSpec((tm, ATTN_KV_HEADS * d), row), pl.BlockSpec((tm, 2 * ATTN_KV_HEADS * d), row)],
        out_shape=[jax.ShapeDtypeStruct((T, D), F32), jax.ShapeDtypeStruct((ATTN_Q_HEADS, T, d), BF16),
                   jax.ShapeDtypeStruct((T, ATTN_KV_HEADS * d), BF16),
                   jax.ShapeDtypeStruct((T, 2 * ATTN_KV_HEADS * d), BF16)],
        compiler_params=_cparams("parallel"),
        name="attn_in_proj",
    )(h, y0, y1, rp, gain.reshape(1, D), w, q_gain.reshape(1, d), k_gain.reshape(1, d), cos, sin_signed)


def _attn_body(q_ref, k_ref, v_ref, o_ref, m_ref, acc_ref, *, tq, tk, n_kv):
    d = ATTN_HEAD_DIM
    m_ref[...] = jnp.full_like(m_ref, -jnp.inf)
    acc_ref[...] = jnp.zeros_like(acc_ref)
    rep = tk // LANES

    def step(c, carry):
        rows = pl.ds(pl.multiple_of(c * tk, tk), tk)
        k = k_ref[rows, :]
        v = v_ref[rows, :]
        for g in range(ATTN_GROUP):
            r = slice(g * tq, (g + 1) * tq)
            s = lax.dot_general(q_ref[g], k, (((1,), (1,)), ((), ())), preferred_element_type=F32)
            m_prev = m_ref[r, :]
            m_new = jnp.maximum(m_prev, jnp.max(s, axis=-1, keepdims=True))
            alpha = jnp.exp2(m_prev - m_new)
            p = jnp.exp2(s - jnp.concatenate([m_new] * rep, axis=1))
            pv = jnp.dot(p.astype(BF16), v, preferred_element_type=F32)
            acc_ref[r, :] = jnp.concatenate([alpha, alpha], axis=1) * acc_ref[r, :] + pv
            m_ref[r, :] = m_new
        return carry

    lax.fori_loop(0, n_kv, step, 0, unroll=min(16, n_kv))
    for g in range(ATTN_GROUP):
        r = slice(g * tq, (g + 1) * tq)
        o_ref[:, g * d:(g + 1) * d] = (acc_ref[r, :d] / acc_ref[r, d:]).astype(o_ref.dtype)


def _attention(qn, kn, vx, B, S, tq=1024, tk=256):
    d = ATTN_HEAD_DIM
    tq = min(tq, S)
    tk = min(tk, S)
    gw = ATTN_GROUP * d
    nq = S // tq
    k3 = kn.reshape(B, S, ATTN_KV_HEADS * d)
    v3 = vx.reshape(B, S, 2 * ATTN_KV_HEADS * d)
    m = ATTN_GROUP * tq
    out = pl.pallas_call(
        functools.partial(_attn_body, tq=tq, tk=tk, n_kv=S // tk),
        grid=(B, ATTN_KV_HEADS, S // tq),
        in_specs=[pl.BlockSpec((ATTN_GROUP, tq, d), lambda b, kh, i: (kh, b * nq + i, 0)),
                  pl.BlockSpec((None, S, d), lambda b, kh, i: (b, 0, kh)),
                  pl.BlockSpec((None, S, 2 * d), lambda b, kh, i: (b, 0, kh))],
        out_specs=pl.BlockSpec((None, tq, gw), lambda b, kh, i: (b, i, kh)),
        out_shape=jax.ShapeDtypeStruct((B, S, ATTN_Q_HEADS * d), BF16),
        scratch_shapes=[pltpu.VMEM((m, LANES), F32), pltpu.VMEM((m, 2 * d), F32)],
        compiler_params=_cparams("parallel", "parallel", "arbitrary"),
        name="attention",
    )(qn, k3, v3)
    return out.reshape(B * S, ATTN_Q_HEADS * d)


ROUTER_GROUP_ROW = MOE_EXPERTS
SUBLANES = 8
assert MOE_EXPERTS_PER_GROUP == SUBLANES and MOE_GROUPS <= SUBLANES


def _route(logits_t):
    tm = logits_t.shape[1]
    row = lax.broadcasted_iota(jnp.int32, (SUBLANES, tm), 0)
    neg = jnp.float32(-jnp.inf)
    big = jnp.int32(SUBLANES)

    def first_argmax(x):
        mx = jnp.max(x, axis=0, keepdims=True)
        return mx, jnp.min(jnp.where(x == mx, row, big), axis=0, keepdims=True)

    gl = jnp.where(row < MOE_GROUPS, logits_t[ROUTER_GROUP_ROW:ROUTER_GROUP_ROW + SUBLANES, :], neg)
    gmax, g_sel = first_argmax(gl)
    g_w = 1.0 / jnp.sum(jnp.exp(gl - gmax), axis=0, keepdims=True)
    el = logits_t[0:SUBLANES, :]
    for g in range(1, MOE_GROUPS):
        el = jnp.where(g_sel == g, logits_t[g * SUBLANES:(g + 1) * SUBLANES, :], el)
    v1, i1 = first_argmax(el)
    v2, i2 = first_argmax(jnp.where(row == i1, neg, el))
    e2 = jnp.exp(v2 - v1)
    w1 = g_w / (1.0 + e2)
    w2 = g_w * e2 / (1.0 + e2)
    base = g_sel * MOE_EXPERTS_PER_GROUP
    return jnp.where(row == 0, (base + i1).astype(F32),
                     jnp.where(row == 1, (base + i2).astype(F32),
                               jnp.where(row == 2, w1, jnp.where(row == 3, w2, 0.0))))


def _retention_gate(o, g):
    dv = RET_V_DIM
    parts = []
    for hh in range(RET_HEADS):
        oh = o[:, hh * dv:(hh + 1) * dv].astype(F32)
        oh = oh * lax.rsqrt(jnp.mean(oh * oh, axis=-1, keepdims=True) + NORM_EPS)
        gh = g[:, hh * dv:(hh + 1) * dv]
        parts.append(gh * jax.nn.sigmoid(gh) * oh.astype(BF16))
    return jnp.concatenate(parts, axis=1)


def _out_proj_body(gated, y_ref, *refs):
    if gated:
        gate_ref, refs = refs[0], refs[1:]
    h_ref, w_ref, g_ref, wr_ref, br_ref, hout_ref, xn_ref, rp_ref, rpt_ref = refs
    tm = h_ref.shape[0]
    sub = min(OUT_PROJ_SUB_ROWS, tm)
    wr = wr_ref[...]
    for s in range(tm // sub):
        rows = slice(s * sub, (s + 1) * sub)
        y = _retention_gate(y_ref[rows, :], gate_ref[rows, :]) if gated else y_ref[rows, :]
        h = h_ref[rows, :] + jnp.dot(y, w_ref[...], preferred_element_type=F32)
        hout_ref[rows, :] = h
        xn = h * lax.rsqrt(jnp.mean(h * h, axis=-1, keepdims=True) + NORM_EPS) * g_ref[...]
        xn_ref[rows, :] = _pack_bf16_pairs(xn)
        xn_hi = xn.astype(BF16)
        xn_lo = (xn - xn_hi.astype(F32)).astype(BF16)
        hi = jnp.dot(xn_hi, wr, preferred_element_type=F32)
        lo = jnp.dot(xn_lo, wr[:, :LANES], preferred_element_type=F32)
        logits = hi[:, :LANES] + hi[:, LANES:] + lo + br_ref[...]
        rpt = _route(logits.T)
        rpt_ref[:, rows] = rpt
        full = jnp.concatenate([rpt, jnp.zeros((LANES - SUBLANES, sub), F32)], axis=0)
        rp_ref[rows, :] = full.T


OUT_PROJ_SUB_ROWS = 512


def _out_proj(y, h, w, gain, w_router, b_router, gate_src=None, tm=1024):
    T, K = y.shape
    D = h.shape[1]
    tm = min(tm, T)
    row = lambda i: (i, 0)
    const = lambda i: (0, 0)
    in_specs = [pl.BlockSpec((tm, K), row)]
    args = [y]
    if gate_src is not None:
        assert K == RET_V_WIDTH and (2 * RET_QK_WIDTH + RET_V_WIDTH) % K == 0
        gate_col = (2 * RET_QK_WIDTH + RET_V_WIDTH) // K
        in_specs.append(pl.BlockSpec((tm, K), lambda i: (i, gate_col)))
        args.append(gate_src)
    in_specs += [pl.BlockSpec((tm, D), row), pl.BlockSpec((K, D), const), pl.BlockSpec((1, D), const),
                 pl.BlockSpec((D, 2 * LANES), const), pl.BlockSpec((1, LANES), const)]
    args += [h, w, gain.reshape(1, D), w_router, b_router]
    return pl.pallas_call(
        functools.partial(_out_proj_body, gate_src is not None),
        grid=(T // tm,),
        in_specs=in_specs,
        out_specs=[pl.BlockSpec((tm, D), row), pl.BlockSpec((tm, D // 2), row), pl.BlockSpec((tm, LANES), row),
                   pl.BlockSpec((SUBLANES, tm), lambda i: (0, i))],
        out_shape=[jax.ShapeDtypeStruct((T, D), F32), jax.ShapeDtypeStruct((T, D // 2), U32),
                   jax.ShapeDtypeStruct((T, LANES), F32), jax.ShapeDtypeStruct((SUBLANES, T), F32)],
        compiler_params=_cparams("parallel"),
        name="out_proj",
    )(*args)


MOE_STEP_BLOCKS = 2


def _moe_body(be_ref, bv_ref, x_ref, *refs):
    nb = MOE_STEP_BLOCKS
    wgu_refs, wd_refs = refs[:nb], refs[nb:2 * nb]
    y_ref = refs[2 * nb]
    wgu_bf_refs, wd_bf_refs = refs[2 * nb + 1:3 * nb + 1], refs[3 * nb + 1:]
    s = pl.program_id(0)

    for j in range(nb):
        b = s * nb + j

        @pl.when((s == 0) | (be_ref[b] != be_ref[jnp.maximum(b - nb, 0)]))
        def _():
            wgu_bf_refs[j][...] = wgu_refs[j][...].astype(BF16)
            wd_bf_refs[j][...] = wd_refs[j][...].astype(BF16)

    @pl.when(bv_ref[s * nb] != 0)
    def _():
        for j in range(nb):
            rows = slice(j * MOE_BLOCK, (j + 1) * MOE_BLOCK)
            lo, hi = _unpack_bf16_pairs(x_ref[rows, :])
            x = jnp.concatenate([lo.astype(BF16), hi.astype(BF16)], axis=1)
            gu = jnp.dot(x, wgu_bf_refs[j][...], preferred_element_type=F32)
            gate = gu[:, :MOE_FF]
            up = gu[:, MOE_FF:]
            act = (gate * jax.nn.sigmoid(gate) * up).astype(BF16)
            y_ref[rows, :] = _pack_bf16_pairs(jnp.dot(act, wd_bf_refs[j][...], preferred_element_type=F32))

    @pl.when(bv_ref[s * nb] == 0)
    def _():
        y_ref[...] = jnp.zeros_like(y_ref)


def _moe_experts(xs, block_e, block_valid, w_gate_up, w_down, layer):
    P, W = xs.shape
    D = 2 * W
    nb = MOE_STEP_BLOCKS
    n_steps = P // (nb * MOE_BLOCK)
    rows = lambda s, be, bv: (s, 0)
    wspec = lambda shape, j: pl.BlockSpec((None, None) + shape, lambda s, be, bv: (layer, be[s * nb + j], 0, 0))
    grid_spec = pltpu.PrefetchScalarGridSpec(
        num_scalar_prefetch=2,
        grid=(n_steps,),
        in_specs=([pl.BlockSpec((nb * MOE_BLOCK, W), rows)]
                  + [wspec((D, 2 * MOE_FF), j) for j in range(nb)] + [wspec((MOE_FF, D), j) for j in range(nb)]),
        out_specs=pl.BlockSpec((nb * MOE_BLOCK, W), rows),
        scratch_shapes=([pltpu.VMEM((D, 2 * MOE_FF), BF16)] * nb + [pltpu.VMEM((MOE_FF, D), BF16)] * nb),
    )
    return pl.pallas_call(
        _moe_body,
        grid_spec=grid_spec,
        out_shape=jax.ShapeDtypeStruct((P, W), U32),
        compiler_params=_cparams("arbitrary"),
        name="moe_experts",
    )(block_e, block_valid, xs, *([w_gate_up] * nb), *([w_down] * nb))


def _rank_body(rpt_ref, tri_ref, rank_ref, counts_ref, carry_ref, *, n_groups):
    @pl.when(pl.program_id(0) == 0)
    def _():
        carry_ref[...] = jnp.zeros_like(carry_ref)

    erow = lax.broadcasted_iota(I32, (MOE_EXPERTS, LANES), 0)
    carry = carry_ref[...]
    tri = tri_ref[...]
    r0, r1 = [], []
    for g in range(n_groups):
        c = slice(g * LANES, (g + 1) * LANES)
        oh0 = erow == rpt_ref[0:1, c].astype(I32)
        oh1 = erow == rpt_ref[1:2, c].astype(I32)
        both = jnp.where(oh0 | oh1, 1.0, 0.0)
        before = jnp.dot(both.astype(BF16), tri, preferred_element_type=F32) + carry
        r0.append(jnp.sum(jnp.where(oh0, before, 0.0), axis=0, keepdims=True))
        r1.append(jnp.sum(jnp.where(oh1, before, 0.0), axis=0, keepdims=True))
        carry = carry + jnp.sum(both, axis=1, keepdims=True)
    carry_ref[...] = carry
    counts_ref[...] = carry.astype(I32)
    tm = n_groups * LANES
    rank = jnp.concatenate([jnp.concatenate(r0, axis=1), jnp.concatenate(r1, axis=1),
                            jnp.zeros((SUBLANES - MOE_TOPK, tm), F32)], axis=0)
    rank_ref[...] = rank.astype(I32)


def _slot_ranks(rpt, tm=2048):
    T = rpt.shape[1]
    tm = min(tm, T)
    tri = (lax.broadcasted_iota(I32, (LANES, LANES), 0) < lax.broadcasted_iota(I32, (LANES, LANES), 1)).astype(BF16)
    return pl.pallas_call(
        functools.partial(_rank_body, n_groups=tm // LANES),
        grid=(T // tm,),
        in_specs=[pl.BlockSpec((SUBLANES, tm), lambda i: (0, i)), pl.BlockSpec((LANES, LANES), lambda i: (0, 0))],
        out_specs=[pl.BlockSpec((SUBLANES, tm), lambda i: (0, i)), pl.BlockSpec((MOE_EXPERTS, LANES), lambda i: (0, 0))],
        out_shape=[jax.ShapeDtypeStruct((SUBLANES, T), I32), jax.ShapeDtypeStruct((MOE_EXPERTS, LANES), I32)],
        scratch_shapes=[pltpu.VMEM((MOE_EXPERTS, LANES), F32)],
        compiler_params=_cparams("arbitrary"),
        name="slot_ranks",
    )(rpt, tri)


def _moe_dispatch_plan(rpt):
    T = rpt.shape[1]
    A = T * MOE_TOPK
    n_blocks = -(-(A + MOE_EXPERTS * (MOE_BLOCK - 1)) // MOE_BLOCK)
    n_blocks = -(-n_blocks // MOE_STEP_BLOCKS) * MOE_STEP_BLOCKS
    rank, counts = _slot_ranks(rpt)
    counts = counts[:, 0]
    padded = ((counts + MOE_BLOCK - 1) // MOE_BLOCK) * MOE_BLOCK
    pends = jnp.cumsum(padded)
    pstarts = pends - padded
    e = rpt[:MOE_TOPK].astype(I32)
    experts = jnp.arange(MOE_EXPERTS, dtype=I32)[:, None, None]
    dest = jnp.sum(jnp.where(e[None] == experts, pstarts[:, None, None], 0), axis=0) + rank[:MOE_TOPK]
    block_start = jnp.arange(n_blocks, dtype=I32) * MOE_BLOCK
    block_e = jnp.minimum(jnp.sum(pends[None, :] <= block_start[:, None], axis=1), MOE_EXPERTS - 1).astype(I32)
    block_valid = (block_start < pends[-1]).astype(I32)
    return dest, block_e, block_valid, n_blocks * MOE_BLOCK


SC_CORES = 2
SC_SUBCORES = 16
SC_WORKERS = SC_CORES * SC_SUBCORES
SC_ROWS = 64


def _sc_mesh():
    return plsc.VectorSubcoreMesh(core_axis_name="c", subcore_axis_name="s")


def _sc_scratch(n_chunks, width):
    return [pltpu.VMEM((n_chunks, SC_ROWS), I32), pltpu.VMEM((n_chunks, SC_ROWS), I32),
            pltpu.VMEM((SC_ROWS, width), U32), pltpu.VMEM((SC_ROWS, width), U32),
            pltpu.SemaphoreType.DMA, pltpu.SemaphoreType.DMA, pltpu.SemaphoreType.DMA, pltpu.SemaphoreType.DMA]


def _sc_scatter_rows(x, dest, P):
    T, W = x.shape
    tw = T // SC_WORKERS
    nch = tw // SC_ROWS
    assert tw * SC_WORKERS == T and nch * SC_ROWS == tw and nch % 2 == 0
    idx = dest.reshape(MOE_TOPK, SC_WORKERS, nch, SC_ROWS)

    @functools.partial(pl.kernel, mesh=_sc_mesh(), out_type=jax.ShapeDtypeStruct((P, W), U32),
                       scratch_types=_sc_scratch(nch, W))
    def scatter(x_hbm, i0_hbm, i1_hbm, o_hbm, i0_v, i1_v, b0, b1, r0, r1, s0, s1):
        wid = lax.axis_index("s") * SC_CORES + lax.axis_index("c")
        base = wid * tw
        pltpu.sync_copy(i0_hbm.at[wid], i0_v)
        pltpu.sync_copy(i1_hbm.at[wid], i1_v)

        @pl.loop(0, nch, step=2)
        def _(j):
            ca = pltpu.async_copy(x_hbm.at[pl.ds(base + j * SC_ROWS, SC_ROWS)], b0, r0)
            cb = pltpu.async_copy(x_hbm.at[pl.ds(base + (j + 1) * SC_ROWS, SC_ROWS)], b1, r1)
            ca.wait()
            a0 = pltpu.async_copy(b0, o_hbm.at[i0_v.at[j]], s0)
            a1 = pltpu.async_copy(b0, o_hbm.at[i1_v.at[j]], s0)
            cb.wait()
            e0 = pltpu.async_copy(b1, o_hbm.at[i0_v.at[j + 1]], s1)
            e1 = pltpu.async_copy(b1, o_hbm.at[i1_v.at[j + 1]], s1)
            a0.wait()
            a1.wait()
            e0.wait()
            e1.wait()

    return scatter(x, idx[0], idx[1])


def _sc_gather_rows(table, dest):
    T = dest.shape[1]
    W = table.shape[1]
    tw = T // SC_WORKERS
    nch = tw // SC_ROWS
    assert tw * SC_WORKERS == T and nch * SC_ROWS == tw
    idx = dest.reshape(MOE_TOPK, SC_WORKERS, nch, SC_ROWS)
    out = jax.ShapeDtypeStruct((T, W), U32)

    @functools.partial(pl.kernel, mesh=_sc_mesh(), out_type=[out, out], scratch_types=_sc_scratch(nch, W))
    def gather(tab_hbm, i0_hbm, i1_hbm, o0_hbm, o1_hbm, i0_v, i1_v, b0, b1, g0, g1, w0, w1):
        wid = lax.axis_index("s") * SC_CORES + lax.axis_index("c")
        base = wid * tw
        pltpu.sync_copy(i0_hbm.at[wid], i0_v)
        pltpu.sync_copy(i1_hbm.at[wid], i1_v)

        @pl.loop(0, nch)
        def _(j):
            rows = pl.ds(base + j * SC_ROWS, SC_ROWS)
            c0 = pltpu.async_copy(tab_hbm.at[i0_v.at[j]], b0, g0)
            c1 = pltpu.async_copy(tab_hbm.at[i1_v.at[j]], b1, g1)
            c0.wait()
            d0 = pltpu.async_copy(b0, o0_hbm.at[rows], w0)
            c1.wait()
            d1 = pltpu.async_copy(b1, o1_hbm.at[rows], w1)
            d0.wait()
            d1.wait()

    return gather(table, idx[0], idx[1])


def _moe(xn, rpt, w_gate_up, w_down, layer):
    dest, block_e, block_valid, P = _moe_dispatch_plan(rpt)
    xs = _sc_scatter_rows(xn, dest, P)
    y = _moe_experts(xs, block_e, block_valid, w_gate_up, w_down, layer)
    return _sc_gather_rows(y, dest)


def _final_body(h_ref, y0_ref, y1_ref, rp_ref, g_ref, o_ref):
    h = _combine(h_ref[...], y0_ref[...], y1_ref[...], rp_ref[...])
    o_ref[...] = h * lax.rsqrt(jnp.mean(h * h, axis=-1, keepdims=True) + NORM_EPS) * g_ref[...]


def _final(h, y0, y1, rp, gain, tm=1024):
    T, D = h.shape
    tm = min(tm, T)
    row = lambda i: (i, 0)
    return pl.pallas_call(
        _final_body,
        grid=(T // tm,),
        in_specs=[pl.BlockSpec((tm, D), row), pl.BlockSpec((tm, D // 2), row), pl.BlockSpec((tm, D // 2), row),
                  pl.BlockSpec((tm, LANES), row), pl.BlockSpec((1, D), lambda i: (0, 0))],
        out_specs=pl.BlockSpec((tm, D), row),
        out_shape=jax.ShapeDtypeStruct((T, D), F32),
        compiler_params=_cparams("parallel"),
        name="final_norm",
    )(h, y0, y1, rp, gain.reshape(1, D))


def _rope_tables(n_tok, head_dim):
    t = jnp.arange(n_tok, dtype=jnp.int32)
    row = (t // GRID_W).astype(F32)
    col = (t % GRID_W).astype(F32)
    half = head_dim // 2
    inv = ROPE_THETA ** (-jnp.arange(0, half, 2, dtype=F32) / half)
    ang_r = row[:, None] * inv[None, :]
    ang_c = col[:, None] * inv[None, :]
    ang = jnp.concatenate([ang_r, ang_r, ang_c, ang_c], axis=-1)
    q = half // 2
    sign = jnp.where((jnp.arange(head_dim) % half) < q, -1.0, 1.0).astype(F32)
    return jnp.cos(ang), jnp.sin(ang) * sign[None, :]


def _router_weights(w_group, b_group, w_expert, b_expert):
    D = w_group.shape[0]
    pad = LANES - MOE_EXPERTS - MOE_GROUPS
    w = jnp.concatenate([w_expert, w_group, jnp.zeros((D, pad), F32)], axis=1).astype(F32)
    b = jnp.concatenate([b_expert, b_group, jnp.zeros((pad,), F32)]).reshape(1, LANES)
    w_hi = w.astype(BF16)
    w_lo = (w - w_hi.astype(F32)).astype(BF16)
    return jnp.concatenate([w_hi, w_lo], axis=1), b.astype(F32)


def _trunk(x, p):
    B, S, D = x.shape
    T = B * S
    cos_r, sin_r = _rope_tables(S, RET_QK_DIM)
    cos_a, sin_a = [jnp.take(t, jnp.asarray(ATTN_HEAD_PERM, I32), axis=1) for t in _rope_tables(S, ATTN_HEAD_DIM)]
    h = x.reshape(T, D)

    proj = _in_proj(h, p["ln_mix"][0], p["ret_w_in"][0])
    y = _retention(proj, p["ret_decay_logit"][0], cos_r, sin_r, B, S)
    h, xn, rp, rpt = _out_proj(y, h, p["ret_w_out"][0], p["ln_ffn"][0], *p["router"][0], gate_src=proj)
    y0, y1 = _moe(xn, rpt, p["moe_w_gate_up"], p["moe_w_down"], 0)

    h, qn, kn, vx = _attn_in_proj(h, p["ln_mix"][1], p["attn_w_in"][0], (y0, y1, rp),
                                  p["attn_q_gain"][0], p["attn_k_gain"][0], cos_a, sin_a, S)
    y = _attention(qn, kn, vx, B, S)
    h, xn, rp, rpt = _out_proj(y, h, p["attn_w_out"][0], p["ln_ffn"][1], *p["router"][1])
    y0, y1 = _moe(xn, rpt, p["moe_w_gate_up"], p["moe_w_down"], 1)

    return _final(h, y0, y1, rp, p["ln_final"]).reshape(B, S, D)


def _prepare_params(ln_mix, ret_w_in, ret_decay_logit, ret_w_out, attn_w_in, attn_q_gain, attn_k_gain,
                    attn_w_out, ln_ffn, moe_w_group, moe_b_group, moe_w_expert, moe_b_expert, moe_w_gate_up,
                    moe_w_down, ln_final):
    d = ATTN_HEAD_DIM
    perm = jnp.asarray(ATTN_HEAD_PERM, I32)
    qk_cols = (jnp.arange((ATTN_Q_HEADS + ATTN_KV_HEADS) * d, dtype=I32) // d) * d
    qk_cols = qk_cols + jnp.tile(perm, ATTN_Q_HEADS + ATTN_KV_HEADS)
    cols = jnp.concatenate([qk_cols, jnp.arange(qk_cols.shape[0], ATTN_IN_WIDTH, dtype=I32)])
    return {
        "ln_mix": ln_mix, "ln_ffn": ln_ffn, "ln_final": ln_final,
        "ret_w_in": ret_w_in.astype(BF16), "ret_decay_logit": ret_decay_logit, "ret_w_out": ret_w_out.astype(BF16),
        "attn_w_in": jnp.take(attn_w_in, cols, axis=2).astype(BF16),
        "attn_q_gain": jnp.take(attn_q_gain, perm, axis=1), "attn_k_gain": jnp.take(attn_k_gain, perm, axis=1),
        "attn_w_out": attn_w_out.astype(BF16),
        "router": [_router_weights(moe_w_group[i], moe_b_group[i], moe_w_expert[i], moe_b_expert[i])
                   for i in range(moe_w_group.shape[0])],
        "moe_w_gate_up": moe_w_gate_up, "moe_w_down": moe_w_down,
    }


def kernel(x_prompt, x_sample, ln_mix, ret_w_in, ret_decay_logit, ret_w_out, attn_w_in, attn_q_gain, attn_k_gain,
           attn_w_out, ln_ffn, moe_w_group, moe_b_group, moe_w_expert, moe_b_expert, moe_w_gate_up, moe_w_down,
           ln_final):
    p = _prepare_params(ln_mix, ret_w_in, ret_decay_logit, ret_w_out, attn_w_in, attn_q_gain, attn_k_gain,
                        attn_w_out, ln_ffn, moe_w_group, moe_b_group, moe_w_expert, moe_b_expert, moe_w_gate_up,
                        moe_w_down, ln_final)
    return _trunk(x_prompt, p), _trunk(x_sample, p)
```

```python
import functools

import jax
import jax.numpy as jnp
from jax import lax
from jax.experimental import pallas as pl
from jax.experimental.pallas import tpu as pltpu
from jax.experimental.pallas import tpu_sc as plsc

F32 = jnp.float32
BF16 = jnp.bfloat16
U32 = jnp.uint32
I32 = jnp.int32

GRID_W = 64
ROPE_THETA = 10000.0
NORM_EPS = 1e-6

RET_HEADS = 4
RET_QK_DIM = 256
RET_V_DIM = 512
RET_SCAN_CHUNK = 256
RET_QK_WIDTH = RET_HEADS * RET_QK_DIM
RET_V_WIDTH = RET_HEADS * RET_V_DIM

ATTN_Q_HEADS = 8
ATTN_KV_HEADS = 2
ATTN_HEAD_DIM = 128
ATTN_GROUP = ATTN_Q_HEADS // ATTN_KV_HEADS
ATTN_IN_WIDTH = (ATTN_Q_HEADS + 2 * ATTN_KV_HEADS) * ATTN_HEAD_DIM

MOE_GROUPS = 4
MOE_EXPERTS_PER_GROUP = 8
MOE_EXPERTS = MOE_GROUPS * MOE_EXPERTS_PER_GROUP
MOE_TOPK = 2
MOE_FF = 512

LANES = 128
SUBLANES = 8
LOG2_E = 1.4426950408889634
VMEM_LIMIT = 56 * 1024 * 1024

IN_PROJ_ROWS = 512
IN_PROJ_SUB_ROWS = 256
ATTN_IN_PROJ_ROWS = 1024
RET_SEQ_BLOCK = 4096
RET_UNROLL = 8
ATTN_Q_TILE = 1024
ATTN_KV_TILE = 256
ATTN_KV_UNROLL = 16
OUT_PROJ_TILE_ELEMS = 2048 * 1024
OUT_PROJ_SUB_ROWS = 1024
MOE_BLOCK = 512
MOE_STEP_BLOCKS = 2
RANK_TOKENS = 2048
FINAL_ROWS = 1024
SC_ROWS = 64


def _cparams(*sem):
    return pltpu.CompilerParams(dimension_semantics=sem, vmem_limit_bytes=VMEM_LIMIT)


def _pack_bf16_pairs(x):
    w = x.shape[1] // 2
    lo = lax.bitcast_convert_type(x[:, :w].astype(BF16).astype(F32), U32)
    hi = lax.bitcast_convert_type(x[:, w:].astype(BF16).astype(F32), U32)
    return lax.shift_right_logical(lo, jnp.uint32(16)) | (hi & jnp.uint32(0xFFFF0000))


def _unpack_bf16_pairs(u):
    lo = lax.bitcast_convert_type(lax.shift_left(u, jnp.uint32(16)), F32)
    hi = lax.bitcast_convert_type(u & jnp.uint32(0xFFFF0000), F32)
    return lo, hi


def _combine(h, y0_u32, y1_u32, rp):
    lo0, hi0 = _unpack_bf16_pairs(y0_u32)
    lo1, hi1 = _unpack_bf16_pairs(y1_u32)
    w0 = rp[:, 2:3]
    w1 = rp[:, 3:4]
    half = h.shape[1] // 2
    return jnp.concatenate([h[:, :half] + w0 * lo0 + w1 * lo1, h[:, half:] + w0 * hi0 + w1 * hi1], axis=1)


def _rope256(x, cos, sin_signed):
    xr = jnp.concatenate([pltpu.roll(x[:, :LANES], 64, 1), pltpu.roll(x[:, LANES:], 64, 1)], axis=1)
    return x * cos + xr * sin_signed


def _in_proj_body(h_ref, g_ref, w_ref, cos_ref, sin_ref, o_ref):
    tm = h_ref.shape[0]
    sub = min(IN_PROJ_SUB_ROWS, tm)
    dk = RET_QK_DIM
    for s in range(tm // sub):
        rows = slice(s * sub, (s + 1) * sub)
        h = h_ref[rows, :]
        xn = (h * lax.rsqrt(jnp.mean(h * h, axis=-1, keepdims=True) + NORM_EPS) * g_ref[...]).astype(BF16)
        acc = jnp.dot(xn, w_ref[...], preferred_element_type=F32)
        cos = cos_ref[rows, :]
        sin = sin_ref[rows, :]
        for hb in range(2 * RET_HEADS):
            c = slice(hb * dk, (hb + 1) * dk)
            o_ref[rows, c] = _rope256(acc[:, c], cos, sin).astype(o_ref.dtype)
        o_ref[rows, 2 * RET_QK_WIDTH:] = acc[:, 2 * RET_QK_WIDTH:].astype(o_ref.dtype)


def _in_proj(h, gain, w, cos, sin_signed, S, tm=IN_PROJ_ROWS):
    T, D = h.shape
    N = w.shape[1]
    tm = min(tm, S)
    spb = S // tm
    tab = lambda i: (i % spb, 0)
    return pl.pallas_call(
        _in_proj_body,
        grid=(T // tm,),
        in_specs=[pl.BlockSpec((tm, D), lambda i: (i, 0)), pl.BlockSpec((1, D), lambda i: (0, 0)),
                  pl.BlockSpec((D, N), lambda i: (0, 0)),
                  pl.BlockSpec((tm, RET_QK_DIM), tab), pl.BlockSpec((tm, RET_QK_DIM), tab)],
        out_specs=pl.BlockSpec((tm, N), lambda i: (i, 0)),
        out_shape=jax.ShapeDtypeStruct((T, N), BF16),
        compiler_params=_cparams("parallel"),
        name="in_proj",
    )(h, gain.reshape(1, D), w, cos, sin_signed)


def _retention_body(dl_ref, q_ref, k_ref, v_ref, y_ref, state_ref, obwd_ref, *, n_chunks, n_blocks):
    C = RET_SCAN_CHUNK
    h = pl.program_id(1)
    phase = pl.program_id(2)
    n = pl.program_id(3)
    sb = n_chunks * C

    def log_gamma(s):
        x = jnp.full((1, 1), s, F32)
        return jnp.minimum(x, 0.0) - jnp.log(1.0 + jnp.exp(-jnp.abs(x)))

    lg_f = log_gamma(dl_ref[h])
    lg_b = log_gamma(dl_ref[RET_HEADS + h])
    pos = lax.broadcasted_iota(jnp.int32, (C, RET_QK_DIM), 0).astype(F32)
    k_scale = RET_QK_DIM ** -0.5

    @pl.when(n == 0)
    def _():
        state_ref[...] = jnp.zeros_like(state_ref)

    def load(rows, q_dec, k_dec):
        q = q_ref[rows, :]
        k = k_ref[rows, :]
        return q, k, v_ref[rows, :], q * q_dec, (k * k_dec).T

    @pl.when(phase == 0)
    def _():
        q_dec = jnp.exp((C - pos) * lg_b).astype(BF16)
        k_dec = (jnp.exp(pos * lg_b) * k_scale).astype(BF16)
        chunk_dec = jnp.exp(lg_b * C)
        base = (n_blocks - 1 - n) * sb

        def body(cc, carry):
            c = n_chunks - 1 - cc
            rows = pl.ds(pl.multiple_of(c * C, C), C)
            _, _, v, qd, kt = load(rows, q_dec, k_dec)
            st = state_ref[...]
            out = jnp.dot(qd, st.astype(BF16), preferred_element_type=F32)
            state_ref[...] = st * chunk_dec + jnp.dot(kt, v, preferred_element_type=F32)
            obwd_ref[pl.ds(pl.multiple_of(base + c * C, C), C), :] = out.astype(obwd_ref.dtype)
            return carry

        lax.fori_loop(0, n_chunks, body, 0, unroll=min(RET_UNROLL, n_chunks))

    @pl.when(phase == 1)
    def _():
        q_dec = jnp.exp((pos + 1.0) * lg_f).astype(BF16)
        k_dec = (jnp.exp((C - 1.0 - pos) * lg_f) * k_scale).astype(BF16)
        chunk_dec = jnp.exp(lg_f * C)
        ii = lax.broadcasted_iota(jnp.int32, (C, C), 0)
        jj = lax.broadcasted_iota(jnp.int32, (C, C), 1)
        dist = (ii - jj).astype(F32)
        inner = (jnp.where(dist >= 0, jnp.exp(lg_f * jnp.maximum(dist, 0.0)), 0.0)
                 + jnp.where(dist <= 0, jnp.exp(lg_b * jnp.maximum(-dist, 0.0)), 0.0)) * k_scale
        base = n * sb

        def body(c, carry):
            rows = pl.ds(pl.multiple_of(c * C, C), C)
            q, k, v, qd, kt = load(rows, q_dec, k_dec)
            scores = lax.dot_general(q, k, (((1,), (1,)), ((), ())), preferred_element_type=F32) * inner
            st = state_ref[...]
            lhs = jnp.concatenate([qd, scores.astype(BF16)], axis=1)
            rhs = jnp.concatenate([st.astype(BF16), v], axis=0)
            o = jnp.dot(lhs, rhs, preferred_element_type=F32)
            state_ref[...] = st * chunk_dec + jnp.dot(kt, v, preferred_element_type=F32)
            o = o + obwd_ref[pl.ds(pl.multiple_of(base + c * C, C), C), :].astype(F32)
            y_ref[rows, :] = o.astype(y_ref.dtype)
            return carry

        lax.fori_loop(0, n_chunks, body, 0, unroll=min(RET_UNROLL, n_chunks))


def _retention(proj, decay_logit, B, S, sb=RET_SEQ_BLOCK):
    sb = min(sb, S)
    nb = S // sb
    proj3 = proj.reshape(B, S, proj.shape[-1])
    blk = lambda p, n: jnp.where(p == 0, nb - 1 - n, n)
    kq = RET_QK_WIDTH // RET_QK_DIM
    kv = 2 * RET_QK_WIDTH // RET_V_DIM
    in_specs = [
        pl.BlockSpec(memory_space=pltpu.SMEM),
        pl.BlockSpec((None, sb, RET_QK_DIM), lambda b, h, p, n: (b, blk(p, n), h)),
        pl.BlockSpec((None, sb, RET_QK_DIM), lambda b, h, p, n: (b, blk(p, n), kq + h)),
        pl.BlockSpec((None, sb, RET_V_DIM), lambda b, h, p, n: (b, blk(p, n), kv + h)),
    ]
    out = pl.pallas_call(
        functools.partial(_retention_body, n_chunks=sb // RET_SCAN_CHUNK, n_blocks=nb),
        grid=(B, RET_HEADS, 2, nb),
        in_specs=in_specs,
        out_specs=pl.BlockSpec((None, sb, RET_V_DIM), lambda b, h, p, n: (b, n * p, h)),
        out_shape=jax.ShapeDtypeStruct((B, S, RET_V_WIDTH), BF16),
        scratch_shapes=[pltpu.VMEM((RET_QK_DIM, RET_V_DIM), F32), pltpu.VMEM((S, RET_V_DIM), BF16)],
        compiler_params=_cparams("parallel", "parallel", "arbitrary", "arbitrary"),
        name="retention",
    )(decay_logit.reshape(2 * RET_HEADS).astype(F32), proj3, proj3, proj3)
    return out.reshape(B * S, RET_V_WIDTH)


ATTN_HEAD_PERM = tuple(list(range(0, 32)) + list(range(64, 96)) + list(range(32, 64)) + list(range(96, 128)))


def _norm_rope_head_pair(x, gain2, cos2, sin2):
    d = ATTN_HEAD_DIM
    bi = lax.broadcasted_iota(I32, (2 * d, 2 * d), 0) // d
    bj = lax.broadcasted_iota(I32, (2 * d, 2 * d), 1) // d
    ones_bd = jnp.where(bi == bj, 1.0, 0.0).astype(BF16)
    ss = jnp.dot((x * x).astype(BF16), ones_bd, preferred_element_type=F32)
    x = x * lax.rsqrt(ss * (1.0 / d) + NORM_EPS) * gain2
    return _rope256(x, cos2, sin2)


def _attn_in_proj_body(h_ref, y0_ref, y1_ref, rp_ref, g_ref, w_ref, qg_ref, kg_ref, cos_ref, sin_ref,
                       hout_ref, q_ref, k_ref, v_ref):
    d = ATTN_HEAD_DIM
    tm = h_ref.shape[0]
    sub = min(IN_PROJ_SUB_ROWS, tm)
    q_scale = (d ** -0.5) * LOG2_E
    for s in range(tm // sub):
        rows = slice(s * sub, (s + 1) * sub)
        h = _combine(h_ref[rows, :], y0_ref[rows, :], y1_ref[rows, :], rp_ref[rows, :])
        hout_ref[rows, :] = h
        xn = (h * lax.rsqrt(jnp.mean(h * h, axis=-1, keepdims=True) + NORM_EPS) * g_ref[...]).astype(BF16)
        acc = jnp.dot(xn, w_ref[...], preferred_element_type=F32)
        cos2 = jnp.concatenate([cos_ref[rows, :]] * 2, axis=1)
        sin2 = jnp.concatenate([sin_ref[rows, :]] * 2, axis=1)
        qg2 = jnp.concatenate([qg_ref[...]] * 2, axis=1) * q_scale
        kg2 = jnp.concatenate([kg_ref[...]] * 2, axis=1)
        for hp in range(ATTN_Q_HEADS // 2):
            c = slice(2 * hp * d, (2 * hp + 2) * d)
            q = _norm_rope_head_pair(acc[:, c], qg2, cos2, sin2).astype(q_ref.dtype)
            q_ref[2 * hp, rows, :] = q[:, :d]
            q_ref[2 * hp + 1, rows, :] = q[:, d:]
        for hp in range(ATTN_KV_HEADS // 2):
            c0 = (ATTN_Q_HEADS + 2 * hp) * d
            k = _norm_rope_head_pair(acc[:, c0:c0 + 2 * d], kg2, cos2, sin2)
            k_ref[rows, 2 * hp * d:(2 * hp + 2) * d] = k.astype(k_ref.dtype)
        for hk in range(ATTN_KV_HEADS):
            c1 = (ATTN_Q_HEADS + ATTN_KV_HEADS + hk) * d
            v_ref[rows, 2 * hk * d:(2 * hk + 1) * d] = acc[:, c1:c1 + d].astype(v_ref.dtype)
            v_ref[rows, (2 * hk + 1) * d:(2 * hk + 2) * d] = jnp.ones((sub, d), v_ref.dtype)


def _attn_in_proj(h, gain, w, comb, q_gain, k_gain, cos, sin_signed, S, tm=ATTN_IN_PROJ_ROWS):
    T, D = h.shape
    N = w.shape[1]
    tm = min(tm, S)
    spb = S // tm
    d = ATTN_HEAD_DIM
    y0, y1, rp = comb
    row = lambda i: (i, 0)
    const = lambda i: (0, 0)
    tab = lambda i: (i % spb, 0)
    return pl.pallas_call(
        _attn_in_proj_body,
        grid=(T // tm,),
        in_specs=[pl.BlockSpec((tm, D), row), pl.BlockSpec((tm, D // 2), row), pl.BlockSpec((tm, D // 2), row),
                  pl.BlockSpec((tm, LANES), row), pl.BlockSpec((1, D), const), pl.BlockSpec((D, N), const),
                  pl.BlockSpec((1, d), const), pl.BlockSpec((1, d), const),
                  pl.BlockSpec((tm, d), tab), pl.BlockSpec((tm, d), tab)],
        out_specs=[pl.BlockSpec((tm, D), row), pl.BlockSpec((ATTN_Q_HEADS, tm, d), lambda i: (0, i, 0)),
                   pl.BlockSpec((tm, ATTN_KV_HEADS * d), row), pl.BlockSpec((tm, 2 * ATTN_KV_HEADS * d), row)],
        out_shape=[jax.ShapeDtypeStruct((T, D), F32), jax.ShapeDtypeStruct((ATTN_Q_HEADS, T, d), BF16),
                   jax.ShapeDtypeStruct((T, ATTN_KV_HEADS * d), BF16),
                   jax.ShapeDtypeStruct((T, 2 * ATTN_KV_HEADS * d), BF16)],
        compiler_params=_cparams("parallel"),
        name="attn_in_proj",
    )(h, y0, y1, rp, gain.reshape(1, D), w, q_gain.reshape(1, d), k_gain.reshape(1, d), cos, sin_signed)


def _attn_body(q_ref, k_ref, v_ref, o_ref, m_ref, acc_ref, *, tq, tk, n_kv):
    d = ATTN_HEAD_DIM
    m_ref[...] = jnp.full_like(m_ref, -jnp.inf)
    acc_ref[...] = jnp.zeros_like(acc_ref)
    rep = tk // LANES

    def step(c, carry):
        rows = pl.ds(pl.multiple_of(c * tk, tk), tk)
        k = k_ref[rows, :]
        v = v_ref[rows, :]
        for g in range(ATTN_GROUP):
            r = slice(g * tq, (g + 1) * tq)
            s = lax.dot_general(q_ref[g], k, (((1,), (1,)), ((), ())), preferred_element_type=F32)
            m_prev = m_ref[r, :]
            m_new = jnp.maximum(m_prev, jnp.max(s, axis=-1, keepdims=True))
            alpha = jnp.exp2(m_prev - m_new)
            p = jnp.exp2(s - jnp.concatenate([m_new] * rep, axis=1))
            pv = jnp.dot(p.astype(BF16), v, preferred_element_type=F32)
            acc_ref[r, :] = jnp.concatenate([alpha, alpha], axis=1) * acc_ref[r, :] + pv
            m_ref[r, :] = m_new
        return carry

    lax.fori_loop(0, n_kv, step, 0, unroll=min(ATTN_KV_UNROLL, n_kv))
    for g in range(ATTN_GROUP):
        r = slice(g * tq, (g + 1) * tq)
        o_ref[:, g * d:(g + 1) * d] = (acc_ref[r, :d] / acc_ref[r, d:]).astype(o_ref.dtype)


def _attention(qn, kn, vx, B, S, tq=ATTN_Q_TILE, tk=ATTN_KV_TILE):
    d = ATTN_HEAD_DIM
    tq = min(tq, S)
    tk = min(tk, S)
    gw = ATTN_GROUP * d
    nq = S // tq
    k3 = kn.reshape(B, S, ATTN_KV_HEADS * d)
    v3 = vx.reshape(B, S, 2 * ATTN_KV_HEADS * d)
    m = ATTN_GROUP * tq
    out = pl.pallas_call(
        functools.partial(_attn_body, tq=tq, tk=tk, n_kv=S // tk),
        grid=(B, ATTN_KV_HEADS, S // tq),
        in_specs=[pl.BlockSpec((ATTN_GROUP, tq, d), lambda b, kh, i: (kh, b * nq + i, 0)),
                  pl.BlockSpec((None, S, d), lambda b, kh, i: (b, 0, kh)),
                  pl.BlockSpec((None, S, 2 * d), lambda b, kh, i: (b, 0, kh))],
        out_specs=pl.BlockSpec((None, tq, gw), lambda b, kh, i: (b, i, kh)),
        out_shape=jax.ShapeDtypeStruct((B, S, ATTN_Q_HEADS * d), BF16),
        scratch_shapes=[pltpu.VMEM((m, LANES), F32), pltpu.VMEM((m, 2 * d), F32)],
        compiler_params=_cparams("parallel", "parallel", "arbitrary"),
        name="attention",
    )(qn, k3, v3)
    return out.reshape(B * S, ATTN_Q_HEADS * d)


ROUTER_GROUP_ROW = MOE_EXPERTS
assert MOE_EXPERTS_PER_GROUP == SUBLANES and MOE_GROUPS <= SUBLANES


def _route(logits_t):
    tm = logits_t.shape[1]
    row = lax.broadcasted_iota(jnp.int32, (SUBLANES, tm), 0)
    neg = jnp.float32(-jnp.inf)
    big = jnp.int32(SUBLANES)

    def first_argmax(x):
        mx = jnp.max(x, axis=0, keepdims=True)
        return mx, jnp.min(jnp.where(x == mx, row, big), axis=0, keepdims=True)

    gl = jnp.where(row < MOE_GROUPS, logits_t[ROUTER_GROUP_ROW:ROUTER_GROUP_ROW + SUBLANES, :], neg)
    gmax, g_sel = first_argmax(gl)
    g_w = 1.0 / jnp.sum(jnp.exp(gl - gmax), axis=0, keepdims=True)
    el = logits_t[0:SUBLANES, :]
    for g in range(1, MOE_GROUPS):
        el = jnp.where(g_sel == g, logits_t[g * SUBLANES:(g + 1) * SUBLANES, :], el)
    v1, i1 = first_argmax(el)
    v2, i2 = first_argmax(jnp.where(row == i1, neg, el))
    e2 = jnp.exp(v2 - v1)
    w1 = g_w / (1.0 + e2)
    w2 = g_w * e2 / (1.0 + e2)
    base = g_sel * MOE_EXPERTS_PER_GROUP
    return jnp.where(row == 0, (base + i1).astype(F32),
                     jnp.where(row == 1, (base + i2).astype(F32),
                               jnp.where(row == 2, w1, jnp.where(row == 3, w2, 0.0))))


def _retention_gate(o, g):
    dv = RET_V_DIM
    parts = []
    for hh in range(RET_HEADS):
        oh = o[:, hh * dv:(hh + 1) * dv].astype(F32)
        oh = oh * lax.rsqrt(jnp.mean(oh * oh, axis=-1, keepdims=True) + NORM_EPS)
        gh = g[:, hh * dv:(hh + 1) * dv]
        parts.append(gh * jax.nn.sigmoid(gh) * oh.astype(BF16))
    return jnp.concatenate(parts, axis=1)


def _out_proj_body(gated, y_ref, *refs):
    if gated:
        gate_ref, refs = refs[0], refs[1:]
    h_ref, w_ref, g_ref, wr_ref, br_ref, hout_ref, xn_ref, rp_ref, rpt_ref = refs
    tm = h_ref.shape[0]
    sub = min(OUT_PROJ_SUB_ROWS, tm)
    wr = wr_ref[...]
    for s in range(tm // sub):
        rows = slice(s * sub, (s + 1) * sub)
        y = _retention_gate(y_ref[rows, :], gate_ref[rows, :]) if gated else y_ref[rows, :]
        h = h_ref[rows, :] + jnp.dot(y, w_ref[...], preferred_element_type=F32)
        hout_ref[rows, :] = h
        xn = h * lax.rsqrt(jnp.mean(h * h, axis=-1, keepdims=True) + NORM_EPS) * g_ref[...]
        xn_ref[rows, :] = _pack_bf16_pairs(xn)
        xn_hi = xn.astype(BF16)
        xn_lo = (xn - xn_hi.astype(F32)).astype(BF16)
        hi = jnp.dot(xn_hi, wr, preferred_element_type=F32)
        lo = jnp.dot(xn_lo, wr[:, :LANES], preferred_element_type=F32)
        logits = hi[:, :LANES] + hi[:, LANES:] + lo + br_ref[...]
        rpt = _route(logits.T)
        rpt_ref[:, rows] = rpt
        full = jnp.concatenate([rpt, jnp.zeros((LANES - SUBLANES, sub), F32)], axis=0)
        rp_ref[rows, :] = full.T


def _out_proj(y, h, w, gain, w_router, b_router, gate_src=None, tm=None):
    T, K = y.shape
    D = h.shape[1]
    tm = min(OUT_PROJ_TILE_ELEMS // K if tm is None else tm, T)
    row = lambda i: (i, 0)
    const = lambda i: (0, 0)
    in_specs = [pl.BlockSpec((tm, K), row)]
    args = [y]
    if gate_src is not None:
        assert K == RET_V_WIDTH and (2 * RET_QK_WIDTH + RET_V_WIDTH) % K == 0
        gate_col = (2 * RET_QK_WIDTH + RET_V_WIDTH) // K
        in_specs.append(pl.BlockSpec((tm, K), lambda i: (i, gate_col)))
        args.append(gate_src)
    in_specs += [pl.BlockSpec((tm, D), row), pl.BlockSpec((K, D), const), pl.BlockSpec((1, D), const),
                 pl.BlockSpec((D, 2 * LANES), const), pl.BlockSpec((1, LANES), const)]
    args += [h, w, gain.reshape(1, D), w_router, b_router]
    return pl.pallas_call(
        functools.partial(_out_proj_body, gate_src is not None),
        grid=(T // tm,),
        in_specs=in_specs,
        out_specs=[pl.BlockSpec((tm, D), row), pl.BlockSpec((tm, D // 2), row), pl.BlockSpec((tm, LANES), row),
                   pl.BlockSpec((SUBLANES, tm), lambda i: (0, i))],
        out_shape=[jax.ShapeDtypeStruct((T, D), F32), jax.ShapeDtypeStruct((T, D // 2), U32),
                   jax.ShapeDtypeStruct((T, LANES), F32), jax.ShapeDtypeStruct((SUBLANES, T), F32)],
        compiler_params=_cparams("parallel"),
        name="out_proj",
    )(*args)


def _moe_body(be_ref, bv_ref, x_ref, *refs):
    nb = MOE_STEP_BLOCKS
    wgu_refs, wd_refs = refs[:nb], refs[nb:2 * nb]
    y_ref = refs[2 * nb]
    wgu_bf_refs, wd_bf_refs = refs[2 * nb + 1:3 * nb + 1], refs[3 * nb + 1:]
    s = pl.program_id(0)

    for j in range(nb):
        b = s * nb + j

        @pl.when((s == 0) | (be_ref[b] != be_ref[jnp.maximum(b - nb, 0)]))
        def _():
            wgu_bf_refs[j][...] = wgu_refs[j][...].astype(BF16)
            wd_bf_refs[j][...] = wd_refs[j][...].astype(BF16)

    @pl.when(bv_ref[s * nb] != 0)
    def _():
        for j in range(nb):
            rows = slice(j * MOE_BLOCK, (j + 1) * MOE_BLOCK)
            lo, hi = _unpack_bf16_pairs(x_ref[rows, :])
            x = jnp.concatenate([lo.astype(BF16), hi.astype(BF16)], axis=1)
            gu = jnp.dot(x, wgu_bf_refs[j][...], preferred_element_type=F32)
            gate = gu[:, :MOE_FF]
            up = gu[:, MOE_FF:]
            act = (gate * jax.nn.sigmoid(gate) * up).astype(BF16)
            y_ref[rows, :] = _pack_bf16_pairs(jnp.dot(act, wd_bf_refs[j][...], preferred_element_type=F32))

    @pl.when(bv_ref[s * nb] == 0)
    def _():
        y_ref[...] = jnp.zeros_like(y_ref)


def _moe_experts(xs, block_e, block_valid, w_gate_up, w_down, layer):
    P, W = xs.shape
    D = 2 * W
    nb = MOE_STEP_BLOCKS
    n_steps = P // (nb * MOE_BLOCK)
    rows = lambda s, be, bv: (s, 0)
    wspec = lambda shape, j: pl.BlockSpec((None, None) + shape, lambda s, be, bv: (layer, be[s * nb + j], 0, 0))
    grid_spec = pltpu.PrefetchScalarGridSpec(
        num_scalar_prefetch=2,
        grid=(n_steps,),
        in_specs=([pl.BlockSpec((nb * MOE_BLOCK, W), rows)]
                  + [wspec((D, 2 * MOE_FF), j) for j in range(nb)] + [wspec((MOE_FF, D), j) for j in range(nb)]),
        out_specs=pl.BlockSpec((nb * MOE_BLOCK, W), rows),
        scratch_shapes=([pltpu.VMEM((D, 2 * MOE_FF), BF16)] * nb + [pltpu.VMEM((MOE_FF, D), BF16)] * nb),
    )
    return pl.pallas_call(
        _moe_body,
        grid_spec=grid_spec,
        out_shape=jax.ShapeDtypeStruct((P, W), U32),
        compiler_params=_cparams("arbitrary"),
        name="moe_experts",
    )(block_e, block_valid, xs, *([w_gate_up] * nb), *([w_down] * nb))


def _rank_body(rpt_ref, tri_ref, rank_ref, counts_ref, carry_ref, *, n_groups):
    @pl.when(pl.program_id(0) == 0)
    def _():
        carry_ref[...] = jnp.zeros_like(carry_ref)

    erow = lax.broadcasted_iota(I32, (MOE_EXPERTS, LANES), 0)
    carry = carry_ref[...]
    tri = tri_ref[...]
    r0, r1 = [], []
    for g in range(n_groups):
        c = slice(g * LANES, (g + 1) * LANES)
        oh0 = erow == rpt_ref[0:1, c].astype(I32)
        oh1 = erow == rpt_ref[1:2, c].astype(I32)
        both = jnp.where(oh0 | oh1, 1.0, 0.0)
        before = jnp.dot(both.astype(BF16), tri, preferred_element_type=F32) + carry
        r0.append(jnp.sum(jnp.where(oh0, before, 0.0), axis=0, keepdims=True))
        r1.append(jnp.sum(jnp.where(oh1, before, 0.0), axis=0, keepdims=True))
        carry = carry + jnp.sum(both, axis=1, keepdims=True)
    carry_ref[...] = carry
    counts_ref[...] = carry.astype(I32)
    tm = n_groups * LANES
    rank = jnp.concatenate([jnp.concatenate(r0, axis=1), jnp.concatenate(r1, axis=1),
                            jnp.zeros((SUBLANES - MOE_TOPK, tm), F32)], axis=0)
    rank_ref[...] = rank.astype(I32)


def _slot_ranks(rpt, tm=RANK_TOKENS):
    T = rpt.shape[1]
    tm = min(tm, T)
    tri = (lax.broadcasted_iota(I32, (LANES, LANES), 0) < lax.broadcasted_iota(I32, (LANES, LANES), 1)).astype(BF16)
    return pl.pallas_call(
        functools.partial(_rank_body, n_groups=tm // LANES),
        grid=(T // tm,),
        in_specs=[pl.BlockSpec((SUBLANES, tm), lambda i: (0, i)), pl.BlockSpec((LANES, LANES), lambda i: (0, 0))],
        out_specs=[pl.BlockSpec((SUBLANES, tm), lambda i: (0, i)),
                   pl.BlockSpec((MOE_EXPERTS, LANES), lambda i: (0, 0))],
        out_shape=[jax.ShapeDtypeStruct((SUBLANES, T), I32), jax.ShapeDtypeStruct((MOE_EXPERTS, LANES), I32)],
        scratch_shapes=[pltpu.VMEM((MOE_EXPERTS, LANES), F32)],
        compiler_params=_cparams("arbitrary"),
        name="slot_ranks",
    )(rpt, tri)


def _moe_dispatch_plan(rpt):
    T = rpt.shape[1]
    A = T * MOE_TOPK
    n_blocks = -(-(A + MOE_EXPERTS * (MOE_BLOCK - 1)) // MOE_BLOCK)
    n_blocks = -(-n_blocks // MOE_STEP_BLOCKS) * MOE_STEP_BLOCKS
    rank, counts = _slot_ranks(rpt)
    counts = counts[:, 0]
    padded = ((counts + MOE_BLOCK - 1) // MOE_BLOCK) * MOE_BLOCK
    pends = jnp.cumsum(padded)
    pstarts = pends - padded
    e = rpt[:MOE_TOPK].astype(I32)
    experts = jnp.arange(MOE_EXPERTS, dtype=I32)[:, None, None]
    dest = jnp.sum(jnp.where(e[None] == experts, pstarts[:, None, None], 0), axis=0) + rank[:MOE_TOPK]
    block_start = jnp.arange(n_blocks, dtype=I32) * MOE_BLOCK
    block_e = jnp.minimum(jnp.sum(pends[None, :] <= block_start[:, None], axis=1), MOE_EXPERTS - 1).astype(I32)
    block_valid = (block_start < pends[-1]).astype(I32)
    return dest, block_e, block_valid, n_blocks * MOE_BLOCK


SC_CORES = 2
SC_SUBCORES = 16
SC_WORKERS = SC_CORES * SC_SUBCORES


def _sc_mesh():
    return plsc.VectorSubcoreMesh(core_axis_name="c", subcore_axis_name="s")


def _sc_scratch(n_chunks, width):
    return [pltpu.VMEM((n_chunks, SC_ROWS), I32), pltpu.VMEM((n_chunks, SC_ROWS), I32),
            pltpu.VMEM((SC_ROWS, width), U32), pltpu.VMEM((SC_ROWS, width), U32),
            pltpu.SemaphoreType.DMA, pltpu.SemaphoreType.DMA, pltpu.SemaphoreType.DMA, pltpu.SemaphoreType.DMA]


def _sc_scatter_rows(x, dest, P):
    T, W = x.shape
    tw = T // SC_WORKERS
    nch = tw // SC_ROWS
    assert tw * SC_WORKERS == T and nch * SC_ROWS == tw and nch % 2 == 0
    idx = dest.reshape(MOE_TOPK, SC_WORKERS, nch, SC_ROWS)

    @functools.partial(pl.kernel, mesh=_sc_mesh(), out_type=jax.ShapeDtypeStruct((P, W), U32),
                       scratch_types=_sc_scratch(nch, W))
    def scatter(x_hbm, i0_hbm, i1_hbm, o_hbm, i0_v, i1_v, b0, b1, r0, r1, s0, s1):
        wid = lax.axis_index("s") * SC_CORES + lax.axis_index("c")
        base = wid * tw
        pltpu.sync_copy(i0_hbm.at[wid], i0_v)
        pltpu.sync_copy(i1_hbm.at[wid], i1_v)

        @pl.loop(0, nch, step=2)
        def _(j):
            ca = pltpu.async_copy(x_hbm.at[pl.ds(base + j * SC_ROWS, SC_ROWS)], b0, r0)
            cb = pltpu.async_copy(x_hbm.at[pl.ds(base + (j + 1) * SC_ROWS, SC_ROWS)], b1, r1)
            ca.wait()
            a0 = pltpu.async_copy(b0, o_hbm.at[i0_v.at[j]], s0)
            a1 = pltpu.async_copy(b0, o_hbm.at[i1_v.at[j]], s0)
            cb.wait()
            e0 = pltpu.async_copy(b1, o_hbm.at[i0_v.at[j + 1]], s1)
            e1 = pltpu.async_copy(b1, o_hbm.at[i1_v.at[j + 1]], s1)
            a0.wait()
            a1.wait()
            e0.wait()
            e1.wait()

    return scatter(x, idx[0], idx[1])


def _sc_gather_rows(table, dest):
    T = dest.shape[1]
    W = table.shape[1]
    tw = T // SC_WORKERS
    nch = tw // SC_ROWS
    assert tw * SC_WORKERS == T and nch * SC_ROWS == tw
    idx = dest.reshape(MOE_TOPK, SC_WORKERS, nch, SC_ROWS)
    out = jax.ShapeDtypeStruct((T, W), U32)

    @functools.partial(pl.kernel, mesh=_sc_mesh(), out_type=[out, out], scratch_types=_sc_scratch(nch, W))
    def gather(tab_hbm, i0_hbm, i1_hbm, o0_hbm, o1_hbm, i0_v, i1_v, b0, b1, g0, g1, w0, w1):
        wid = lax.axis_index("s") * SC_CORES + lax.axis_index("c")
        base = wid * tw
        pltpu.sync_copy(i0_hbm.at[wid], i0_v)
        pltpu.sync_copy(i1_hbm.at[wid], i1_v)

        @pl.loop(0, nch)
        def _(j):
            rows = pl.ds(base + j * SC_ROWS, SC_ROWS)
            c0 = pltpu.async_copy(tab_hbm.at[i0_v.at[j]], b0, g0)
            c1 = pltpu.async_copy(tab_hbm.at[i1_v.at[j]], b1, g1)
            c0.wait()
            d0 = pltpu.async_copy(b0, o0_hbm.at[rows], w0)
            c1.wait()
            d1 = pltpu.async_copy(b1, o1_hbm.at[rows], w1)
            d0.wait()
            d1.wait()

    return gather(table, idx[0], idx[1])


def _moe(xn, rpt, w_gate_up, w_down, layer):
    dest, block_e, block_valid, P = _moe_dispatch_plan(rpt)
    xs = _sc_scatter_rows(xn, dest, P)
    y = _moe_experts(xs, block_e, block_valid, w_gate_up, w_down, layer)
    return _sc_gather_rows(y, dest)


def _final_body(h_ref, y0_ref, y1_ref, rp_ref, g_ref, o_ref):
    h = _combine(h_ref[...], y0_ref[...], y1_ref[...], rp_ref[...])
    o_ref[...] = h * lax.rsqrt(jnp.mean(h * h, axis=-1, keepdims=True) + NORM_EPS) * g_ref[...]


def _final(h, y0, y1, rp, gain, tm=FINAL_ROWS):
    T, D = h.shape
    tm = min(tm, T)
    row = lambda i: (i, 0)
    return pl.pallas_call(
        _final_body,
        grid=(T // tm,),
        in_specs=[pl.BlockSpec((tm, D), row), pl.BlockSpec((tm, D // 2), row), pl.BlockSpec((tm, D // 2), row),
                  pl.BlockSpec((tm, LANES), row), pl.BlockSpec((1, D), lambda i: (0, 0))],
        out_specs=pl.BlockSpec((tm, D), row),
        out_shape=jax.ShapeDtypeStruct((T, D), F32),
        compiler_params=_cparams("parallel"),
        name="final_norm",
    )(h, y0, y1, rp, gain.reshape(1, D))


def _rope_tables(n_tok, head_dim):
    t = jnp.arange(n_tok, dtype=jnp.int32)
    row = (t // GRID_W).astype(F32)
    col = (t % GRID_W).astype(F32)
    half = head_dim // 2
    inv = ROPE_THETA ** (-jnp.arange(0, half, 2, dtype=F32) / half)
    ang_r = row[:, None] * inv[None, :]
    ang_c = col[:, None] * inv[None, :]
    ang = jnp.concatenate([ang_r, ang_r, ang_c, ang_c], axis=-1)
    q = half // 2
    sign = jnp.where((jnp.arange(head_dim) % half) < q, -1.0, 1.0).astype(F32)
    return jnp.cos(ang), jnp.sin(ang) * sign[None, :]


def _router_weights(w_group, b_group, w_expert, b_expert):
    D = w_group.shape[0]
    pad = LANES - MOE_EXPERTS - MOE_GROUPS
    w = jnp.concatenate([w_expert, w_group, jnp.zeros((D, pad), F32)], axis=1).astype(F32)
    b = jnp.concatenate([b_expert, b_group, jnp.zeros((pad,), F32)]).reshape(1, LANES)
    w_hi = w.astype(BF16)
    w_lo = (w - w_hi.astype(F32)).astype(BF16)
    return jnp.concatenate([w_hi, w_lo], axis=1), b.astype(F32)


def _trunk(x, p):
    B, S, D = x.shape
    T = B * S
    cos_r, sin_r = _rope_tables(S, RET_QK_DIM)
    cos_a, sin_a = [jnp.take(t, jnp.asarray(ATTN_HEAD_PERM, I32), axis=1) for t in _rope_tables(S, ATTN_HEAD_DIM)]
    h = x.reshape(T, D)

    proj = _in_proj(h, p["ln_mix"][0], p["ret_w_in"][0], cos_r, sin_r, S)
    y = _retention(proj, p["ret_decay_logit"][0], B, S)
    h, xn, rp, rpt = _out_proj(y, h, p["ret_w_out"][0], p["ln_ffn"][0], *p["router"][0], gate_src=proj)
    y0, y1 = _moe(xn, rpt, p["moe_w_gate_up"], p["moe_w_down"], 0)

    h, qn, kn, vx = _attn_in_proj(h, p["ln_mix"][1], p["attn_w_in"][0], (y0, y1, rp),
                                  p["attn_q_gain"][0], p["attn_k_gain"][0], cos_a, sin_a, S)
    y = _attention(qn, kn, vx, B, S)
    h, xn, rp, rpt = _out_proj(y, h, p["attn_w_out"][0], p["ln_ffn"][1], *p["router"][1])
    y0, y1 = _moe(xn, rpt, p["moe_w_gate_up"], p["moe_w_down"], 1)

    return _final(h, y0, y1, rp, p["ln_final"]).reshape(B, S, D)


def _prepare_params(ln_mix, ret_w_in, ret_decay_logit, ret_w_out, attn_w_in, attn_q_gain, attn_k_gain,
                    attn_w_out, ln_ffn, moe_w_group, moe_b_group, moe_w_expert, moe_b_expert, moe_w_gate_up,
                    moe_w_down, ln_final):
    d = ATTN_HEAD_DIM
    perm = jnp.asarray(ATTN_HEAD_PERM, I32)
    qk_cols = (jnp.arange((ATTN_Q_HEADS + ATTN_KV_HEADS) * d, dtype=I32) // d) * d
    qk_cols = qk_cols + jnp.tile(perm, ATTN_Q_HEADS + ATTN_KV_HEADS)
    cols = jnp.concatenate([qk_cols, jnp.arange(qk_cols.shape[0], ATTN_IN_WIDTH, dtype=I32)])
    return {
        "ln_mix": ln_mix, "ln_ffn": ln_ffn, "ln_final": ln_final,
        "ret_w_in": ret_w_in.astype(BF16), "ret_decay_logit": ret_decay_logit, "ret_w_out": ret_w_out.astype(BF16),
        "attn_w_in": jnp.take(attn_w_in, cols, axis=2).astype(BF16),
        "attn_q_gain": jnp.take(attn_q_gain, perm, axis=1), "attn_k_gain": jnp.take(attn_k_gain, perm, axis=1),
        "attn_w_out": attn_w_out.astype(BF16),
        "router": [_router_weights(moe_w_group[i], moe_b_group[i], moe_w_expert[i], moe_b_expert[i])
                   for i in range(moe_w_group.shape[0])],
        "moe_w_gate_up": moe_w_gate_up, "moe_w_down": moe_w_down,
    }


def kernel(x_prompt, x_sample, ln_mix, ret_w_in, ret_decay_logit, ret_w_out, attn_w_in, attn_q_gain, attn_k_gain,
           attn_w_out, ln_ffn, moe_w_group, moe_b_group, moe_w_expert, moe_b_expert, moe_w_gate_up, moe_w_down,
           ln_final):
    p = _prepare_params(ln_mix, ret_w_in, ret_decay_logit, ret_w_out, attn_w_in, attn_q_gain, attn_k_gain,
                        attn_w_out, ln_ffn, moe_w_group, moe_b_group, moe_w_expert, moe_b_expert, moe_w_gate_up,
                        moe_w_down, ln_final)
    return _trunk(x_prompt, p), _trunk(x_sample, p)
```

```python
import functools

import jax
import jax.numpy as jnp
from jax import lax
from jax.experimental import pallas as pl
from jax.experimental.pallas import tpu as pltpu
from jax.experimental.pallas import tpu_sc as plsc

F32 = jnp.float32
BF16 = jnp.bfloat16
U32 = jnp.uint32
I32 = jnp.int32

GRID_W = 64
ROPE_THETA = 10000.0
NORM_EPS = 1e-6

RET_HEADS = 4
RET_QK_DIM = 256
RET_V_DIM = 512
RET_SCAN_CHUNK = 256
RET_QK_WIDTH = RET_HEADS * RET_QK_DIM
RET_V_WIDTH = RET_HEADS * RET_V_DIM

ATTN_Q_HEADS = 8
ATTN_KV_HEADS = 2
ATTN_HEAD_DIM = 128
ATTN_GROUP = ATTN_Q_HEADS // ATTN_KV_HEADS
ATTN_IN_WIDTH = (ATTN_Q_HEADS + 2 * ATTN_KV_HEADS) * ATTN_HEAD_DIM

MOE_GROUPS = 4
MOE_EXPERTS_PER_GROUP = 8
MOE_EXPERTS = MOE_GROUPS * MOE_EXPERTS_PER_GROUP
MOE_TOPK = 2
MOE_FF = 512

LANES = 128
SUBLANES = 8
LOG2_E = 1.4426950408889634
VMEM_LIMIT = 56 * 1024 * 1024

IN_PROJ_ROWS = 512
IN_PROJ_SUB_ROWS = 256
ATTN_IN_PROJ_ROWS = 1024
RET_SEQ_BLOCK = 2048
RET_HEADS_PER_STEP = 2
RET_UNROLL = 8
ATTN_Q_TILE = 1024
ATTN_KV_TILE = 256
ATTN_KV_UNROLL = 16
OUT_PROJ_TILE_ELEMS = 2048 * 1024
OUT_PROJ_SUB_ROWS = 1024
MOE_BLOCK = 512
MOE_STEP_BLOCKS = 2
RANK_TOKENS = 2048
FINAL_ROWS = 1024
SC_ROWS = 64


def _cparams(*sem):
    return pltpu.CompilerParams(dimension_semantics=sem, vmem_limit_bytes=VMEM_LIMIT)


def _pack_bf16_pairs(x):
    w = x.shape[1] // 2
    lo = lax.bitcast_convert_type(x[:, :w].astype(BF16).astype(F32), U32)
    hi = lax.bitcast_convert_type(x[:, w:].astype(BF16).astype(F32), U32)
    return lax.shift_right_logical(lo, jnp.uint32(16)) | (hi & jnp.uint32(0xFFFF0000))


def _unpack_bf16_pairs(u):
    lo = lax.bitcast_convert_type(lax.shift_left(u, jnp.uint32(16)), F32)
    hi = lax.bitcast_convert_type(u & jnp.uint32(0xFFFF0000), F32)
    return lo, hi


def _combine(h, y0_u32, y1_u32, rp):
    lo0, hi0 = _unpack_bf16_pairs(y0_u32)
    lo1, hi1 = _unpack_bf16_pairs(y1_u32)
    w0 = rp[:, 2:3]
    w1 = rp[:, 3:4]
    half = h.shape[1] // 2
    return jnp.concatenate([h[:, :half] + w0 * lo0 + w1 * lo1, h[:, half:] + w0 * hi0 + w1 * hi1], axis=1)


def _rope256(x, cos, sin_signed):
    xr = jnp.concatenate([pltpu.roll(x[:, :LANES], 64, 1), pltpu.roll(x[:, LANES:], 64, 1)], axis=1)
    return x * cos + xr * sin_signed


def _in_proj_body(h_ref, g_ref, w_ref, cos_ref, sin_ref, o_ref):
    tm = h_ref.shape[0]
    sub = min(IN_PROJ_SUB_ROWS, tm)
    dk = RET_QK_DIM
    for s in range(tm // sub):
        rows = slice(s * sub, (s + 1) * sub)
        h = h_ref[rows, :]
        xn = (h * lax.rsqrt(jnp.mean(h * h, axis=-1, keepdims=True) + NORM_EPS) * g_ref[...]).astype(BF16)
        acc = jnp.dot(xn, w_ref[...], preferred_element_type=F32)
        cos = cos_ref[rows, :]
        sin = sin_ref[rows, :]
        for hb in range(2 * RET_HEADS):
            c = slice(hb * dk, (hb + 1) * dk)
            o_ref[rows, c] = _rope256(acc[:, c], cos, sin).astype(o_ref.dtype)
        o_ref[rows, 2 * RET_QK_WIDTH:] = acc[:, 2 * RET_QK_WIDTH:].astype(o_ref.dtype)


def _in_proj(h, gain, w, cos, sin_signed, S, tm=IN_PROJ_ROWS):
    T, D = h.shape
    N = w.shape[1]
    tm = min(tm, S)
    spb = S // tm
    tab = lambda i: (i % spb, 0)
    return pl.pallas_call(
        _in_proj_body,
        grid=(T // tm,),
        in_specs=[pl.BlockSpec((tm, D), lambda i: (i, 0)), pl.BlockSpec((1, D), lambda i: (0, 0)),
                  pl.BlockSpec((D, N), lambda i: (0, 0)),
                  pl.BlockSpec((tm, RET_QK_DIM), tab), pl.BlockSpec((tm, RET_QK_DIM), tab)],
        out_specs=pl.BlockSpec((tm, N), lambda i: (i, 0)),
        out_shape=jax.ShapeDtypeStruct((T, N), BF16),
        compiler_params=_cparams("parallel"),
        name="in_proj",
    )(h, gain.reshape(1, D), w, cos, sin_signed)


def _retention_body(dl_ref, q_ref, k_ref, v_ref, y_ref, state_ref, obwd_ref, *, n_chunks, n_blocks):
    C = RET_SCAN_CHUNK
    hps = RET_HEADS_PER_STEP
    dk, dv = RET_QK_DIM, RET_V_DIM
    hp = pl.program_id(1)
    phase = pl.program_id(2)
    n = pl.program_id(3)
    sb = n_chunks * C

    def log_gamma(s):
        x = jnp.full((1, 1), s, F32)
        return jnp.minimum(x, 0.0) - jnp.log(1.0 + jnp.exp(-jnp.abs(x)))

    lg_f = [log_gamma(dl_ref[hp * hps + i]) for i in range(hps)]
    lg_b = [log_gamma(dl_ref[RET_HEADS + hp * hps + i]) for i in range(hps)]
    pos = lax.broadcasted_iota(jnp.int32, (C, dk), 0).astype(F32)
    k_scale = dk ** -0.5

    @pl.when(n == 0)
    def _():
        state_ref[...] = jnp.zeros_like(state_ref)

    def load(i, rows, q_dec, k_dec):
        q = q_ref[rows, i * dk:(i + 1) * dk]
        k = k_ref[rows, i * dk:(i + 1) * dk]
        return q, k, v_ref[rows, i * dv:(i + 1) * dv], q * q_dec, (k * k_dec).T

    @pl.when(phase == 0)
    def _():
        q_dec = [jnp.exp((C - pos) * lg).astype(BF16) for lg in lg_b]
        k_dec = [(jnp.exp(pos * lg) * k_scale).astype(BF16) for lg in lg_b]
        chunk_dec = [jnp.exp(lg * C) for lg in lg_b]
        base = (n_blocks - 1 - n) * sb

        def body(cc, carry):
            c = n_chunks - 1 - cc
            rows = pl.ds(pl.multiple_of(c * C, C), C)
            orow = pl.ds(pl.multiple_of(base + c * C, C), C)
            for i in range(hps):
                _, _, v, qd, kt = load(i, rows, q_dec[i], k_dec[i])
                st = state_ref[i]
                out = jnp.dot(qd, st.astype(BF16), preferred_element_type=F32)
                state_ref[i] = st * chunk_dec[i] + jnp.dot(kt, v, preferred_element_type=F32)
                obwd_ref[orow, i * dv:(i + 1) * dv] = out.astype(obwd_ref.dtype)
            return carry

        lax.fori_loop(0, n_chunks, body, 0, unroll=min(RET_UNROLL, n_chunks))

    @pl.when(phase == 1)
    def _():
        q_dec = [jnp.exp((pos + 1.0) * lg).astype(BF16) for lg in lg_f]
        k_dec = [(jnp.exp((C - 1.0 - pos) * lg) * k_scale).astype(BF16) for lg in lg_f]
        chunk_dec = [jnp.exp(lg * C) for lg in lg_f]
        ii = lax.broadcasted_iota(jnp.int32, (C, C), 0)
        jj = lax.broadcasted_iota(jnp.int32, (C, C), 1)
        dist = (ii - jj).astype(F32)
        inner = [(jnp.where(dist >= 0, jnp.exp(f * jnp.maximum(dist, 0.0)), 0.0)
                  + jnp.where(dist <= 0, jnp.exp(bk * jnp.maximum(-dist, 0.0)), 0.0)) * k_scale
                 for f, bk in zip(lg_f, lg_b)]
        base = n * sb

        def body(c, carry):
            rows = pl.ds(pl.multiple_of(c * C, C), C)
            orow = pl.ds(pl.multiple_of(base + c * C, C), C)
            for i in range(hps):
                q, k, v, qd, kt = load(i, rows, q_dec[i], k_dec[i])
                scores = lax.dot_general(q, k, (((1,), (1,)), ((), ())), preferred_element_type=F32) * inner[i]
                st = state_ref[i]
                lhs = jnp.concatenate([qd, scores.astype(BF16)], axis=1)
                rhs = jnp.concatenate([st.astype(BF16), v], axis=0)
                o = jnp.dot(lhs, rhs, preferred_element_type=F32)
                state_ref[i] = st * chunk_dec[i] + jnp.dot(kt, v, preferred_element_type=F32)
                o = o + obwd_ref[orow, i * dv:(i + 1) * dv].astype(F32)
                y_ref[rows, i * dv:(i + 1) * dv] = o.astype(y_ref.dtype)
            return carry

        lax.fori_loop(0, n_chunks, body, 0, unroll=min(RET_UNROLL, n_chunks))


def _retention(proj, decay_logit, B, S, sb=RET_SEQ_BLOCK):
    sb = min(sb, S)
    nb = S // sb
    hps = RET_HEADS_PER_STEP
    qw, vw = hps * RET_QK_DIM, hps * RET_V_DIM
    proj3 = proj.reshape(B, S, proj.shape[-1])
    blk = lambda p, n: jnp.where(p == 0, nb - 1 - n, n)
    kq = RET_QK_WIDTH // qw
    kv = 2 * RET_QK_WIDTH // vw
    in_specs = [
        pl.BlockSpec(memory_space=pltpu.SMEM),
        pl.BlockSpec((None, sb, qw), lambda b, h, p, n: (b, blk(p, n), h)),
        pl.BlockSpec((None, sb, qw), lambda b, h, p, n: (b, blk(p, n), kq + h)),
        pl.BlockSpec((None, sb, vw), lambda b, h, p, n: (b, blk(p, n), kv + h)),
    ]
    out = pl.pallas_call(
        functools.partial(_retention_body, n_chunks=sb // RET_SCAN_CHUNK, n_blocks=nb),
        grid=(B, RET_HEADS // hps, 2, nb),
        in_specs=in_specs,
        out_specs=pl.BlockSpec((None, sb, vw), lambda b, h, p, n: (b, n * p, h)),
        out_shape=jax.ShapeDtypeStruct((B, S, RET_V_WIDTH), BF16),
        scratch_shapes=[pltpu.VMEM((hps, RET_QK_DIM, RET_V_DIM), F32), pltpu.VMEM((S, vw), BF16)],
        compiler_params=_cparams("parallel", "parallel", "arbitrary", "arbitrary"),
        name="retention",
    )(decay_logit.reshape(2 * RET_HEADS).astype(F32), proj3, proj3, proj3)
    return out.reshape(B * S, RET_V_WIDTH)


ATTN_HEAD_PERM = tuple(list(range(0, 32)) + list(range(64, 96)) + list(range(32, 64)) + list(range(96, 128)))


def _norm_rope_head_pair(x, gain2, cos2, sin2):
    d = ATTN_HEAD_DIM
    bi = lax.broadcasted_iota(I32, (2 * d, 2 * d), 0) // d
    bj = lax.broadcasted_iota(I32, (2 * d, 2 * d), 1) // d
    ones_bd = jnp.where(bi == bj, 1.0, 0.0).astype(BF16)
    ss = jnp.dot((x * x).astype(BF16), ones_bd, preferred_element_type=F32)
    x = x * lax.rsqrt(ss * (1.0 / d) + NORM_EPS) * gain2
    return _rope256(x, cos2, sin2)


def _attn_in_proj_body(h_ref, y0_ref, y1_ref, rp_ref, g_ref, w_ref, qg_ref, kg_ref, cos_ref, sin_ref,
                       hout_ref, q_ref, k_ref, v_ref):
    d = ATTN_HEAD_DIM
    tm = h_ref.shape[0]
    sub = min(IN_PROJ_SUB_ROWS, tm)
    q_scale = (d ** -0.5) * LOG2_E
    for s in range(tm // sub):
        rows = slice(s * sub, (s + 1) * sub)
        h = _combine(h_ref[rows, :], y0_ref[rows, :], y1_ref[rows, :], rp_ref[rows, :])
        hout_ref[rows, :] = h
        xn = (h * lax.rsqrt(jnp.mean(h * h, axis=-1, keepdims=True) + NORM_EPS) * g_ref[...]).astype(BF16)
        acc = jnp.dot(xn, w_ref[...], preferred_element_type=F32)
        cos2 = jnp.concatenate([cos_ref[rows, :]] * 2, axis=1)
        sin2 = jnp.concatenate([sin_ref[rows, :]] * 2, axis=1)
        qg2 = jnp.concatenate([qg_ref[...]] * 2, axis=1) * q_scale
        kg2 = jnp.concatenate([kg_ref[...]] * 2, axis=1)
        for hp in range(ATTN_Q_HEADS // 2):
            c = slice(2 * hp * d, (2 * hp + 2) * d)
            q = _norm_rope_head_pair(acc[:, c], qg2, cos2, sin2).astype(q_ref.dtype)
            q_ref[2 * hp, rows, :] = q[:, :d]
            q_ref[2 * hp + 1, rows, :] = q[:, d:]
        for hp in range(ATTN_KV_HEADS // 2):
            c0 = (ATTN_Q_HEADS + 2 * hp) * d
            k = _norm_rope_head_pair(acc[:, c0:c0 + 2 * d], kg2, cos2, sin2)
            k_ref[rows, 2 * hp * d:(2 * hp + 2) * d] = k.astype(k_ref.dtype)
        for hk in range(ATTN_KV_HEADS):
            c1 = (ATTN_Q_HEADS + ATTN_KV_HEADS + hk) * d
            v_ref[rows, 2 * hk * d:(2 * hk + 1) * d] = acc[:, c1:c1 + d].astype(v_ref.dtype)
            v_ref[rows, (2 * hk + 1) * d:(2 * hk + 2) * d] = jnp.ones((sub, d), v_ref.dtype)


def _attn_in_proj(h, gain, w, comb, q_gain, k_gain, cos, sin_signed, S, tm=ATTN_IN_PROJ_ROWS):
    T, D = h.shape
    N = w.shape[1]
    tm = min(tm, S)
    spb = S // tm
    d = ATTN_HEAD_DIM
    y0, y1, rp = comb
    row = lambda i: (i, 0)
    const = lambda i: (0, 0)
    tab = lambda i: (i % spb, 0)
    return pl.pallas_call(
        _attn_in_proj_body,
        grid=(T // tm,),
        in_specs=[pl.BlockSpec((tm, D), row), pl.BlockSpec((tm, D // 2), row), pl.BlockSpec((tm, D // 2), row),
                  pl.BlockSpec((tm, LANES), row), pl.BlockSpec((1, D), const), pl.BlockSpec((D, N), const),
                  pl.BlockSpec((1, d), const), pl.BlockSpec((1, d), const),
                  pl.BlockSpec((tm, d), tab), pl.BlockSpec((tm, d), tab)],
        out_specs=[pl.BlockSpec((tm, D), row), pl.BlockSpec((ATTN_Q_HEADS, tm, d), lambda i: (0, i, 0)),
                   pl.BlockSpec((tm, ATTN_KV_HEADS * d), row), pl.BlockSpec((tm, 2 * ATTN_KV_HEADS * d), row)],
        out_shape=[jax.ShapeDtypeStruct((T, D), F32), jax.ShapeDtypeStruct((ATTN_Q_HEADS, T, d), BF16),
                   jax.ShapeDtypeStruct((T, ATTN_KV_HEADS * d), BF16),
                   jax.ShapeDtypeStruct((T, 2 * ATTN_KV_HEADS * d), BF16)],
        compiler_params=_cparams("parallel"),
        name="attn_in_proj",
    )(h, y0, y1, rp, gain.reshape(1, D), w, q_gain.reshape(1, d), k_gain.reshape(1, d), cos, sin_signed)


def _attn_body(q_ref, k_ref, v_ref, o_ref, m_ref, acc_ref, *, tq, tk, n_kv):
    d = ATTN_HEAD_DIM
    m_ref[...] = jnp.full_like(m_ref, -jnp.inf)
    acc_ref[...] = jnp.zeros_like(acc_ref)
    rep = tk // LANES

    def step(c, carry):
        rows = pl.ds(pl.multiple_of(c * tk, tk), tk)
        k = k_ref[rows, :]
        v = v_ref[rows, :]
        for g in range(ATTN_GROUP):
            r = slice(g * tq, (g + 1) * tq)
            s = lax.dot_general(q_ref[g], k, (((1,), (1,)), ((), ())), preferred_element_type=F32)
            m_prev = m_ref[r, :]
            m_new = jnp.maximum(m_prev, jnp.max(s, axis=-1, keepdims=True))
            alpha = jnp.exp2(m_prev - m_new)
            p = jnp.exp2(s - jnp.concatenate([m_new] * rep, axis=1))
            pv = jnp.dot(p.astype(BF16), v, preferred_element_type=F32)
            acc_ref[r, :] = jnp.concatenate([alpha, alpha], axis=1) * acc_ref[r, :] + pv
            m_ref[r, :] = m_new
        return carry

    lax.fori_loop(0, n_kv, step, 0, unroll=min(ATTN_KV_UNROLL, n_kv))
    for g in range(ATTN_GROUP):
        r = slice(g * tq, (g + 1) * tq)
        o_ref[:, g * d:(g + 1) * d] = (acc_ref[r, :d] / acc_ref[r, d:]).astype(o_ref.dtype)


def _attention(qn, kn, vx, B, S, tq=ATTN_Q_TILE, tk=ATTN_KV_TILE):
    d = ATTN_HEAD_DIM
    tq = min(tq, S)
    tk = min(tk, S)
    gw = ATTN_GROUP * d
    nq = S // tq
    k3 = kn.reshape(B, S, ATTN_KV_HEADS * d)
    v3 = vx.reshape(B, S, 2 * ATTN_KV_HEADS * d)
    m = ATTN_GROUP * tq
    out = pl.pallas_call(
        functools.partial(_attn_body, tq=tq, tk=tk, n_kv=S // tk),
        grid=(B, ATTN_KV_HEADS, S // tq),
        in_specs=[pl.BlockSpec((ATTN_GROUP, tq, d), lambda b, kh, i: (kh, b * nq + i, 0)),
                  pl.BlockSpec((None, S, d), lambda b, kh, i: (b, 0, kh)),
                  pl.BlockSpec((None, S, 2 * d), lambda b, kh, i: (b, 0, kh))],
        out_specs=pl.BlockSpec((None, tq, gw), lambda b, kh, i: (b, i, kh)),
        out_shape=jax.ShapeDtypeStruct((B, S, ATTN_Q_HEADS * d), BF16),
        scratch_shapes=[pltpu.VMEM((m, LANES), F32), pltpu.VMEM((m, 2 * d), F32)],
        compiler_params=_cparams("parallel", "parallel", "arbitrary"),
        name="attention",
    )(qn, k3, v3)
    return out.reshape(B * S, ATTN_Q_HEADS * d)


ROUTER_GROUP_ROW = MOE_EXPERTS
assert MOE_EXPERTS_PER_GROUP == SUBLANES and MOE_GROUPS <= SUBLANES


def _route(logits_t):
    tm = logits_t.shape[1]
    row = lax.broadcasted_iota(jnp.int32, (SUBLANES, tm), 0)
    neg = jnp.float32(-jnp.inf)
    big = jnp.int32(SUBLANES)

    def first_argmax(x):
        mx = jnp.max(x, axis=0, keepdims=True)
        return mx, jnp.min(jnp.where(x == mx, row, big), axis=0, keepdims=True)

    gl = jnp.where(row < MOE_GROUPS, logits_t[ROUTER_GROUP_ROW:ROUTER_GROUP_ROW + SUBLANES, :], neg)
    gmax, g_sel = first_argmax(gl)
    g_w = 1.0 / jnp.sum(jnp.exp(gl - gmax), axis=0, keepdims=True)
    el = logits_t[0:SUBLANES, :]
    for g in range(1, MOE_GROUPS):
        el = jnp.where(g_sel == g, logits_t[g * SUBLANES:(g + 1) * SUBLANES, :], el)
    v1, i1 = first_argmax(el)
    v2, i2 = first_argmax(jnp.where(row == i1, neg, el))
    e2 = jnp.exp(v2 - v1)
    w1 = g_w / (1.0 + e2)
    w2 = g_w * e2 / (1.0 + e2)
    base = g_sel * MOE_EXPERTS_PER_GROUP
    return jnp.where(row == 0, (base + i1).astype(F32),
                     jnp.where(row == 1, (base + i2).astype(F32),
                               jnp.where(row == 2, w1, jnp.where(row == 3, w2, 0.0))))


def _retention_gate(o, g):
    dv = RET_V_DIM
    parts = []
    for hh in range(RET_HEADS):
        oh = o[:, hh * dv:(hh + 1) * dv].astype(F32)
        oh = oh * lax.rsqrt(jnp.mean(oh * oh, axis=-1, keepdims=True) + NORM_EPS)
        gh = g[:, hh * dv:(hh + 1) * dv]
        parts.append(gh * jax.nn.sigmoid(gh) * oh.astype(BF16))
    return jnp.concatenate(parts, axis=1)


def _out_proj_body(gated, y_ref, *refs):
    if gated:
        gate_ref, refs = refs[0], refs[1:]
    h_ref, w_ref, g_ref, wr_ref, br_ref, hout_ref, xn_ref, rp_ref, rpt_ref = refs
    tm = h_ref.shape[0]
    sub = min(OUT_PROJ_SUB_ROWS, tm)
    wr = wr_ref[...]
    for s in range(tm // sub):
        rows = slice(s * sub, (s + 1) * sub)
        y = _retention_gate(y_ref[rows, :], gate_ref[rows, :]) if gated else y_ref[rows, :]
        h = h_ref[rows, :] + jnp.dot(y, w_ref[...], preferred_element_type=F32)
        hout_ref[rows, :] = h
        xn = h * lax.rsqrt(jnp.mean(h * h, axis=-1, keepdims=True) + NORM_EPS) * g_ref[...]
        xn_ref[rows, :] = _pack_bf16_pairs(xn)
        xn_hi = xn.astype(BF16)
        xn_lo = (xn - xn_hi.astype(F32)).astype(BF16)
        hi = jnp.dot(xn_hi, wr, preferred_element_type=F32)
        lo = jnp.dot(xn_lo, wr[:, :LANES], preferred_element_type=F32)
        logits = hi[:, :LANES] + hi[:, LANES:] + lo + br_ref[...]
        rpt = _route(logits.T)
        rpt_ref[:, rows] = rpt
        full = jnp.concatenate([rpt, jnp.zeros((LANES - SUBLANES, sub), F32)], axis=0)
        rp_ref[rows, :] = full.T


def _out_proj(y, h, w, gain, w_router, b_router, gate_src=None, tm=None):
    T, K = y.shape
    D = h.shape[1]
    tm = min(OUT_PROJ_TILE_ELEMS // K if tm is None else tm, T)
    row = lambda i: (i, 0)
    const = lambda i: (0, 0)
    in_specs = [pl.BlockSpec((tm, K), row)]
    args = [y]
    if gate_src is not None:
        assert K == RET_V_WIDTH and (2 * RET_QK_WIDTH + RET_V_WIDTH) % K == 0
        gate_col = (2 * RET_QK_WIDTH + RET_V_WIDTH) // K
        in_specs.append(pl.BlockSpec((tm, K), lambda i: (i, gate_col)))
        args.append(gate_src)
    in_specs += [pl.BlockSpec((tm, D), row), pl.BlockSpec((K, D), const), pl.BlockSpec((1, D), const),
                 pl.BlockSpec((D, 2 * LANES), const), pl.BlockSpec((1, LANES), const)]
    args += [h, w, gain.reshape(1, D), w_router, b_router]
    return pl.pallas_call(
        functools.partial(_out_proj_body, gate_src is not None),
        grid=(T // tm,),
        in_specs=in_specs,
        out_specs=[pl.BlockSpec((tm, D), row), pl.BlockSpec((tm, D // 2), row), pl.BlockSpec((tm, LANES), row),
                   pl.BlockSpec((SUBLANES, tm), lambda i: (0, i))],
        out_shape=[jax.ShapeDtypeStruct((T, D), F32), jax.ShapeDtypeStruct((T, D // 2), U32),
                   jax.ShapeDtypeStruct((T, LANES), F32), jax.ShapeDtypeStruct((SUBLANES, T), F32)],
        compiler_params=_cparams("parallel"),
        name="out_proj",
    )(*args)


def _moe_body(be_ref, bv_ref, x_ref, *refs):
    nb = MOE_STEP_BLOCKS
    wgu_refs, wd_refs = refs[:nb], refs[nb:2 * nb]
    y_ref = refs[2 * nb]
    wgu_bf_refs, wd_bf_refs = refs[2 * nb + 1:3 * nb + 1], refs[3 * nb + 1:]
    s = pl.program_id(0)

    for j in range(nb):
        b = s * nb + j

        @pl.when((s == 0) | (be_ref[b] != be_ref[jnp.maximum(b - nb, 0)]))
        def _():
            wgu_bf_refs[j][...] = wgu_refs[j][...].astype(BF16)
            wd_bf_refs[j][...] = wd_refs[j][...].astype(BF16)

    @pl.when(bv_ref[s * nb] != 0)
    def _():
        for j in range(nb):
            rows = slice(j * MOE_BLOCK, (j + 1) * MOE_BLOCK)
            lo, hi = _unpack_bf16_pairs(x_ref[rows, :])
            x = jnp.concatenate([lo.astype(BF16), hi.astype(BF16)], axis=1)
            gu = jnp.dot(x, wgu_bf_refs[j][...], preferred_element_type=F32)
            gate = gu[:, :MOE_FF]
            up = gu[:, MOE_FF:]
            act = (gate * jax.nn.sigmoid(gate) * up).astype(BF16)
            y_ref[rows, :] = _pack_bf16_pairs(jnp.dot(act, wd_bf_refs[j][...], preferred_element_type=F32))

    @pl.when(bv_ref[s * nb] == 0)
    def _():
        y_ref[...] = jnp.zeros_like(y_ref)


def _moe_experts(xs, block_e, block_valid, w_gate_up, w_down, layer):
    P, W = xs.shape
    D = 2 * W
    nb = MOE_STEP_BLOCKS
    n_steps = P // (nb * MOE_BLOCK)
    rows = lambda s, be, bv: (s, 0)
    wspec = lambda shape, j: pl.BlockSpec((None, None) + shape, lambda s, be, bv: (layer, be[s * nb + j], 0, 0))
    grid_spec = pltpu.PrefetchScalarGridSpec(
        num_scalar_prefetch=2,
        grid=(n_steps,),
        in_specs=([pl.BlockSpec((nb * MOE_BLOCK, W), rows)]
                  + [wspec((D, 2 * MOE_FF), j) for j in range(nb)] + [wspec((MOE_FF, D), j) for j in range(nb)]),
        out_specs=pl.BlockSpec((nb * MOE_BLOCK, W), rows),
        scratch_shapes=([pltpu.VMEM((D, 2 * MOE_FF), BF16)] * nb + [pltpu.VMEM((MOE_FF, D), BF16)] * nb),
    )
    return pl.pallas_call(
        _moe_body,
        grid_spec=grid_spec,
        out_shape=jax.ShapeDtypeStruct((P, W), U32),
        compiler_params=_cparams("arbitrary"),
        name="moe_experts",
    )(block_e, block_valid, xs, *([w_gate_up] * nb), *([w_down] * nb))


def _rank_body(rpt_ref, tri_ref, rank_ref, counts_ref, carry_ref, *, n_groups):
    @pl.when(pl.program_id(0) == 0)
    def _():
        carry_ref[...] = jnp.zeros_like(carry_ref)

    erow = lax.broadcasted_iota(I32, (MOE_EXPERTS, LANES), 0)
    carry = carry_ref[...]
    tri = tri_ref[...]
    r0, r1 = [], []
    for g in range(n_groups):
        c = slice(g * LANES, (g + 1) * LANES)
        oh0 = erow == rpt_ref[0:1, c].astype(I32)
        oh1 = erow == rpt_ref[1:2, c].astype(I32)
        both = jnp.where(oh0 | oh1, 1.0, 0.0)
        before = jnp.dot(both.astype(BF16), tri, preferred_element_type=F32) + carry
        r0.append(jnp.sum(jnp.where(oh0, before, 0.0), axis=0, keepdims=True))
        r1.append(jnp.sum(jnp.where(oh1, before, 0.0), axis=0, keepdims=True))
        carry = carry + jnp.sum(both, axis=1, keepdims=True)
    carry_ref[...] = carry
    counts_ref[...] = carry.astype(I32)
    tm = n_groups * LANES
    rank = jnp.concatenate([jnp.concatenate(r0, axis=1), jnp.concatenate(r1, axis=1),
                            jnp.zeros((SUBLANES - MOE_TOPK, tm), F32)], axis=0)
    rank_ref[...] = rank.astype(I32)


def _slot_ranks(rpt, tm=RANK_TOKENS):
    T = rpt.shape[1]
    tm = min(tm, T)
    tri = (lax.broadcasted_iota(I32, (LANES, LANES), 0) < lax.broadcasted_iota(I32, (LANES, LANES), 1)).astype(BF16)
    return pl.pallas_call(
        functools.partial(_rank_body, n_groups=tm // LANES),
        grid=(T // tm,),
        in_specs=[pl.BlockSpec((SUBLANES, tm), lambda i: (0, i)), pl.BlockSpec((LANES, LANES), lambda i: (0, 0))],
        out_specs=[pl.BlockSpec((SUBLANES, tm), lambda i: (0, i)),
                   pl.BlockSpec((MOE_EXPERTS, LANES), lambda i: (0, 0))],
        out_shape=[jax.ShapeDtypeStruct((SUBLANES, T), I32), jax.ShapeDtypeStruct((MOE_EXPERTS, LANES), I32)],
        scratch_shapes=[pltpu.VMEM((MOE_EXPERTS, LANES), F32)],
        compiler_params=_cparams("arbitrary"),
        name="slot_ranks",
    )(rpt, tri)


def _moe_dispatch_plan(rpt):
    T = rpt.shape[1]
    A = T * MOE_TOPK
    n_blocks = -(-(A + MOE_EXPERTS * (MOE_BLOCK - 1)) // MOE_BLOCK)
    n_blocks = -(-n_blocks // MOE_STEP_BLOCKS) * MOE_STEP_BLOCKS
    rank, counts = _slot_ranks(rpt)
    counts = counts[:, 0]
    padded = ((counts + MOE_BLOCK - 1) // MOE_BLOCK) * MOE_BLOCK
    pends = jnp.cumsum(padded)
    pstarts = pends - padded
    e = rpt[:MOE_TOPK].astype(I32)
    experts = jnp.arange(MOE_EXPERTS, dtype=I32)[:, None, None]
    dest = jnp.sum(jnp.where(e[None] == experts, pstarts[:, None, None], 0), axis=0) + rank[:MOE_TOPK]
    block_start = jnp.arange(n_blocks, dtype=I32) * MOE_BLOCK
    block_e = jnp.minimum(jnp.sum(pends[None, :] <= block_start[:, None], axis=1), MOE_EXPERTS - 1).astype(I32)
    block_valid = (block_start < pends[-1]).astype(I32)
    return dest, block_e, block_valid, n_blocks * MOE_BLOCK


SC_CORES = 2
SC_SUBCORES = 16
SC_WORKERS = SC_CORES * SC_SUBCORES


def _sc_mesh():
    return plsc.VectorSubcoreMesh(core_axis_name="c", subcore_axis_name="s")


def _sc_scratch(n_chunks, width):
    return [pltpu.VMEM((n_chunks, SC_ROWS), I32), pltpu.VMEM((n_chunks, SC_ROWS), I32),
            pltpu.VMEM((SC_ROWS, width), U32), pltpu.VMEM((SC_ROWS, width), U32),
            pltpu.SemaphoreType.DMA, pltpu.SemaphoreType.DMA, pltpu.SemaphoreType.DMA, pltpu.SemaphoreType.DMA]


def _sc_scatter_rows(x, dest, P):
    T, W = x.shape
    tw = T // SC_WORKERS
    nch = tw // SC_ROWS
    assert tw * SC_WORKERS == T and nch * SC_ROWS == tw and nch % 2 == 0
    idx = dest.reshape(MOE_TOPK, SC_WORKERS, nch, SC_ROWS)

    @functools.partial(pl.kernel, mesh=_sc_mesh(), out_type=jax.ShapeDtypeStruct((P, W), U32),
                       scratch_types=_sc_scratch(nch, W))
    def scatter(x_hbm, i0_hbm, i1_hbm, o_hbm, i0_v, i1_v, b0, b1, r0, r1, s0, s1):
        wid = lax.axis_index("s") * SC_CORES + lax.axis_index("c")
        base = wid * tw
        pltpu.sync_copy(i0_hbm.at[wid], i0_v)
        pltpu.sync_copy(i1_hbm.at[wid], i1_v)

        @pl.loop(0, nch, step=2)
        def _(j):
            ca = pltpu.async_copy(x_hbm.at[pl.ds(base + j * SC_ROWS, SC_ROWS)], b0, r0)
            cb = pltpu.async_copy(x_hbm.at[pl.ds(base + (j + 1) * SC_ROWS, SC_ROWS)], b1, r1)
            ca.wait()
            a0 = pltpu.async_copy(b0, o_hbm.at[i0_v.at[j]], s0)
            a1 = pltpu.async_copy(b0, o_hbm.at[i1_v.at[j]], s0)
            cb.wait()
            e0 = pltpu.async_copy(b1, o_hbm.at[i0_v.at[j + 1]], s1)
            e1 = pltpu.async_copy(b1, o_hbm.at[i1_v.at[j + 1]], s1)
            a0.wait()
            a1.wait()
            e0.wait()
            e1.wait()

    return scatter(x, idx[0], idx[1])


def _sc_gather_rows(table, dest):
    T = dest.shape[1]
    W = table.shape[1]
    tw = T // SC_WORKERS
    nch = tw // SC_ROWS
    assert tw * SC_WORKERS == T and nch * SC_ROWS == tw
    idx = dest.reshape(MOE_TOPK, SC_WORKERS, nch, SC_ROWS)
    out = jax.ShapeDtypeStruct((T, W), U32)

    @functools.partial(pl.kernel, mesh=_sc_mesh(), out_type=[out, out], scratch_types=_sc_scratch(nch, W))
    def gather(tab_hbm, i0_hbm, i1_hbm, o0_hbm, o1_hbm, i0_v, i1_v, b0, b1, g0, g1, w0, w1):
        wid = lax.axis_index("s") * SC_CORES + lax.axis_index("c")
        base = wid * tw
        pltpu.sync_copy(i0_hbm.at[wid], i0_v)
        pltpu.sync_copy(i1_hbm.at[wid], i1_v)

        @pl.loop(0, nch)
        def _(j):
            rows = pl.ds(base + j * SC_ROWS, SC_ROWS)
            c0 = pltpu.async_copy(tab_hbm.at[i0_v.at[j]], b0, g0)
            c1 = pltpu.async_copy(tab_hbm.at[i1_v.at[j]], b1, g1)
            c0.wait()
            d0 = pltpu.async_copy(b0, o0_hbm.at[rows], w0)
            c1.wait()
            d1 = pltpu.async_copy(b1, o1_hbm.at[rows], w1)
            d0.wait()
            d1.wait()

    return gather(table, idx[0], idx[1])


def _moe(xn, rpt, w_gate_up, w_down, layer):
    dest, block_e, block_valid, P = _moe_dispatch_plan(rpt)
    xs = _sc_scatter_rows(xn, dest, P)
    y = _moe_experts(xs, block_e, block_valid, w_gate_up, w_down, layer)
    return _sc_gather_rows(y, dest)


def _final_body(h_ref, y0_ref, y1_ref, rp_ref, g_ref, o_ref):
    h = _combine(h_ref[...], y0_ref[...], y1_ref[...], rp_ref[...])
    o_ref[...] = h * lax.rsqrt(jnp.mean(h * h, axis=-1, keepdims=True) + NORM_EPS) * g_ref[...]


def _final(h, y0, y1, rp, gain, tm=FINAL_ROWS):
    T, D = h.shape
    tm = min(tm, T)
    row = lambda i: (i, 0)
    return pl.pallas_call(
        _final_body,
        grid=(T // tm,),
        in_specs=[pl.BlockSpec((tm, D), row), pl.BlockSpec((tm, D // 2), row), pl.BlockSpec((tm, D // 2), row),
                  pl.BlockSpec((tm, LANES), row), pl.BlockSpec((1, D), lambda i: (0, 0))],
        out_specs=pl.BlockSpec((tm, D), row),
        out_shape=jax.ShapeDtypeStruct((T, D), F32),
        compiler_params=_cparams("parallel"),
        name="final_norm",
    )(h, y0, y1, rp, gain.reshape(1, D))


def _rope_tables(n_tok, head_dim):
    t = jnp.arange(n_tok, dtype=jnp.int32)
    row = (t // GRID_W).astype(F32)
    col = (t % GRID_W).astype(F32)
    half = head_dim // 2
    inv = ROPE_THETA ** (-jnp.arange(0, half, 2, dtype=F32) / half)
    ang_r = row[:, None] * inv[None, :]
    ang_c = col[:, None] * inv[None, :]
    ang = jnp.concatenate([ang_r, ang_r, ang_c, ang_c], axis=-1)
    q = half // 2
    sign = jnp.where((jnp.arange(head_dim) % half) < q, -1.0, 1.0).astype(F32)
    return jnp.cos(ang), jnp.sin(ang) * sign[None, :]


def _router_weights(w_group, b_group, w_expert, b_expert):
    D = w_group.shape[0]
    pad = LANES - MOE_EXPERTS - MOE_GROUPS
    w = jnp.concatenate([w_expert, w_group, jnp.zeros((D, pad), F32)], axis=1).astype(F32)
    b = jnp.concatenate([b_expert, b_group, jnp.zeros((pad,), F32)]).reshape(1, LANES)
    w_hi = w.astype(BF16)
    w_lo = (w - w_hi.astype(F32)).astype(BF16)
    return jnp.concatenate([w_hi, w_lo], axis=1), b.astype(F32)


def _trunk(x, p):
    B, S, D = x.shape
    T = B * S
    cos_r, sin_r = _rope_tables(S, RET_QK_DIM)
    cos_a, sin_a = [jnp.take(t, jnp.asarray(ATTN_HEAD_PERM, I32), axis=1) for t in _rope_tables(S, ATTN_HEAD_DIM)]
    h = x.reshape(T, D)

    proj = _in_proj(h, p["ln_mix"][0], p["ret_w_in"][0], cos_r, sin_r, S)
    y = _retention(proj, p["ret_decay_logit"][0], B, S)
    h, xn, rp, rpt = _out_proj(y, h, p["ret_w_out"][0], p["ln_ffn"][0], *p["router"][0], gate_src=proj)
    y0, y1 = _moe(xn, rpt, p["moe_w_gate_up"], p["moe_w_down"], 0)

    h, qn, kn, vx = _attn_in_proj(h, p["ln_mix"][1], p["attn_w_in"][0], (y0, y1, rp),
                                  p["attn_q_gain"][0], p["attn_k_gain"][0], cos_a, sin_a, S)
    y = _attention(qn, kn, vx, B, S)
    h, xn, rp, rpt = _out_proj(y, h, p["attn_w_out"][0], p["ln_ffn"][1], *p["router"][1])
    y0, y1 = _moe(xn, rpt, p["moe_w_gate_up"], p["moe_w_down"], 1)

    return _final(h, y0, y1, rp, p["ln_final"]).reshape(B, S, D)


def _prepare_params(ln_mix, ret_w_in, ret_decay_logit, ret_w_out, attn_w_in, attn_q_gain, attn_k_gain,
                    attn_w_out, ln_ffn, moe_w_group, moe_b_group, moe_w_expert, moe_b_expert, moe_w_gate_up,
                    moe_w_down, ln_final):
    d = ATTN_HEAD_DIM
    perm = jnp.asarray(ATTN_HEAD_PERM, I32)
    qk_cols = (jnp.arange((ATTN_Q_HEADS + ATTN_KV_HEADS) * d, dtype=I32) // d) * d
    qk_cols = qk_cols + jnp.tile(perm, ATTN_Q_HEADS + ATTN_KV_HEADS)
    cols = jnp.concatenate([qk_cols, jnp.arange(qk_cols.shape[0], ATTN_IN_WIDTH, dtype=I32)])
    return {
        "ln_mix": ln_mix, "ln_ffn": ln_ffn, "ln_final": ln_final,
        "ret_w_in": ret_w_in.astype(BF16), "ret_decay_logit": ret_decay_logit, "ret_w_out": ret_w_out.astype(BF16),
        "attn_w_in": jnp.take(attn_w_in, cols, axis=2).astype(BF16),
        "attn_q_gain": jnp.take(attn_q_gain, perm, axis=1), "attn_k_gain": jnp.take(attn_k_gain, perm, axis=1),
        "attn_w_out": attn_w_out.astype(BF16),
        "router": [_router_weights(moe_w_group[i], moe_b_group[i], moe_w_expert[i], moe_b_expert[i])
                   for i in range(moe_w_group.shape[0])],
        "moe_w_gate_up": moe_w_gate_up, "moe_w_down": moe_w_down,
    }


def kernel(x_prompt, x_sample, ln_mix, ret_w_in, ret_decay_logit, ret_w_out, attn_w_in, attn_q_gain, attn_k_gain,
           attn_w_out, ln_ffn, moe_w_group, moe_b_group, moe_w_expert, moe_b_expert, moe_w_gate_up, moe_w_down,
           ln_final):
    p = _prepare_params(ln_mix, ret_w_in, ret_decay_logit, ret_w_out, attn_w_in, attn_q_gain, attn_k_gain,
                        attn_w_out, ln_ffn, moe_w_group, moe_b_group, moe_w_expert, moe_b_expert, moe_w_gate_up,
                        moe_w_down, ln_final)
    return _trunk(x_prompt, p), _trunk(x_sample, p)
```
